```python
import jax, jax.numpy as jnp
from jax import lax
import numpy as np

D_MODEL = 1024
BATCH = 4
SEQ = 4096
DEPTH = 2

CTX_LEN = 256
GRID_W = 64

A_HEADS = 4
A_DK = 128
A_DV = 128
A_WIDTH = A_HEADS * A_DV
A_CHUNK = 64
B_GROUPS = 4
B_GW = 64
B_WIDTH = B_GROUPS * B_GW
B_CHUNK = 128
C_GROUPS = 4
C_WIDTH = 256
C_KERNEL = 31
MIX_WIDTH = A_WIDTH + B_WIDTH + C_WIDTH
IN_COLS = 5 * A_WIDTH + 2 * B_WIDTH + 2 * C_WIDTH
SPLITS = (A_WIDTH, 2 * A_WIDTH, 3 * A_WIDTH, 4 * A_WIDTH, 5 * A_WIDTH,
          5 * A_WIDTH + B_WIDTH, 5 * A_WIDTH + 2 * B_WIDTH, 5 * A_WIDTH + 2 * B_WIDTH + C_WIDTH)
D_FF = -(-8 * D_MODEL // (3 * 256)) * 256
EPS = 1e-6

kernel_name = 'hybrid_hgrn2_gmlp_conformer_dit'


def _rms_norm(x, g):
    x32 = x.astype(jnp.float32)
    y = x32 * lax.rsqrt(jnp.mean(x32 * x32, axis=-1, keepdims=True) + EPS)
    return (y * g.astype(jnp.float32)).astype(x.dtype)


def _group_layer_norm(x, g, b, groups):
    shp = x.shape
    x32 = x.astype(jnp.float32).reshape(shp[:-1] + (groups, shp[-1] // groups))
    mu = jnp.mean(x32, axis=-1, keepdims=True)
    var = jnp.mean(jnp.square(x32 - mu), axis=-1, keepdims=True)
    y = ((x32 - mu) * lax.rsqrt(var + EPS)).reshape(shp)
    return (y * g.astype(jnp.float32) + b.astype(jnp.float32)).astype(x.dtype)


def _modulate(h, shift, scale):
    return h * (1 + scale) + shift


def _hgrn2_chunk_scan(q, log_f, k, v, s0):
    bsz, t_len, h, _ = q.shape
    dv = v.shape[-1]
    n = t_len // A_CHUNK

    def to_chunks(a):
        return a.reshape(bsz, n, A_CHUNK, h, a.shape[-1]).transpose(1, 0, 3, 2, 4)

    mask = jnp.tril(jnp.ones((A_CHUNK, A_CHUNK), dtype=bool))[:, :, None]

    def step(s, inp):
        qc, lfc, kc, vc = inp
        b = jnp.cumsum(lfc, axis=2)
        inter = jnp.einsum('bhtk,bhkv->bhtv', qc * jnp.exp(b), s)
        diff = b[:, :, :, None, :] - b[:, :, None, :, :]
        decay = jnp.where(mask, jnp.exp(jnp.where(mask, diff, 0.0)), 0.0)
        scores = jnp.einsum('bhtk,bhtsk,bhsk->bhts', qc, decay, kc)
        intra = jnp.einsum('bhts,bhsv->bhtv', scores, vc)
        b_last = b[:, :, -1, :]
        k_dec = kc * jnp.exp(b_last[:, :, None, :] - b)
        s_new = jnp.exp(b_last)[..., None] * s + jnp.einsum('bhsk,bhsv->bhkv', k_dec, vc)
        return s_new, inter + intra

    s_t, o = lax.scan(step, s0, (to_chunks(q), to_chunks(log_f), to_chunks(k), to_chunks(v)))
    o = o.transpose(1, 0, 3, 2, 4).reshape(bsz, t_len, h, dv)
    return o, s_t


def _bidir_hgrn2(q, zf_fwd, zf_bwd, vi, lb, s0_f, s0_b):
    bsz, t_len, _ = q.shape
    lb = lb.astype(jnp.float32)

    def heads(a):
        return a.astype(jnp.float32).reshape(bsz, t_len, A_HEADS, -1)

    def gates(z, lb_d):
        z32 = z.astype(jnp.float32)
        f = lb_d + (1.0 - lb_d) * jax.nn.sigmoid(z32)
        k = (1.0 - lb_d) * jax.nn.sigmoid(-z32)
        return heads(jnp.log(f)), heads(k)

    qh, vh = heads(q), heads(vi)
    lf_f, k_f = gates(zf_fwd, lb[0])
    lf_b, k_b = gates(zf_bwd, lb[1])
    o_f, s_f = _hgrn2_chunk_scan(qh, lf_f, k_f, vh, s0_f)
    o_b, s_b = _hgrn2_chunk_scan(qh[:, ::-1], lf_b[:, ::-1], k_b[:, ::-1], vh[:, ::-1], s0_b)
    return o_f + o_b[:, ::-1], s_f, s_b


def _chunk_token_mlp(u, v, ln_g, ln_b, w_s, b_s):
    bsz, t_len, _ = u.shape
    n = t_len // B_CHUNK
    u = jax.nn.gelu(u)
    v = _group_layer_norm(jax.nn.gelu(v), ln_g, ln_b, B_GROUPS)
    vg = v.reshape(bsz, n, B_CHUNK, B_GROUPS, B_GW)
    mixed = jnp.einsum('gts,bnsgd->bntgd', w_s, vg) + b_s.T[:, :, None]
    return u * mixed.reshape(bsz, t_len, B_WIDTH)


def _conv_module(a, gate, dw_w, dw_b, ln_g, ln_b, pw_w, pw_b, n_seg):
    z = a * jax.nn.sigmoid(gate)
    bsz, t_len, ch = z.shape
    zs = z.reshape(bsz * n_seg, t_len // n_seg, ch)
    zs = lax.conv_general_dilated(zs, dw_w[:, None, :], window_strides=(1,),
                                  padding=[(C_KERNEL // 2, C_KERNEL // 2)],
                                  dimension_numbers=('NWC', 'WIO', 'NWC'),
                                  feature_group_count=ch)
    z = zs.reshape(bsz, t_len, ch) + dw_b
    z = jax.nn.silu(_group_layer_norm(z, ln_g, ln_b, C_GROUPS))
    return z @ pw_w + pw_b


def _token_mixer(p, lw, lb, s0_f, s0_b, n_seg):
    q, zf, zb, vi, g, u, v, ca, cg = jnp.split(p, SPLITS, axis=-1)
    bsz, t_len, _ = p.shape
    o, s_f, s_b = _bidir_hgrn2(q, zf, zb, vi, lb, s0_f, s0_b)
    o = o * lax.rsqrt(jnp.mean(o * o, axis=-1, keepdims=True) + EPS)
    o = o.reshape(bsz, t_len, A_WIDTH) * lw['hgrn_onorm_g'].astype(jnp.float32)
    ya = (o * jax.nn.silu(g.astype(jnp.float32))).astype(p.dtype)
    yb = _chunk_token_mlp(u, v, lw['gmlp_ln_g'], lw['gmlp_ln_b'], lw['gmlp_w_s'], lw['gmlp_b_s'])
    yc = _conv_module(ca, cg, lw['conv_dw_w'], lw['conv_dw_b'], lw['conv_ln_g'], lw['conv_ln_b'],
                      lw['conv_pw_w'], lw['conv_pw_b'], n_seg)
    y = jnp.concatenate([ya, yb, yc], axis=-1) @ lw['w_out']
    return y, s_f, s_b


def _swiglu(h, w13, w2):
    a, b = jnp.split(h @ w13, 2, axis=-1)
    return (jax.nn.silu(a) * b) @ w2


def _layer(x, ctx, c, c_ctx, lw, lb, rows, last):
    mod = jax.nn.silu(c) @ lw['ada_w'] + lw['ada_b']
    mod_ctx = jax.nn.silu(c_ctx) @ lw['ada_w'] + lw['ada_b']
    sh1, sc1, g1, sh2, sc2, g2 = jnp.split(mod[:, None, :], 6, axis=-1)
    csh1, csc1, cg1, csh2, csc2, cg2 = jnp.split(mod_ctx, 6)
    zero = jnp.zeros((x.shape[0], A_HEADS, A_DK, A_DV), jnp.float32)
    pc = _modulate(_rms_norm(ctx, lw['norm1_g']), csh1, csc1) @ lw['w_in']
    if last:
        qc, zfc, zbc, vic = jnp.split(pc[..., :4 * A_WIDTH], 4, axis=-1)
        _, s_f, s_b = _bidir_hgrn2(qc, zfc, zbc, vic, lb, zero, zero)
    else:
        yc, s_f, s_b = _token_mixer(pc, lw, lb, zero, zero, 1)
        ctx = ctx + cg1 * yc
        hc = _modulate(_rms_norm(ctx, lw['norm2_g']), csh2, csc2)
        ctx = ctx + cg2 * _swiglu(hc, lw['ffn_w13'], lw['ffn_w2'])
    px = _modulate(_rms_norm(x, lw['norm1_g']), sh1, sc1) @ lw['w_in']
    y, _, _ = _token_mixer(px, lw, lb, s_f, s_b, rows)
    x = x + g1 * y
    hx = _modulate(_rms_norm(x, lw['norm2_g']), sh2, sc2)
    x = x + g2 * _swiglu(hx, lw['ffn_w13'], lw['ffn_w2'])
    return x, ctx


def setup_inputs(seed: int = 0) -> dict:
    key = jax.random.key(seed)
    ks = jax.random.split(key, 25)
    f32 = jnp.float32
    L, D = DEPTH, D_MODEL

    def nrm(k, shape, scale):
        return jax.random.normal(k, shape, f32) * scale

    def gain(k, shape):
        return 1.0 + 0.01 * jax.random.normal(k, shape, f32)

    return {
        'x': nrm(ks[0], (BATCH, SEQ, D), 1.0),
        'c': nrm(ks[1], (BATCH, D), 1.0),
        'ctx': nrm(ks[2], (BATCH, CTX_LEN, D), 1.0),
        'c_ctx': nrm(ks[3], (D,), 1.0),
        'ada_w': nrm(ks[4], (L, D, 6 * D), 0.5 * D ** -0.5),
        'ada_b': nrm(ks[5], (L, 6 * D), 0.01),
        'norm1_g': gain(ks[6], (L, D)),
        'w_in': nrm(ks[7], (L, D, IN_COLS), D ** -0.5),
        'hgrn_lb_logits': nrm(ks[8], (L, 2, A_WIDTH), 1.0),
        'hgrn_onorm_g': gain(ks[9], (L, A_WIDTH)),
        'gmlp_ln_g': gain(ks[10], (L, B_WIDTH)),
        'gmlp_ln_b': nrm(ks[11], (L, B_WIDTH), 0.01),
        'gmlp_w_s': nrm(ks[12], (L, B_GROUPS, B_CHUNK, B_CHUNK), B_CHUNK ** -0.5),
        'gmlp_b_s': gain(ks[13], (L, B_GROUPS, B_CHUNK)),
        'conv_dw_w': nrm(ks[14], (L, C_KERNEL, C_WIDTH), C_KERNEL ** -0.5),
        'conv_dw_b': nrm(ks[15], (L, C_WIDTH), 0.01),
        'conv_ln_g': gain(ks[16], (L, C_WIDTH)),
        'conv_ln_b': nrm(ks[17], (L, C_WIDTH), 0.01),
        'conv_pw_w': nrm(ks[18], (L, C_WIDTH, C_WIDTH), C_WIDTH ** -0.5),
        'conv_pw_b': nrm(ks[19], (L, C_WIDTH), 0.01),
        'w_out': nrm(ks[20], (L, MIX_WIDTH, D), MIX_WIDTH ** -0.5),
        'norm2_g': gain(ks[21], (L, D)),
        'ffn_w13': nrm(ks[22], (L, D, 2 * D_FF), D ** -0.5),
        'ffn_w2': nrm(ks[23], (L, D_FF, D), D_FF ** -0.5),
        'final_norm_g': gain(ks[24], (D,)),
    }


def reference(x, c, ctx, c_ctx, ada_w, ada_b, norm1_g, w_in, hgrn_lb_logits, hgrn_onorm_g,
              gmlp_ln_g, gmlp_ln_b, gmlp_w_s, gmlp_b_s, conv_dw_w, conv_dw_b, conv_ln_g,
              conv_ln_b, conv_pw_w, conv_pw_b, w_out, norm2_g, ffn_w13, ffn_w2, final_norm_g):
    rows = x.shape[1] // GRID_W
    lb_p = jax.nn.softmax(hgrn_lb_logits.astype(jnp.float32), axis=0)
    lbs = jnp.cumsum(lb_p, axis=0) - lb_p[0:1]
    for l in range(DEPTH):
        lw = {
            'ada_w': ada_w[l], 'ada_b': ada_b[l], 'norm1_g': norm1_g[l], 'w_in': w_in[l],
            'hgrn_onorm_g': hgrn_onorm_g[l], 'gmlp_ln_g': gmlp_ln_g[l], 'gmlp_ln_b': gmlp_ln_b[l],
            'gmlp_w_s': gmlp_w_s[l], 'gmlp_b_s': gmlp_b_s[l], 'conv_dw_w': conv_dw_w[l],
            'conv_dw_b': conv_dw_b[l], 'conv_ln_g': conv_ln_g[l], 'conv_ln_b': conv_ln_b[l],
            'conv_pw_w': conv_pw_w[l], 'conv_pw_b': conv_pw_b[l], 'w_out': w_out[l],
            'norm2_g': norm2_g[l], 'ffn_w13': ffn_w13[l], 'ffn_w2': ffn_w2[l],
        }
        x, ctx = _layer(x, ctx, c, c_ctx, lw, lbs[l], rows, l == DEPTH - 1)
    return _rms_norm(x, final_norm_g)
```

```python
import functools

import jax
import jax.numpy as jnp
from jax import lax
from jax.experimental import pallas as pl
from jax.experimental.pallas import tpu as pltpu

F32 = jnp.float32
BF16 = jnp.bfloat16

D_MODEL = 1024
GRID_W = 64
A_HEADS = 4
A_DK = 128
A_WIDTH = A_HEADS * A_DK
A_CHUNK = 64
A_SUB = 16
B_GROUPS = 4
B_WIDTH = 256
B_CHUNK = 128
C_GROUPS = 4
C_WIDTH = 256
C_KERNEL = 31
C_PAD = 16
PA_COLS = 4 * A_WIDTH
PM_COLS = A_WIDTH + 2 * B_WIDTH + 2 * C_WIDTH
EPS = 1e-6
EXP_CLAMP = 60.0
MOD_ROWS = 8

V7X_VMEM_LIMIT = 56 * 1024 * 1024

_NT = (((1,), (1,)), ((), ()))
_TN = (((0,), (0,)), ((), ()))


def _cparams(sem):
    return pltpu.CompilerParams(dimension_semantics=sem, vmem_limit_bytes=V7X_VMEM_LIMIT)


def _full(shape):
    n = len(shape)
    return pl.BlockSpec(shape, lambda *_: (0,) * n)


def _mod_kernel(cs_ref, w_ref, b_ref, o_ref):
    s = cs_ref[...]
    s = s * jax.nn.sigmoid(s)
    o_ref[0] = jnp.dot(s, w_ref[0], preferred_element_type=F32,
                       precision=lax.Precision.HIGHEST) + b_ref[0]


def _modulation(cs, ada_w, ada_b):
    depth, d, n = ada_w.shape
    tn = 1024
    return pl.pallas_call(
        _mod_kernel,
        grid=(depth, n // tn),
        in_specs=[
            pl.BlockSpec((MOD_ROWS, d), lambda l, j: (0, 0)),
            pl.BlockSpec((1, d, tn), lambda l, j: (l, 0, j)),
            pl.BlockSpec((1, 1, tn), lambda l, j: (l, 0, j)),
        ],
        out_specs=pl.BlockSpec((1, MOD_ROWS, tn), lambda l, j: (l, 0, j)),
        out_shape=jax.ShapeDtypeStruct((depth, MOD_ROWS, n), F32),
        compiler_params=_cparams(("arbitrary", "arbitrary")),
        name="modulation",
    )(cs, ada_w, ada_b.reshape(depth, 1, n))


def _norm_modulate(x, g, shift, scale):
    ms = jnp.mean(x * x, axis=-1, keepdims=True)
    y = x * lax.rsqrt(ms + EPS) * g
    return y * (1.0 + scale) + shift


def _in_proj_kernel(x_ref, mod_ref, g_ref, w_ref, *out_refs):
    h = _norm_modulate(x_ref[0], g_ref[...], mod_ref[0, 0:1, :], mod_ref[0, 1:2, :]).astype(BF16)
    out_refs[0][0] = jnp.dot(h, w_ref[:, :PA_COLS], preferred_element_type=F32)
    if len(out_refs) > 1:
        out_refs[1][0] = jnp.dot(h, w_ref[:, PA_COLS:], preferred_element_type=F32)


def _in_proj(x, mod, norm_g, w_in, tile, with_mixers):
    bsz, t_len, d = x.shape
    cols = w_in.shape[1]
    out_shape = [jax.ShapeDtypeStruct((bsz, t_len, PA_COLS), F32)]
    out_specs = [pl.BlockSpec((1, tile, PA_COLS), lambda b, i: (b, i, 0))]
    if with_mixers:
        out_shape.append(jax.ShapeDtypeStruct((bsz, t_len, PM_COLS), F32))
        out_specs.append(pl.BlockSpec((1, tile, PM_COLS), lambda b, i: (b, i, 0)))
    return pl.pallas_call(
        _in_proj_kernel,
        grid=(bsz, t_len // tile),
        in_specs=[
            pl.BlockSpec((1, tile, d), lambda b, i: (b, i, 0)),
            pl.BlockSpec((1, 6, d), lambda b, i: (b, 0, 0)),
            _full((1, d)),
            _full((d, cols)),
        ],
        out_specs=out_specs,
        out_shape=out_shape,
        compiler_params=_cparams(("arbitrary", "arbitrary")),
        name="in_proj",
    )(x, mod, norm_g.reshape(1, d), w_in)


def _lower_bounds(logits, layer):
    depth = logits.shape[0]
    rows = [logits[j] for j in range(depth)]
    m = rows[0]
    for r in rows[1:]:
        m = jnp.maximum(m, r)
    es = [jnp.exp(r - m) for r in rows]
    tot = es[0]
    for e in es[1:]:
        tot = tot + e
    lb = jnp.zeros_like(m)
    for j in range(1, layer + 1):
        lb = lb + es[j] / tot
    return lb


def _hgrn_direction(q, z, v, lb, st_ref, d, tri, mask, rev):
    oml = 1.0 - lb
    sg = jax.nn.sigmoid(z)
    lf = jnp.log(lb + oml * sg)
    kk = oml * (1.0 - sg)
    b = jnp.dot(tri, lf, preferred_element_type=F32, precision=lax.Precision.HIGHEST)
    blast = b[0:1] if rev else b[A_CHUNK - 1:A_CHUNK]
    qe = (q * jnp.exp(b)).astype(BF16)
    kd = (kk * jnp.exp(blast - b)).astype(BF16)
    dec = jnp.exp(blast)
    vb = v.astype(BF16)
    qts, khs = [], []
    for i in range(A_CHUNK // A_SUB):
        lo = A_SUB * i
        if rev:
            r = b[lo + A_SUB:lo + A_SUB + 1] if lo + A_SUB < A_CHUNK else None
        else:
            r = b[lo - 1:lo] if i > 0 else None
        bi = b[lo:lo + A_SUB]
        qts.append((q[lo:lo + A_SUB] * jnp.exp(bi if r is None else bi - r)).astype(BF16))
        e = -b if r is None else r - b
        khs.append((kk * jnp.exp(jnp.minimum(e, EXP_CLAMP))).astype(BF16))
    outs = []
    for h in range(A_HEADS):
        c = slice(h * A_DK, (h + 1) * A_DK)
        st = st_ref[0, d, h]
        inter = lax.dot_general(qe[:, c], st.astype(BF16), _NT, preferred_element_type=F32)
        sc = jnp.concatenate(
            [lax.dot_general(qt[:, c], kh[:, c], _NT, preferred_element_type=F32)
             for qt, kh in zip(qts, khs)], axis=0)
        sc = jnp.where(mask, sc, 0.0).astype(BF16)
        outs.append(inter + jnp.dot(sc, vb[:, c], preferred_element_type=F32))
        upd = lax.dot_general(vb[:, c], kd[:, c], _TN, preferred_element_type=F32)
        st_ref[0, d, h] = st * dec[:, c] + upd
    return jnp.concatenate(outs, axis=1)


def _hgrn_kernel(qf_ref, zf_ref, vf_ref, qb_ref, zb_ref, vb_ref, lbl_ref, s0_ref,
                 of_ref, ob_ref, st_ref, *, layer, n_chunks):
    @pl.when(pl.program_id(1) == 0)
    def _():
        st_ref[...] = s0_ref[...]

    lb = _lower_bounds(lbl_ref[...], layer)
    row = lax.broadcasted_iota(jnp.int32, (A_CHUNK, A_CHUNK), 0)
    col = lax.broadcasted_iota(jnp.int32, (A_CHUNK, A_CHUNK), 1)
    m_f = col <= row
    m_b = col >= row
    tri_f = m_f.astype(F32)
    tri_b = m_b.astype(F32)

    def body(c, carry):
        rf = pl.multiple_of(c * A_CHUNK, A_CHUNK)
        rb = pl.multiple_of((n_chunks - 1 - c) * A_CHUNK, A_CHUNK)
        sl_f = pl.ds(rf, A_CHUNK)
        sl_b = pl.ds(rb, A_CHUNK)
        of_ref[0, sl_f, :] = _hgrn_direction(qf_ref[0, sl_f, :], zf_ref[0, sl_f, :], vf_ref[0, sl_f, :],
                                             lb[0:1], st_ref, 0, tri_f, m_f, False)
        ob_ref[0, sl_b, :] = _hgrn_direction(qb_ref[0, sl_b, :], zb_ref[0, sl_b, :], vb_ref[0, sl_b, :],
                                             lb[1:2], st_ref, 1, tri_b, m_b, True)
        return carry

    lax.fori_loop(0, n_chunks, body, 0)


def _hgrn(pa, lb_logits, s0, layer, tile):
    bsz, t_len, _ = pa.shape
    nb = t_len // tile
    depth = lb_logits.shape[0]

    def col_spec(colblk, rev):
        if rev:
            return pl.BlockSpec((1, tile, A_WIDTH), lambda b, i: (b, nb - 1 - i, colblk))
        return pl.BlockSpec((1, tile, A_WIDTH), lambda b, i: (b, i, colblk))

    st_spec = pl.BlockSpec((1, 2, A_HEADS, A_DK, A_DK), lambda b, i: (b, 0, 0, 0, 0))
    o_sds = jax.ShapeDtypeStruct((bsz, t_len, A_WIDTH), F32)
    return pl.pallas_call(
        functools.partial(_hgrn_kernel, layer=layer, n_chunks=tile // A_CHUNK),
        grid=(bsz, nb),
        in_specs=[col_spec(0, False), col_spec(1, False), col_spec(3, False),
                  col_spec(0, True), col_spec(2, True), col_spec(3, True),
                  _full((depth, 2, A_WIDTH)), st_spec],
        out_specs=[col_spec(0, False), col_spec(0, True), st_spec],
        out_shape=[o_sds, o_sds, jax.ShapeDtypeStruct(s0.shape, F32)],
        compiler_params=_cparams(("arbitrary", "arbitrary")),
        name="hgrn",
    )(pa, pa, pa, pa, pa, pa, lb_logits, s0)


def _group_mean(x, gavg):
    hi = x.astype(BF16)
    lo = (x - hi.astype(F32)).astype(BF16)
    return (jnp.dot(hi, gavg, preferred_element_type=F32)
            + jnp.dot(lo, gavg, preferred_element_type=F32))


def _group_layer_norm(x, gavg, g, b):
    mu = _group_mean(x, gavg)
    xc = x - mu
    var = _group_mean(xc * xc, gavg)
    return xc * lax.rsqrt(var + EPS) * g + b


def _mixer_kernel(x_ref, of_ref, ob_ref, pm_ref, mod_ref, og_ref, gavg_ref,
                  bg_ref, bb_ref, ws_ref, bs_ref,
                  dww_ref, dwb_ref, cg_ref, cb_ref, pww_ref, pwb_ref, wo_ref,
                  o_ref, zpad_ref, *, seg):
    tile = x_ref.shape[1]
    gavg = gavg_ref[...]
    o = of_ref[0] + ob_ref[0]
    parts = []
    for h in range(A_HEADS):
        oh = o[:, h * A_DK:(h + 1) * A_DK]
        parts.append(oh * lax.rsqrt(jnp.mean(oh * oh, axis=-1, keepdims=True) + EPS))
    gate = pm_ref[0, :, 0:A_WIDTH]
    ya = jnp.concatenate(parts, axis=1) * og_ref[...] * (gate * jax.nn.sigmoid(gate))
    u = jax.nn.gelu(pm_ref[0, :, A_WIDTH:A_WIDTH + B_WIDTH])
    vn = _group_layer_norm(jax.nn.gelu(pm_ref[0, :, A_WIDTH + B_WIDTH:A_WIDTH + 2 * B_WIDTH]),
                           gavg, bg_ref[...], bb_ref[...]).astype(BF16)
    lane_group = lax.broadcasted_iota(jnp.int32, (B_CHUNK, B_WIDTH), 1) // (B_WIDTH // B_GROUPS)
    mixed = []
    for n in range(tile // B_CHUNK):
        r = jnp.dot(ws_ref[...], vn[n * B_CHUNK:(n + 1) * B_CHUNK], preferred_element_type=F32)
        m = bs_ref[...]
        for g in range(B_GROUPS):
            m = m + jnp.where(lane_group == g, r[g * B_CHUNK:(g + 1) * B_CHUNK], 0.0)
        mixed.append(m)
    yb = u * jnp.concatenate(mixed, axis=0)
    c0 = A_WIDTH + 2 * B_WIDTH
    ca = pm_ref[0, :, c0:c0 + C_WIDTH]
    cgate = pm_ref[0, :, c0 + C_WIDTH:c0 + 2 * C_WIDTH]
    z = ca * jax.nn.sigmoid(cgate)
    nseg = tile // seg
    pad_zeros = jnp.zeros((C_PAD, C_WIDTH), F32)
    convs = []
    for s in range(nseg):
        zpad_ref[s, 0:C_PAD, :] = pad_zeros
        zpad_ref[s, C_PAD + seg:2 * C_PAD + seg, :] = pad_zeros
        zpad_ref[s, C_PAD:C_PAD + seg, :] = z[s * seg:(s + 1) * seg]
    for s in range(nseg):
        acc = jnp.zeros((seg, C_WIDTH), F32)
        for j in range(C_KERNEL):
            start = C_PAD - C_KERNEL // 2 + j
            acc = acc + dww_ref[j:j + 1, :] * zpad_ref[s, start:start + seg, :]
        convs.append(acc)
    zc = jnp.concatenate(convs, axis=0) + dwb_ref[...]
    zc = _group_layer_norm(zc, gavg, cg_ref[...], cb_ref[...])
    zc = (zc * jax.nn.sigmoid(zc)).astype(BF16)
    yc = jnp.dot(zc, pww_ref[...], preferred_element_type=F32) + pwb_ref[...]
    y = (jnp.dot(ya.astype(BF16), wo_ref[0:A_WIDTH, :], preferred_element_type=F32)
         + jnp.dot(yb.astype(BF16), wo_ref[A_WIDTH:A_WIDTH + B_WIDTH, :], preferred_element_type=F32)
         + jnp.dot(yc.astype(BF16), wo_ref[A_WIDTH + B_WIDTH:, :], preferred_element_type=F32))
    o_ref[0] = x_ref[0] + mod_ref[0, 2:3, :] * y


def _mixer(x, o_f, o_b, pm, mod, lw, tile, seg):
    bsz, t_len, d = x.shape
    nseg = tile // seg

    def tok(cols):
        return pl.BlockSpec((1, tile, cols), lambda b, i: (b, i, 0))

    params = [lw['onorm_g'], lw['gavg'], lw['gmlp_ln_g'], lw['gmlp_ln_b'], lw['ws_cat'], lw['bs_tile'],
              lw['dw_w'], lw['dw_b'], lw['conv_ln_g'], lw['conv_ln_b'], lw['pw_w'], lw['pw_b'], lw['w_out']]
    return pl.pallas_call(
        functools.partial(_mixer_kernel, seg=seg),
        grid=(bsz, t_len // tile),
        in_specs=[tok(d), tok(A_WIDTH), tok(A_WIDTH), tok(PM_COLS),
                  pl.BlockSpec((1, 6, d), lambda b, i: (b, 0, 0))] + [_full(p.shape) for p in params],
        out_specs=tok(d),
        out_shape=jax.ShapeDtypeStruct(x.shape, F32),
        scratch_shapes=[pltpu.VMEM((nseg, seg + 2 * C_PAD, C_WIDTH), F32)],
        compiler_params=_cparams(("arbitrary", "arbitrary")),
        name="mixer",
    )(x, o_f, o_b, pm, mod, *params)


def _ffn_kernel(x_ref, mod_ref, g_ref, w13_ref, w2_ref, fg_ref, o_ref, *, d_ff, chunk, final_norm):
    x = x_ref[0]
    h = _norm_modulate(x, g_ref[...], mod_ref[0, 3:4, :], mod_ref[0, 4:5, :]).astype(BF16)
    acc = jnp.zeros(x.shape, F32)
    for c in range(d_ff // chunk):
        a = jnp.dot(h, w13_ref[:, c * chunk:(c + 1) * chunk], preferred_element_type=F32)
        b = jnp.dot(h, w13_ref[:, d_ff + c * chunk:d_ff + (c + 1) * chunk], preferred_element_type=F32)
        act = (a * jax.nn.sigmoid(a) * b).astype(BF16)
        acc = acc + jnp.dot(act, w2_ref[c * chunk:(c + 1) * chunk, :], preferred_element_type=F32)
    y = x + mod_ref[0, 5:6, :] * acc
    if final_norm:
        y = y * lax.rsqrt(jnp.mean(y * y, axis=-1, keepdims=True) + EPS) * fg_ref[...]
    o_ref[0] = y


def _ffn(x, mod, norm_g, w13, w2, final_g, tile, final_norm):
    bsz, t_len, d = x.shape
    d_ff = w2.shape[0]
    return pl.pallas_call(
        functools.partial(_ffn_kernel, d_ff=d_ff, chunk=256, final_norm=final_norm),
        grid=(bsz, t_len // tile),
        in_specs=[
            pl.BlockSpec((1, tile, d), lambda b, i: (b, i, 0)),
            pl.BlockSpec((1, 6, d), lambda b, i: (b, 0, 0)),
            _full((1, d)), _full(w13.shape), _full(w2.shape), _full((1, d)),
        ],
        out_specs=pl.BlockSpec((1, tile, d), lambda b, i: (b, i, 0)),
        out_shape=jax.ShapeDtypeStruct(x.shape, F32),
        compiler_params=_cparams(("arbitrary", "arbitrary")),
        name="ffn",
    )(x, mod, norm_g.reshape(1, d), w13, w2, final_g.reshape(1, d))


def _layer_params(l, gavg, norm1_g, w_in, hgrn_onorm_g, gmlp_ln_g, gmlp_ln_b, gmlp_w_s, gmlp_b_s,
                  conv_dw_w, conv_dw_b, conv_ln_g, conv_ln_b, conv_pw_w, conv_pw_b, w_out,
                  norm2_g, ffn_w13, ffn_w2):
    row = lambda a: a.reshape(1, -1).astype(F32)
    return {
        'norm1_g': norm1_g[l], 'w_in': w_in[l].astype(BF16),
        'onorm_g': row(hgrn_onorm_g[l]), 'gavg': gavg,
        'gmlp_ln_g': row(gmlp_ln_g[l]), 'gmlp_ln_b': row(gmlp_ln_b[l]),
        'ws_cat': gmlp_w_s[l].reshape(B_GROUPS * B_CHUNK, B_CHUNK).astype(BF16),
        'bs_tile': jnp.repeat(gmlp_b_s[l].T, B_WIDTH // B_GROUPS, axis=1).astype(F32),
        'dw_w': conv_dw_w[l].astype(F32), 'dw_b': row(conv_dw_b[l]),
        'conv_ln_g': row(conv_ln_g[l]), 'conv_ln_b': row(conv_ln_b[l]),
        'pw_w': conv_pw_w[l].astype(BF16), 'pw_b': row(conv_pw_b[l]),
        'w_out': w_out[l].astype(BF16),
        'norm2_g': norm2_g[l], 'ffn_w13': ffn_w13[l].astype(BF16), 'ffn_w2': ffn_w2[l].astype(BF16),
    }


def kernel(x, c, ctx, c_ctx, ada_w, ada_b, norm1_g, w_in, hgrn_lb_logits, hgrn_onorm_g, gmlp_ln_g, gmlp_ln_b, gmlp_w_s, gmlp_b_s, conv_dw_w, conv_dw_b, conv_ln_g, conv_ln_b, conv_pw_w, conv_pw_b, w_out, norm2_g, ffn_w13, ffn_w2, final_norm_g):
    bsz, t_len, d = x.shape
    ctx_len = ctx.shape[1]
    depth = ada_w.shape[0]
    lb_logits = hgrn_lb_logits.astype(F32)

    cs = jnp.concatenate([c, c_ctx[None, :], jnp.zeros((MOD_ROWS - bsz - 1, d), F32)], axis=0)
    mod_all = _modulation(cs, ada_w, ada_b)
    grp = jnp.arange(B_WIDTH) // (B_WIDTH // B_GROUPS)
    gavg = ((grp[:, None] == grp[None, :]).astype(F32) / (B_WIDTH // B_GROUPS)).astype(BF16)
    zero_state = jnp.zeros((bsz, 2, A_HEADS, A_DK, A_DK), F32)

    for l in range(depth):
        last = l == depth - 1
        lw = _layer_params(l, gavg, norm1_g, w_in, hgrn_onorm_g, gmlp_ln_g, gmlp_ln_b, gmlp_w_s,
                           gmlp_b_s, conv_dw_w, conv_dw_b, conv_ln_g, conv_ln_b, conv_pw_w,
                           conv_pw_b, w_out, norm2_g, ffn_w13, ffn_w2)
        mod = mod_all[l, :bsz].reshape(bsz, 6, d)
        mod_ctx = jnp.broadcast_to(mod_all[l, bsz].reshape(1, 6, d), (bsz, 6, d))
        if last:
            (pa_c,) = _in_proj(ctx, mod_ctx, lw['norm1_g'], lw['w_in'][:, :PA_COLS], ctx_len, False)
            _, _, state = _hgrn(pa_c, lb_logits, zero_state, l, ctx_len)
        else:
            pa_c, pm_c = _in_proj(ctx, mod_ctx, lw['norm1_g'], lw['w_in'], ctx_len, True)
            of_c, ob_c, state = _hgrn(pa_c, lb_logits, zero_state, l, ctx_len)
            ctx = _mixer(ctx, of_c, ob_c, pm_c, mod_ctx, lw, ctx_len, ctx_len)
            ctx = _ffn(ctx, mod_ctx, lw['norm2_g'], lw['ffn_w13'], lw['ffn_w2'], final_norm_g,
                       ctx_len, False)
        pa, pm = _in_proj(x, mod, lw['norm1_g'], lw['w_in'], 512, True)
        o_f, o_b, _ = _hgrn(pa, lb_logits, state, l, 512)
        x = _mixer(x, o_f, o_b, pm, mod, lw, 256, t_len // (t_len // GRID_W))
        x = _ffn(x, mod, lw['norm2_g'], lw['ffn_w13'], lw['ffn_w2'], final_norm_g, 512, last)
    return x
```

```python
import functools

import jax
import jax.numpy as jnp
from jax import lax
from jax.experimental import pallas as pl
from jax.experimental.pallas import tpu as pltpu

F32 = jnp.float32
BF16 = jnp.bfloat16

D_MODEL = 1024
GRID_W = 64
A_HEADS = 4
A_DK = 128
A_WIDTH = A_HEADS * A_DK
A_CHUNK = 64
A_SUB = 32
B_GROUPS = 4
B_WIDTH = 256
B_CHUNK = 128
C_GROUPS = 4
C_WIDTH = 256
C_KERNEL = 31
C_PAD = 16
PA_COLS = 4 * A_WIDTH
PM_COLS = A_WIDTH + 2 * B_WIDTH + 2 * C_WIDTH
EPS = 1e-6
EXP2_CLAMP = 80.0
LOG2E = 1.4426950408889634
MOD_ROWS = 8

V7X_VMEM_LIMIT = 56 * 1024 * 1024

_NT = (((1,), (1,)), ((), ()))
_TN = (((0,), (0,)), ((), ()))


def _cparams(sem):
    return pltpu.CompilerParams(dimension_semantics=sem, vmem_limit_bytes=V7X_VMEM_LIMIT)


def _full(shape):
    n = len(shape)
    return pl.BlockSpec(shape, lambda *_: (0,) * n)


def _mod_kernel(cs_ref, w_ref, b_ref, o_ref):
    s = cs_ref[...]
    s = s * jax.nn.sigmoid(s)
    o_ref[0] = jnp.dot(s, w_ref[0], preferred_element_type=F32,
                       precision=lax.Precision.HIGHEST) + b_ref[0]


def _modulation(cs, ada_w, ada_b):
    depth, d, n = ada_w.shape
    tn = 1024
    return pl.pallas_call(
        _mod_kernel,
        grid=(depth, n // tn),
        in_specs=[
            pl.BlockSpec((MOD_ROWS, d), lambda l, j: (0, 0)),
            pl.BlockSpec((1, d, tn), lambda l, j: (l, 0, j)),
            pl.BlockSpec((1, 1, tn), lambda l, j: (l, 0, j)),
        ],
        out_specs=pl.BlockSpec((1, MOD_ROWS, tn), lambda l, j: (l, 0, j)),
        out_shape=jax.ShapeDtypeStruct((depth, MOD_ROWS, n), F32),
        compiler_params=_cparams(("arbitrary", "arbitrary")),
        name="modulation",
    )(cs, ada_w, ada_b.reshape(depth, 1, n))


def _norm_modulate(x, g, shift, scale):
    ms = jnp.mean(x * x, axis=-1, keepdims=True)
    y = x * lax.rsqrt(ms + EPS) * g
    return y * (1.0 + scale) + shift


def _in_proj_kernel(x_ref, mod_ref, g_ref, w_ref, *out_refs):
    h = _norm_modulate(x_ref[0], g_ref[...], mod_ref[0, 0:1, :], mod_ref[0, 1:2, :]).astype(BF16)
    out_refs[0][0] = jnp.dot(h, w_ref[:, :PA_COLS], preferred_element_type=F32)
    if len(out_refs) > 1:
        out_refs[1][0] = jnp.dot(h, w_ref[:, PA_COLS:], preferred_element_type=F32)


def _in_proj(x, mod, norm_g, w_in, tile, with_mixers):
    bsz, t_len, d = x.shape
    cols = w_in.shape[1]
    out_shape = [jax.ShapeDtypeStruct((bsz, t_len, PA_COLS), F32)]
    out_specs = [pl.BlockSpec((1, tile, PA_COLS), lambda b, i: (b, i, 0))]
    if with_mixers:
        out_shape.append(jax.ShapeDtypeStruct((bsz, t_len, PM_COLS), F32))
        out_specs.append(pl.BlockSpec((1, tile, PM_COLS), lambda b, i: (b, i, 0)))
    return pl.pallas_call(
        _in_proj_kernel,
        grid=(bsz, t_len // tile),
        in_specs=[
            pl.BlockSpec((1, tile, d), lambda b, i: (b, i, 0)),
            pl.BlockSpec((1, 6, d), lambda b, i: (b, 0, 0)),
            _full((1, d)),
            _full((d, cols)),
        ],
        out_specs=out_specs,
        out_shape=out_shape,
        compiler_params=_cparams(("arbitrary", "arbitrary")),
        name="in_proj",
    )(x, mod, norm_g.reshape(1, d), w_in)


def _lower_bounds(logits, layer):
    depth = logits.shape[0]
    rows = [logits[j] for j in range(depth)]
    m = rows[0]
    for r in rows[1:]:
        m = jnp.maximum(m, r)
    es = [jnp.exp(r - m) for r in rows]
    tot = es[0]
    for e in es[1:]:
        tot = tot + e
    lb = jnp.zeros_like(m)
    for j in range(1, layer + 1):
        lb = lb + es[j] / tot
    return lb


def _chunk_rows(c, lo=0, n=A_CHUNK):
    return slice(c * A_CHUNK + lo, c * A_CHUNK + lo + n)


def _hgrn_factors(q, z, v, lb, rev, n_chunks):
    width = q.shape[1]
    sub = A_SUB
    row = lax.broadcasted_iota(jnp.int32, (A_CHUNK, A_CHUNK), 0)
    col = lax.broadcasted_iota(jnp.int32, (A_CHUNK, A_CHUNK), 1)
    tri = ((col >= row) if rev else (col <= row)).astype(F32).astype(BF16)
    oml = 1.0 - lb
    sg = jax.nn.sigmoid(z)
    lf = jnp.log(lb + oml * sg)
    kk = oml * (1.0 - sg)
    hi = lf.astype(BF16)
    lo = (lf - hi.astype(F32)).astype(BF16)
    bs = [(jnp.dot(tri, hi[_chunk_rows(c)], preferred_element_type=F32)
           + jnp.dot(tri, lo[_chunk_rows(c)], preferred_element_type=F32)) * LOG2E
          for c in range(n_chunks)]
    wide, narrow = (0, 1) if rev else (1, 0)
    mid_row = sub // 2 if rev else sub // 2 - 1
    last_row = 0 if rev else A_CHUNK - 1
    b_last, b_own, b_wide, b_narrow, decs = [], [], [], [], []
    for bc in bs:
        mids = [bc[i * sub + mid_row:i * sub + mid_row + 1] for i in range(2)]
        b_last.append(jnp.broadcast_to(bc[last_row:last_row + 1], (A_CHUNK, width)))
        b_own += [jnp.broadcast_to(m, (sub, width)) for m in mids]
        b_wide.append(jnp.broadcast_to(mids[wide], (A_CHUNK, width)))
        b_narrow.append(mids[narrow] - bc[narrow * sub:(narrow + 1) * sub])
        decs.append(jnp.exp2(bc[last_row:last_row + 1]))
    b = jnp.concatenate(bs, axis=0)
    kk_narrow = jnp.concatenate([kk[_chunk_rows(c, narrow * sub, sub)] for c in range(n_chunks)], axis=0)
    return {
        'qe': (q * jnp.exp2(b)).astype(BF16),
        'kd': (kk * jnp.exp2(jnp.concatenate(b_last, axis=0) - b)).astype(BF16),
        'qd': (q * jnp.exp2(jnp.minimum(b - jnp.concatenate(b_own, axis=0), EXP2_CLAMP))).astype(BF16),
        'k_wide': (kk * jnp.exp2(jnp.minimum(jnp.concatenate(b_wide, axis=0) - b, EXP2_CLAMP))).astype(BF16),
        'k_narrow': (kk_narrow * jnp.exp2(jnp.minimum(jnp.concatenate(b_narrow, axis=0), EXP2_CLAMP))).astype(BF16),
        'v': v.astype(BF16),
        'dec': decs,
    }


def _hgrn_scores(f, rev, n_chunks):
    sub = A_SUB
    wide, narrow = (0, 1) if rev else (1, 0)
    row_n = lax.broadcasted_iota(jnp.int32, (sub, sub), 0)
    col_n = lax.broadcasted_iota(jnp.int32, (sub, sub), 1)
    row_w = lax.broadcasted_iota(jnp.int32, (sub, A_CHUNK), 0)
    col_w = lax.broadcasted_iota(jnp.int32, (sub, A_CHUNK), 1)
    mask_n = (col_n >= row_n) if rev else (col_n <= row_n)
    mask_w = (col_w >= row_w) if rev else (col_w <= row_w + sub)
    out = []
    for c in range(n_chunks):
        per_head = []
        for h in range(A_HEADS):
            hc = slice(h * A_DK, (h + 1) * A_DK)
            s_n = lax.dot_general(f['qd'][_chunk_rows(c, narrow * sub, sub), hc],
                                  f['k_narrow'][c * sub:(c + 1) * sub, hc], _NT, preferred_element_type=F32)
            s_w = lax.dot_general(f['qd'][_chunk_rows(c, wide * sub, sub), hc],
                                  f['k_wide'][_chunk_rows(c), hc], _NT, preferred_element_type=F32)
            per_head.append((jnp.where(mask_n, s_n, 0.0).astype(BF16),
                             jnp.where(mask_w, s_w, 0.0).astype(BF16)))
        out.append(per_head)
    return out


def _hgrn_intra(f, scores, rev, n_chunks):
    sub = A_SUB
    wide, narrow = (0, 1) if rev else (1, 0)
    out = []
    for c in range(n_chunks):
        per_head = []
        for h in range(A_HEADS):
            hc = slice(h * A_DK, (h + 1) * A_DK)
            s_n, s_w = scores[c][h]
            o_n = jnp.dot(s_n, f['v'][_chunk_rows(c, narrow * sub, sub), hc], preferred_element_type=F32)
            o_w = jnp.dot(s_w, f['v'][_chunk_rows(c), hc], preferred_element_type=F32)
            intra = jnp.concatenate([o_w, o_n] if rev else [o_n, o_w], axis=0)
            upd = lax.dot_general(f['v'][_chunk_rows(c), hc], f['kd'][_chunk_rows(c), hc], _TN,
                                  preferred_element_type=F32)
            per_head.append((intra, upd))
        out.append(per_head)
    return out


def _hgrn_scan(f, intra, st_ref, d, o_ref, rev, n_chunks):
    states = [st_ref[0, d, h] for h in range(A_HEADS)]
    for c in (range(n_chunks - 1, -1, -1) if rev else range(n_chunks)):
        outs = []
        for h in range(A_HEADS):
            hc = slice(h * A_DK, (h + 1) * A_DK)
            inter = lax.dot_general(f['qe'][_chunk_rows(c), hc], states[h].astype(BF16), _NT,
                                    preferred_element_type=F32)
            outs.append(inter + intra[c][h][0])
            states[h] = states[h] * f['dec'][c][:, hc] + intra[c][h][1]
        o_ref[0, _chunk_rows(c), :] = jnp.concatenate(outs, axis=1)
    for h in range(A_HEADS):
        st_ref[0, d, h] = states[h]


def _hgrn_kernel(qf_ref, zf_ref, vf_ref, qb_ref, zb_ref, vb_ref, lbl_ref, s0_ref,
                 of_ref, ob_ref, st_ref, *, layer, n_chunks):
    @pl.when(pl.program_id(1) == 0)
    def _():
        st_ref[...] = s0_ref[...]

    lb = _lower_bounds(lbl_ref[...], layer)
    dirs = ((qf_ref, zf_ref, vf_ref, of_ref, False), (qb_ref, zb_ref, vb_ref, ob_ref, True))
    facs = [_hgrn_factors(q[0], z[0], v[0], lb[d:d + 1], rev, n_chunks)
            for d, (q, z, v, _, rev) in enumerate(dirs)]
    scores = [_hgrn_scores(facs[d], dirs[d][4], n_chunks) for d in range(2)]
    intra = [_hgrn_intra(facs[d], scores[d], dirs[d][4], n_chunks) for d in range(2)]
    for d in range(2):
        _hgrn_scan(facs[d], intra[d], st_ref, d, dirs[d][3], dirs[d][4], n_chunks)


def _hgrn(pa, lb_logits, s0, layer, tile):
    bsz, t_len, _ = pa.shape
    nb = t_len // tile
    depth = lb_logits.shape[0]

    def col_spec(colblk, rev):
        if rev:
            return pl.BlockSpec((1, tile, A_WIDTH), lambda b, i: (b, nb - 1 - i, colblk))
        return pl.BlockSpec((1, tile, A_WIDTH), lambda b, i: (b, i, colblk))

    st_spec = pl.BlockSpec((1, 2, A_HEADS, A_DK, A_DK), lambda b, i: (b, 0, 0, 0, 0))
    o_sds = jax.ShapeDtypeStruct((bsz, t_len, A_WIDTH), F32)
    return pl.pallas_call(
        functools.partial(_hgrn_kernel, layer=layer, n_chunks=tile // A_CHUNK),
        grid=(bsz, nb),
        in_specs=[col_spec(0, False), col_spec(1, False), col_spec(3, False),
                  col_spec(0, True), col_spec(2, True), col_spec(3, True),
                  _full((depth, 2, A_WIDTH)), st_spec],
        out_specs=[col_spec(0, False), col_spec(0, True), st_spec],
        out_shape=[o_sds, o_sds, jax.ShapeDtypeStruct(s0.shape, F32)],
        compiler_params=_cparams(("arbitrary", "arbitrary")),
        name="hgrn",
    )(pa, pa, pa, pa, pa, pa, lb_logits, s0)


def _group_mean(x, gavg):
    hi = x.astype(BF16)
    lo = (x - hi.astype(F32)).astype(BF16)
    return (jnp.dot(hi, gavg, preferred_element_type=F32)
            + jnp.dot(lo, gavg, preferred_element_type=F32))


def _group_layer_norm(x, gavg, g, b):
    mu = _group_mean(x, gavg)
    xc = x - mu
    var = _group_mean(xc * xc, gavg)
    return xc * lax.rsqrt(var + EPS) * g + b


def _mixer_kernel(x_ref, of_ref, ob_ref, pm_ref, mod_ref, og_ref, gavg_ref,
                  bg_ref, bb_ref, ws_ref, bs_ref,
                  dww_ref, dwb_ref, cg_ref, cb_ref, pww_ref, pwb_ref, wo_ref,
                  o_ref, zpad_ref, *, seg):
    tile = x_ref.shape[1]
    gavg = gavg_ref[...]
    o = of_ref[0] + ob_ref[0]
    parts = []
    for h in range(A_HEADS):
        oh = o[:, h * A_DK:(h + 1) * A_DK]
        parts.append(oh * lax.rsqrt(jnp.mean(oh * oh, axis=-1, keepdims=True) + EPS))
    gate = pm_ref[0, :, 0:A_WIDTH]
    ya = jnp.concatenate(parts, axis=1) * og_ref[...] * (gate * jax.nn.sigmoid(gate))
    u = jax.nn.gelu(pm_ref[0, :, A_WIDTH:A_WIDTH + B_WIDTH])
    vn = _group_layer_norm(jax.nn.gelu(pm_ref[0, :, A_WIDTH + B_WIDTH:A_WIDTH + 2 * B_WIDTH]),
                           gavg, bg_ref[...], bb_ref[...]).astype(BF16)
    lane_group = lax.broadcasted_iota(jnp.int32, (B_CHUNK, B_WIDTH), 1) // (B_WIDTH // B_GROUPS)
    mixed = []
    for n in range(tile // B_CHUNK):
        r = jnp.dot(ws_ref[...], vn[n * B_CHUNK:(n + 1) * B_CHUNK], preferred_element_type=F32)
        m = bs_ref[...]
        for g in range(B_GROUPS):
            m = m + jnp.where(lane_group == g, r[g * B_CHUNK:(g + 1) * B_CHUNK], 0.0)
        mixed.append(m)
    yb = u * jnp.concatenate(mixed, axis=0)
    c0 = A_WIDTH + 2 * B_WIDTH
    ca = pm_ref[0, :, c0:c0 + C_WIDTH]
    cgate = pm_ref[0, :, c0 + C_WIDTH:c0 + 2 * C_WIDTH]
    z = ca * jax.nn.sigmoid(cgate)
    nseg = tile // seg
    pad_zeros = jnp.zeros((C_PAD, C_WIDTH), F32)
    convs = []
    for s in range(nseg):
        zpad_ref[s, 0:C_PAD, :] = pad_zeros
        zpad_ref[s, C_PAD + seg:2 * C_PAD + seg, :] = pad_zeros
        zpad_ref[s, C_PAD:C_PAD + seg, :] = z[s * seg:(s + 1) * seg]
    for s in range(nseg):
        acc = jnp.zeros((seg, C_WIDTH), F32)
        for j in range(C_KERNEL):
            start = C_PAD - C_KERNEL // 2 + j
            acc = acc + dww_ref[j:j + 1, :] * zpad_ref[s, start:start + seg, :]
        convs.append(acc)
    zc = jnp.concatenate(convs, axis=0) + dwb_ref[...]
    zc = _group_layer_norm(zc, gavg, cg_ref[...], cb_ref[...])
    zc = (zc * jax.nn.sigmoid(zc)).astype(BF16)
    yc = jnp.dot(zc, pww_ref[...], preferred_element_type=F32) + pwb_ref[...]
    y = (jnp.dot(ya.astype(BF16), wo_ref[0:A_WIDTH, :], preferred_element_type=F32)
         + jnp.dot(yb.astype(BF16), wo_ref[A_WIDTH:A_WIDTH + B_WIDTH, :], preferred_element_type=F32)
         + jnp.dot(yc.astype(BF16), wo_ref[A_WIDTH + B_WIDTH:, :], preferred_element_type=F32))
    o_ref[0] = x_ref[0] + mod_ref[0, 2:3, :] * y


def _mixer(x, o_f, o_b, pm, mod, lw, tile, seg):
    bsz, t_len, d = x.shape
    nseg = tile // seg

    def tok(cols):
        return pl.BlockSpec((1, tile, cols), lambda b, i: (b, i, 0))

    params = [lw['onorm_g'], lw['gavg'], lw['gmlp_ln_g'], lw['gmlp_ln_b'], lw['ws_cat'], lw['bs_tile'],
              lw['dw_w'], lw['dw_b'], lw['conv_ln_g'], lw['conv_ln_b'], lw['pw_w'], lw['pw_b'], lw['w_out']]
    return pl.pallas_call(
        functools.partial(_mixer_kernel, seg=seg),
        grid=(bsz, t_len // tile),
        in_specs=[tok(d), tok(A_WIDTH), tok(A_WIDTH), tok(PM_COLS),
                  pl.BlockSpec((1, 6, d), lambda b, i: (b, 0, 0))] + [_full(p.shape) for p in params],
        out_specs=tok(d),
        out_shape=jax.ShapeDtypeStruct(x.shape, F32),
        scratch_shapes=[pltpu.VMEM((nseg, seg + 2 * C_PAD, C_WIDTH), F32)],
        compiler_params=_cparams(("arbitrary", "arbitrary")),
        name="mixer",
    )(x, o_f, o_b, pm, mod, *params)


def _ffn_kernel(x_ref, mod_ref, g_ref, w13_ref, w2_ref, fg_ref, o_ref, *, d_ff, chunk, final_norm):
    x = x_ref[0]
    h = _norm_modulate(x, g_ref[...], mod_ref[0, 3:4, :], mod_ref[0, 4:5, :]).astype(BF16)
    acc = jnp.zeros(x.shape, F32)
    for c in range(d_ff // chunk):
        a = jnp.dot(h, w13_ref[:, c * chunk:(c + 1) * chunk], preferred_element_type=F32)
        b = jnp.dot(h, w13_ref[:, d_ff + c * chunk:d_ff + (c + 1) * chunk], preferred_element_type=F32)
        act = (a * jax.nn.sigmoid(a) * b).astype(BF16)
        acc = acc + jnp.dot(act, w2_ref[c * chunk:(c + 1) * chunk, :], preferred_element_type=F32)
    y = x + mod_ref[0, 5:6, :] * acc
    if final_norm:
        y = y * lax.rsqrt(jnp.mean(y * y, axis=-1, keepdims=True) + EPS) * fg_ref[...]
    o_ref[0] = y


def _ffn(x, mod, norm_g, w13, w2, final_g, tile, final_norm):
    bsz, t_len, d = x.shape
    d_ff = w2.shape[0]
    return pl.pallas_call(
        functools.partial(_ffn_kernel, d_ff=d_ff, chunk=256, final_norm=final_norm),
        grid=(bsz, t_len // tile),
        in_specs=[
            pl.BlockSpec((1, tile, d), lambda b, i: (b, i, 0)),
            pl.BlockSpec((1, 6, d), lambda b, i: (b, 0, 0)),
            _full((1, d)), _full(w13.shape), _full(w2.shape), _full((1, d)),
        ],
        out_specs=pl.BlockSpec((1, tile, d), lambda b, i: (b, i, 0)),
        out_shape=jax.ShapeDtypeStruct(x.shape, F32),
        compiler_params=_cparams(("arbitrary", "arbitrary")),
        name="ffn",
    )(x, mod, norm_g.reshape(1, d), w13, w2, final_g.reshape(1, d))


def _layer_params(l, gavg, norm1_g, w_in, hgrn_onorm_g, gmlp_ln_g, gmlp_ln_b, gmlp_w_s, gmlp_b_s,
                  conv_dw_w, conv_dw_b, conv_ln_g, conv_ln_b, conv_pw_w, conv_pw_b, w_out,
                  norm2_g, ffn_w13, ffn_w2):
    row = lambda a: a.reshape(1, -1).astype(F32)
    return {
        'norm1_g': norm1_g[l], 'w_in': w_in[l].astype(BF16),
        'onorm_g': row(hgrn_onorm_g[l]), 'gavg': gavg,
        'gmlp_ln_g': row(gmlp_ln_g[l]), 'gmlp_ln_b': row(gmlp_ln_b[l]),
        'ws_cat': gmlp_w_s[l].reshape(B_GROUPS * B_CHUNK, B_CHUNK).astype(BF16),
        'bs_tile': jnp.repeat(gmlp_b_s[l].T, B_WIDTH // B_GROUPS, axis=1).astype(F32),
        'dw_w': conv_dw_w[l].astype(F32), 'dw_b': row(conv_dw_b[l]),
        'conv_ln_g': row(conv_ln_g[l]), 'conv_ln_b': row(conv_ln_b[l]),
        'pw_w': conv_pw_w[l].astype(BF16), 'pw_b': row(conv_pw_b[l]),
        'w_out': w_out[l].astype(BF16),
        'norm2_g': norm2_g[l], 'ffn_w13': ffn_w13[l].astype(BF16), 'ffn_w2': ffn_w2[l].astype(BF16),
    }


def kernel(x, c, ctx, c_ctx, ada_w, ada_b, norm1_g, w_in, hgrn_lb_logits, hgrn_onorm_g, gmlp_ln_g, gmlp_ln_b, gmlp_w_s, gmlp_b_s, conv_dw_w, conv_dw_b, conv_ln_g, conv_ln_b, conv_pw_w, conv_pw_b, w_out, norm2_g, ffn_w13, ffn_w2, final_norm_g):
    bsz, t_len, d = x.shape
    ctx_len = ctx.shape[1]
    depth = ada_w.shape[0]
    lb_logits = hgrn_lb_logits.astype(F32)

    cs = jnp.concatenate([c, c_ctx[None, :], jnp.zeros((MOD_ROWS - bsz - 1, d), F32)], axis=0)
    mod_all = _modulation(cs, ada_w, ada_b)
    grp = jnp.arange(B_WIDTH) // (B_WIDTH // B_GROUPS)
    gavg = ((grp[:, None] == grp[None, :]).astype(F32) / (B_WIDTH // B_GROUPS)).astype(BF16)
    zero_state = jnp.zeros((bsz, 2, A_HEADS, A_DK, A_DK), F32)

    for l in range(depth):
        last = l == depth - 1
        lw = _layer_params(l, gavg, norm1_g, w_in, hgrn_onorm_g, gmlp_ln_g, gmlp_ln_b, gmlp_w_s,
                           gmlp_b_s, conv_dw_w, conv_dw_b, conv_ln_g, conv_ln_b, conv_pw_w,
                           conv_pw_b, w_out, norm2_g, ffn_w13, ffn_w2)
        mod = mod_all[l, :bsz].reshape(bsz, 6, d)
        mod_ctx = jnp.broadcast_to(mod_all[l, bsz].reshape(1, 6, d), (bsz, 6, d))
        if last:
            (pa_c,) = _in_proj(ctx, mod_ctx, lw['norm1_g'], lw['w_in'][:, :PA_COLS], ctx_len, False)
            _, _, state = _hgrn(pa_c, lb_logits, zero_state, l, ctx_len)
        else:
            pa_c, pm_c = _in_proj(ctx, mod_ctx, lw['norm1_g'], lw['w_in'], ctx_len, True)
            of_c, ob_c, state = _hgrn(pa_c, lb_logits, zero_state, l, ctx_len)
            ctx = _mixer(ctx, of_c, ob_c, pm_c, mod_ctx, lw, ctx_len, ctx_len)
            ctx = _ffn(ctx, mod_ctx, lw['norm2_g'], lw['ffn_w13'], lw['ffn_w2'], final_norm_g,
                       ctx_len, False)
        pa, pm = _in_proj(x, mod, lw['norm1_g'], lw['w_in'], 512, True)
        o_f, o_b, _ = _hgrn(pa, lb_logits, state, l, 512)
        x = _mixer(x, o_f, o_b, pm, mod, lw, 256, t_len // (t_len // GRID_W))
        x = _ffn(x, mod, lw['norm2_g'], lw['ffn_w13'], lw['ffn_w2'], final_norm_g, 512, last)
    return x
```

```python
import functools

import jax
import jax.numpy as jnp
from jax import lax
from jax.experimental import pallas as pl
from jax.experimental.pallas import tpu as pltpu

F32 = jnp.float32
BF16 = jnp.bfloat16

D_MODEL = 1024
GRID_W = 64
A_HEADS = 4
A_DK = 128
A_WIDTH = A_HEADS * A_DK
A_CHUNK = 64
A_SUB = 32
B_GROUPS = 4
B_WIDTH = 256
B_CHUNK = 128
C_GROUPS = 4
C_WIDTH = 256
C_KERNEL = 31
C_PAD = 16
SUBLANES = 8
PA_COLS = 4 * A_WIDTH
PM_COLS = A_WIDTH + 2 * B_WIDTH + 2 * C_WIDTH
EPS = 1e-6
EXP2_CLAMP = 80.0
LOG2E = 1.4426950408889634
MOD_ROWS = 8

V7X_VMEM_LIMIT = 56 * 1024 * 1024

_NT = (((1,), (1,)), ((), ()))
_TN = (((0,), (0,)), ((), ()))


def _cparams(sem):
    return pltpu.CompilerParams(dimension_semantics=sem, vmem_limit_bytes=V7X_VMEM_LIMIT)


def _full(shape):
    n = len(shape)
    return pl.BlockSpec(shape, lambda *_: (0,) * n)


def _mod_kernel(cs_ref, w_ref, b_ref, o_ref):
    s = cs_ref[...]
    s = s * jax.nn.sigmoid(s)
    o_ref[0] = jnp.dot(s, w_ref[0], preferred_element_type=F32,
                       precision=lax.Precision.HIGHEST) + b_ref[0]


def _modulation(cs, ada_w, ada_b):
    depth, d, n = ada_w.shape
    tn = 1024
    return pl.pallas_call(
        _mod_kernel,
        grid=(depth, n // tn),
        in_specs=[
            pl.BlockSpec((MOD_ROWS, d), lambda l, j: (0, 0)),
            pl.BlockSpec((1, d, tn), lambda l, j: (l, 0, j)),
            pl.BlockSpec((1, 1, tn), lambda l, j: (l, 0, j)),
        ],
        out_specs=pl.BlockSpec((1, MOD_ROWS, tn), lambda l, j: (l, 0, j)),
        out_shape=jax.ShapeDtypeStruct((depth, MOD_ROWS, n), F32),
        compiler_params=_cparams(("arbitrary", "arbitrary")),
        name="modulation",
    )(cs, ada_w, ada_b.reshape(depth, 1, n))


def _norm_modulate(x, g, shift, scale):
    ms = jnp.mean(x * x, axis=-1, keepdims=True)
    y = x * lax.rsqrt(ms + EPS) * g
    return y * (1.0 + scale) + shift


def _in_proj_kernel(x_ref, mod_ref, g_ref, w_ref, *out_refs):
    h = _norm_modulate(x_ref[0], g_ref[...], mod_ref[0, 0:1, :], mod_ref[0, 1:2, :]).astype(BF16)
    out_refs[0][0] = jnp.dot(h, w_ref[:, :PA_COLS], preferred_element_type=F32)
    if len(out_refs) > 1:
        out_refs[1][0] = jnp.dot(h, w_ref[:, PA_COLS:], preferred_element_type=F32)


def _in_proj(x, mod, norm_g, w_in, tile, with_mixers):
    bsz, t_len, d = x.shape
    cols = w_in.shape[1]
    out_shape = [jax.ShapeDtypeStruct((bsz, t_len, PA_COLS), F32)]
    out_specs = [pl.BlockSpec((1, tile, PA_COLS), lambda b, i: (b, i, 0))]
    if with_mixers:
        out_shape.append(jax.ShapeDtypeStruct((bsz, t_len, PM_COLS), F32))
        out_specs.append(pl.BlockSpec((1, tile, PM_COLS), lambda b, i: (b, i, 0)))
    return pl.pallas_call(
        _in_proj_kernel,
        grid=(bsz, t_len // tile),
        in_specs=[
            pl.BlockSpec((1, tile, d), lambda b, i: (b, i, 0)),
            pl.BlockSpec((1, 6, d), lambda b, i: (b, 0, 0)),
            _full((1, d)),
            _full((d, cols)),
        ],
        out_specs=out_specs,
        out_shape=out_shape,
        compiler_params=_cparams(("arbitrary", "arbitrary")),
        name="in_proj",
    )(x, mod, norm_g.reshape(1, d), w_in)


def _lower_bounds(logits, layer):
    depth = logits.shape[0]
    rows = [logits[j] for j in range(depth)]
    m = rows[0]
    for r in rows[1:]:
        m = jnp.maximum(m, r)
    es = [jnp.exp(r - m) for r in rows]
    tot = es[0]
    for e in es[1:]:
        tot = tot + e
    lb = jnp.zeros_like(m)
    for j in range(1, layer + 1):
        lb = lb + es[j] / tot
    return lb


def _chunk_rows(c, lo=0, n=A_CHUNK):
    return slice(c * A_CHUNK + lo, c * A_CHUNK + lo + n)


def _hgrn_factors(q, z, v, lb, rev, n_chunks):
    width = q.shape[1]
    sub = A_SUB
    row = lax.broadcasted_iota(jnp.int32, (A_CHUNK, A_CHUNK), 0)
    col = lax.broadcasted_iota(jnp.int32, (A_CHUNK, A_CHUNK), 1)
    tri = ((col >= row) if rev else (col <= row)).astype(F32).astype(BF16)
    oml = 1.0 - lb
    sg = jax.nn.sigmoid(z)
    lf = jnp.log(lb + oml * sg)
    kk = oml * (1.0 - sg)
    hi = lf.astype(BF16)
    lo = (lf - hi.astype(F32)).astype(BF16)
    bs = [(jnp.dot(tri, hi[_chunk_rows(c)], preferred_element_type=F32)
           + jnp.dot(tri, lo[_chunk_rows(c)], preferred_element_type=F32)) * LOG2E
          for c in range(n_chunks)]
    wide, narrow = (0, 1) if rev else (1, 0)
    mid_row = sub // 2 if rev else sub // 2 - 1
    last_row = 0 if rev else A_CHUNK - 1
    b_last, b_own, b_wide, b_narrow, decs = [], [], [], [], []
    for bc in bs:
        mids = [bc[i * sub + mid_row:i * sub + mid_row + 1] for i in range(2)]
        b_last.append(jnp.broadcast_to(bc[last_row:last_row + 1], (A_CHUNK, width)))
        b_own += [jnp.broadcast_to(m, (sub, width)) for m in mids]
        b_wide.append(jnp.broadcast_to(mids[wide], (A_CHUNK, width)))
        b_narrow.append(mids[narrow] - bc[narrow * sub:(narrow + 1) * sub])
        decs.append(jnp.exp2(bc[last_row:last_row + 1]))
    b = jnp.concatenate(bs, axis=0)
    kk_narrow = jnp.concatenate([kk[_chunk_rows(c, narrow * sub, sub)] for c in range(n_chunks)], axis=0)
    return {
        'qe': (q * jnp.exp2(b)).astype(BF16),
        'kd': (kk * jnp.exp2(jnp.concatenate(b_last, axis=0) - b)).astype(BF16),
        'qd': (q * jnp.exp2(jnp.minimum(b - jnp.concatenate(b_own, axis=0), EXP2_CLAMP))).astype(BF16),
        'k_wide': (kk * jnp.exp2(jnp.minimum(jnp.concatenate(b_wide, axis=0) - b, EXP2_CLAMP))).astype(BF16),
        'k_narrow': (kk_narrow * jnp.exp2(jnp.minimum(jnp.concatenate(b_narrow, axis=0), EXP2_CLAMP))).astype(BF16),
        'v': v.astype(BF16),
        'dec': decs,
    }


def _hgrn_scores(f, rev, n_chunks):
    sub = A_SUB
    wide, narrow = (0, 1) if rev else (1, 0)
    row_n = lax.broadcasted_iota(jnp.int32, (sub, sub), 0)
    col_n = lax.broadcasted_iota(jnp.int32, (sub, sub), 1)
    row_w = lax.broadcasted_iota(jnp.int32, (sub, A_CHUNK), 0)
    col_w = lax.broadcasted_iota(jnp.int32, (sub, A_CHUNK), 1)
    mask_n = (col_n >= row_n) if rev else (col_n <= row_n)
    mask_w = (col_w >= row_w) if rev else (col_w <= row_w + sub)
    out = []
    for c in range(n_chunks):
        per_head = []
        for h in range(A_HEADS):
            hc = slice(h * A_DK, (h + 1) * A_DK)
            s_n = lax.dot_general(f['qd'][_chunk_rows(c, narrow * sub, sub), hc],
                                  f['k_narrow'][c * sub:(c + 1) * sub, hc], _NT, preferred_element_type=F32)
            s_w = lax.dot_general(f['qd'][_chunk_rows(c, wide * sub, sub), hc],
                                  f['k_wide'][_chunk_rows(c), hc], _NT, preferred_element_type=F32)
            per_head.append((jnp.where(mask_n, s_n, 0.0).astype(BF16),
                             jnp.where(mask_w, s_w, 0.0).astype(BF16)))
        out.append(per_head)
    return out


def _hgrn_intra(f, scores, rev, n_chunks):
    sub = A_SUB
    wide, narrow = (0, 1) if rev else (1, 0)
    out = []
    for c in range(n_chunks):
        per_head = []
        for h in range(A_HEADS):
            hc = slice(h * A_DK, (h + 1) * A_DK)
            s_n, s_w = scores[c][h]
            o_n = jnp.dot(s_n, f['v'][_chunk_rows(c, narrow * sub, sub), hc], preferred_element_type=F32)
            o_w = jnp.dot(s_w, f['v'][_chunk_rows(c), hc], preferred_element_type=F32)
            intra = jnp.concatenate([o_w, o_n] if rev else [o_n, o_w], axis=0)
            upd = lax.dot_general(f['v'][_chunk_rows(c), hc], f['kd'][_chunk_rows(c), hc], _TN,
                                  preferred_element_type=F32)
            per_head.append((intra, upd))
        out.append(per_head)
    return out


def _hgrn_scan(f, intra, st_ref, d, o_ref, rev, n_chunks):
    states = [st_ref[0, d, h] for h in range(A_HEADS)]
    for c in (range(n_chunks - 1, -1, -1) if rev else range(n_chunks)):
        outs = []
        for h in range(A_HEADS):
            hc = slice(h * A_DK, (h + 1) * A_DK)
            inter = lax.dot_general(f['qe'][_chunk_rows(c), hc], states[h].astype(BF16), _NT,
                                    preferred_element_type=F32)
            outs.append(inter + intra[c][h][0])
            states[h] = states[h] * f['dec'][c][:, hc] + intra[c][h][1]
        o_ref[0, _chunk_rows(c), :] = jnp.concatenate(outs, axis=1)
    for h in range(A_HEADS):
        st_ref[0, d, h] = states[h]


def _hgrn_kernel(qf_ref, zf_ref, vf_ref, qb_ref, zb_ref, vb_ref, lbl_ref, s0_ref,
                 of_ref, ob_ref, st_ref, *, layer, n_chunks):
    @pl.when(pl.program_id(1) == 0)
    def _():
        st_ref[...] = s0_ref[...]

    lb = _lower_bounds(lbl_ref[...], layer)
    dirs = ((qf_ref, zf_ref, vf_ref, of_ref, False), (qb_ref, zb_ref, vb_ref, ob_ref, True))
    facs = [_hgrn_factors(q[0], z[0], v[0], lb[d:d + 1], rev, n_chunks)
            for d, (q, z, v, _, rev) in enumerate(dirs)]
    scores = [_hgrn_scores(facs[d], dirs[d][4], n_chunks) for d in range(2)]
    intra = [_hgrn_intra(facs[d], scores[d], dirs[d][4], n_chunks) for d in range(2)]
    for d in range(2):
        _hgrn_scan(facs[d], intra[d], st_ref, d, dirs[d][3], dirs[d][4], n_chunks)


def _hgrn(pa, lb_logits, s0, layer, tile):
    bsz, t_len, _ = pa.shape
    nb = t_len // tile
    depth = lb_logits.shape[0]

    def col_spec(colblk, rev):
        if rev:
            return pl.BlockSpec((1, tile, A_WIDTH), lambda b, i: (b, nb - 1 - i, colblk))
        return pl.BlockSpec((1, tile, A_WIDTH), lambda b, i: (b, i, colblk))

    st_spec = pl.BlockSpec((1, 2, A_HEADS, A_DK, A_DK), lambda b, i: (b, 0, 0, 0, 0))
    o_sds = jax.ShapeDtypeStruct((bsz, t_len, A_WIDTH), F32)
    return pl.pallas_call(
        functools.partial(_hgrn_kernel, layer=layer, n_chunks=tile // A_CHUNK),
        grid=(bsz, nb),
        in_specs=[col_spec(0, False), col_spec(1, False), col_spec(3, False),
                  col_spec(0, True), col_spec(2, True), col_spec(3, True),
                  _full((depth, 2, A_WIDTH)), st_spec],
        out_specs=[col_spec(0, False), col_spec(0, True), st_spec],
        out_shape=[o_sds, o_sds, jax.ShapeDtypeStruct(s0.shape, F32)],
        compiler_params=_cparams(("arbitrary", "arbitrary")),
        name="hgrn",
    )(pa, pa, pa, pa, pa, pa, lb_logits, s0)


def _group_mean(x, gavg):
    hi = x.astype(BF16)
    lo = (x - hi.astype(F32)).astype(BF16)
    return (jnp.dot(hi, gavg, preferred_element_type=F32)
            + jnp.dot(lo, gavg, preferred_element_type=F32))


def _group_layer_norm(x, gavg, g, b):
    mu = _group_mean(x, gavg)
    xc = x - mu
    var = _group_mean(xc * xc, gavg)
    return xc * lax.rsqrt(var + EPS) * g + b


def _mixer_kernel(x_ref, of_ref, ob_ref, pm_ref, mod_ref, og_ref, gavg_ref,
                  bg_ref, bb_ref, ws_ref, bs_ref,
                  dww_ref, dwb_ref, cg_ref, cb_ref, pww_ref, pwb_ref, wo_ref,
                  o_ref, zpad_ref, zsh_ref, *, seg):
    tile = x_ref.shape[1]
    gavg = gavg_ref[...]
    o = of_ref[0] + ob_ref[0]
    parts = []
    for h in range(A_HEADS):
        oh = o[:, h * A_DK:(h + 1) * A_DK]
        parts.append(oh * lax.rsqrt(jnp.mean(oh * oh, axis=-1, keepdims=True) + EPS))
    gate = pm_ref[0, :, 0:A_WIDTH]
    ya = jnp.concatenate(parts, axis=1) * og_ref[...] * (gate * jax.nn.sigmoid(gate))
    u = jax.nn.gelu(pm_ref[0, :, A_WIDTH:A_WIDTH + B_WIDTH])
    vn = _group_layer_norm(jax.nn.gelu(pm_ref[0, :, A_WIDTH + B_WIDTH:A_WIDTH + 2 * B_WIDTH]),
                           gavg, bg_ref[...], bb_ref[...]).astype(BF16)
    lane_group = lax.broadcasted_iota(jnp.int32, (B_CHUNK, B_WIDTH), 1) // (B_WIDTH // B_GROUPS)
    mixed = []
    for n in range(tile // B_CHUNK):
        r = jnp.dot(ws_ref[...], vn[n * B_CHUNK:(n + 1) * B_CHUNK], preferred_element_type=F32)
        m = bs_ref[...]
        for g in range(B_GROUPS):
            m = m + jnp.where(lane_group == g, r[g * B_CHUNK:(g + 1) * B_CHUNK], 0.0)
        mixed.append(m)
    yb = u * jnp.concatenate(mixed, axis=0)
    c0 = A_WIDTH + 2 * B_WIDTH
    ca = pm_ref[0, :, c0:c0 + C_WIDTH]
    cgate = pm_ref[0, :, c0 + C_WIDTH:c0 + 2 * C_WIDTH]
    z = ca * jax.nn.sigmoid(cgate)
    nseg = tile // seg
    pad_zeros = jnp.zeros((C_PAD, C_WIDTH), F32)
    convs = []
    for s in range(nseg):
        zpad_ref[s, 0:C_PAD, :] = pad_zeros
        zpad_ref[s, C_PAD + seg:2 * C_PAD + seg, :] = pad_zeros
        zpad_ref[s, C_PAD:C_PAD + seg, :] = z[s * seg:(s + 1) * seg]
    span = seg + 2 * C_PAD - SUBLANES
    for s in range(nseg):
        for r in range(1, SUBLANES):
            zsh_ref[s, r] = zpad_ref[s, r:r + span, :]
    for s in range(nseg):
        acc = jnp.zeros((seg, C_WIDTH), F32)
        for j in range(C_KERNEL):
            start = C_PAD - C_KERNEL // 2 + j
            r, a = start % SUBLANES, start - start % SUBLANES
            tap = zpad_ref[s, a:a + seg, :] if r == 0 else zsh_ref[s, r, a:a + seg, :]
            acc = acc + dww_ref[j:j + 1, :] * tap
        convs.append(acc)
    zc = jnp.concatenate(convs, axis=0) + dwb_ref[...]
    zc = _group_layer_norm(zc, gavg, cg_ref[...], cb_ref[...])
    zc = (zc * jax.nn.sigmoid(zc)).astype(BF16)
    yc = jnp.dot(zc, pww_ref[...], preferred_element_type=F32) + pwb_ref[...]
    y = (jnp.dot(ya.astype(BF16), wo_ref[0:A_WIDTH, :], preferred_element_type=F32)
         + jnp.dot(yb.astype(BF16), wo_ref[A_WIDTH:A_WIDTH + B_WIDTH, :], preferred_element_type=F32)
         + jnp.dot(yc.astype(BF16), wo_ref[A_WIDTH + B_WIDTH:, :], preferred_element_type=F32))
    o_ref[0] = x_ref[0] + mod_ref[0, 2:3, :] * y


def _mixer(x, o_f, o_b, pm, mod, lw, tile, seg):
    bsz, t_len, d = x.shape
    nseg = tile // seg

    def tok(cols):
        return pl.BlockSpec((1, tile, cols), lambda b, i: (b, i, 0))

    params = [lw['onorm_g'], lw['gavg'], lw['gmlp_ln_g'], lw['gmlp_ln_b'], lw['ws_cat'], lw['bs_tile'],
              lw['dw_w'], lw['dw_b'], lw['conv_ln_g'], lw['conv_ln_b'], lw['pw_w'], lw['pw_b'], lw['w_out']]
    return pl.pallas_call(
        functools.partial(_mixer_kernel, seg=seg),
        grid=(bsz, t_len // tile),
        in_specs=[tok(d), tok(A_WIDTH), tok(A_WIDTH), tok(PM_COLS),
                  pl.BlockSpec((1, 6, d), lambda b, i: (b, 0, 0))] + [_full(p.shape) for p in params],
        out_specs=tok(d),
        out_shape=jax.ShapeDtypeStruct(x.shape, F32),
        scratch_shapes=[pltpu.VMEM((nseg, seg + 2 * C_PAD, C_WIDTH), F32),
                        pltpu.VMEM((nseg, SUBLANES, seg + 2 * C_PAD - SUBLANES, C_WIDTH), F32)],
        compiler_params=_cparams(("arbitrary", "arbitrary")),
        name="mixer",
    )(x, o_f, o_b, pm, mod, *params)


def _ffn_kernel(x_ref, mod_ref, g_ref, w13_ref, w2_ref, fg_ref, o_ref, *, d_ff, chunk, final_norm):
    x = x_ref[0]
    h = _norm_modulate(x, g_ref[...], mod_ref[0, 3:4, :], mod_ref[0, 4:5, :]).astype(BF16)
    acc = jnp.zeros(x.shape, F32)
    for c in range(d_ff // chunk):
        a = jnp.dot(h, w13_ref[:, c * chunk:(c + 1) * chunk], preferred_element_type=F32)
        b = jnp.dot(h, w13_ref[:, d_ff + c * chunk:d_ff + (c + 1) * chunk], preferred_element_type=F32)
        act = (a * jax.nn.sigmoid(a) * b).astype(BF16)
        acc = acc + jnp.dot(act, w2_ref[c * chunk:(c + 1) * chunk, :], preferred_element_type=F32)
    y = x + mod_ref[0, 5:6, :] * acc
    if final_norm:
        y = y * lax.rsqrt(jnp.mean(y * y, axis=-1, keepdims=True) + EPS) * fg_ref[...]
    o_ref[0] = y


def _ffn(x, mod, norm_g, w13, w2, final_g, tile, final_norm):
    bsz, t_len, d = x.shape
    d_ff = w2.shape[0]
    return pl.pallas_call(
        functools.partial(_ffn_kernel, d_ff=d_ff, chunk=256, final_norm=final_norm),
        grid=(bsz, t_len // tile),
        in_specs=[
            pl.BlockSpec((1, tile, d), lambda b, i: (b, i, 0)),
            pl.BlockSpec((1, 6, d), lambda b, i: (b, 0, 0)),
            _full((1, d)), _full(w13.shape), _full(w2.shape), _full((1, d)),
        ],
        out_specs=pl.BlockSpec((1, tile, d), lambda b, i: (b, i, 0)),
        out_shape=jax.ShapeDtypeStruct(x.shape, F32),
        compiler_params=_cparams(("arbitrary", "arbitrary")),
        name="ffn",
    )(x, mod, norm_g.reshape(1, d), w13, w2, final_g.reshape(1, d))


def _layer_params(l, gavg, norm1_g, w_in, hgrn_onorm_g, gmlp_ln_g, gmlp_ln_b, gmlp_w_s, gmlp_b_s,
                  conv_dw_w, conv_dw_b, conv_ln_g, conv_ln_b, conv_pw_w, conv_pw_b, w_out,
                  norm2_g, ffn_w13, ffn_w2):
    row = lambda a: a.reshape(1, -1).astype(F32)
    return {
        'norm1_g': norm1_g[l], 'w_in': w_in[l].astype(BF16),
        'onorm_g': row(hgrn_onorm_g[l]), 'gavg': gavg,
        'gmlp_ln_g': row(gmlp_ln_g[l]), 'gmlp_ln_b': row(gmlp_ln_b[l]),
        'ws_cat': gmlp_w_s[l].reshape(B_GROUPS * B_CHUNK, B_CHUNK).astype(BF16),
        'bs_tile': jnp.repeat(gmlp_b_s[l].T, B_WIDTH // B_GROUPS, axis=1).astype(F32),
        'dw_w': conv_dw_w[l].astype(F32), 'dw_b': row(conv_dw_b[l]),
        'conv_ln_g': row(conv_ln_g[l]), 'conv_ln_b': row(conv_ln_b[l]),
        'pw_w': conv_pw_w[l].astype(BF16), 'pw_b': row(conv_pw_b[l]),
        'w_out': w_out[l].astype(BF16),
        'norm2_g': norm2_g[l], 'ffn_w13': ffn_w13[l].astype(BF16), 'ffn_w2': ffn_w2[l].astype(BF16),
    }


def kernel(x, c, ctx, c_ctx, ada_w, ada_b, norm1_g, w_in, hgrn_lb_logits, hgrn_onorm_g, gmlp_ln_g, gmlp_ln_b, gmlp_w_s, gmlp_b_s, conv_dw_w, conv_dw_b, conv_ln_g, conv_ln_b, conv_pw_w, conv_pw_b, w_out, norm2_g, ffn_w13, ffn_w2, final_norm_g):
    bsz, t_len, d = x.shape
    ctx_len = ctx.shape[1]
    depth = ada_w.shape[0]
    lb_logits = hgrn_lb_logits.astype(F32)

    cs = jnp.concatenate([c, c_ctx[None, :], jnp.zeros((MOD_ROWS - bsz - 1, d), F32)], axis=0)
    mod_all = _modulation(cs, ada_w, ada_b)
    grp = jnp.arange(B_WIDTH) // (B_WIDTH // B_GROUPS)
    gavg = ((grp[:, None] == grp[None, :]).astype(F32) / (B_WIDTH // B_GROUPS)).astype(BF16)
    zero_state = jnp.zeros((bsz, 2, A_HEADS, A_DK, A_DK), F32)

    for l in range(depth):
        last = l == depth - 1
        lw = _layer_params(l, gavg, norm1_g, w_in, hgrn_onorm_g, gmlp_ln_g, gmlp_ln_b, gmlp_w_s,
                           gmlp_b_s, conv_dw_w, conv_dw_b, conv_ln_g, conv_ln_b, conv_pw_w,
                           conv_pw_b, w_out, norm2_g, ffn_w13, ffn_w2)
        mod = mod_all[l, :bsz].reshape(bsz, 6, d)
        mod_ctx = jnp.broadcast_to(mod_all[l, bsz].reshape(1, 6, d), (bsz, 6, d))
        if last:
            (pa_c,) = _in_proj(ctx, mod_ctx, lw['norm1_g'], lw['w_in'][:, :PA_COLS], ctx_len, False)
            _, _, state = _hgrn(pa_c, lb_logits, zero_state, l, ctx_len)
        else:
            pa_c, pm_c = _in_proj(ctx, mod_ctx, lw['norm1_g'], lw['w_in'], ctx_len, True)
            of_c, ob_c, state = _hgrn(pa_c, lb_logits, zero_state, l, ctx_len)
            ctx = _mixer(ctx, of_c, ob_c, pm_c, mod_ctx, lw, ctx_len, ctx_len)
            ctx = _ffn(ctx, mod_ctx, lw['norm2_g'], lw['ffn_w13'], lw['ffn_w2'], final_norm_g,
                       ctx_len, False)
        pa, pm = _in_proj(x, mod, lw['norm1_g'], lw['w_in'], 512, True)
        o_f, o_b, _ = _hgrn(pa, lb_logits, state, l, 512)
        x = _mixer(x, o_f, o_b, pm, mod, lw, 512, t_len // (t_len // GRID_W))
        x = _ffn(x, mod, lw['norm2_g'], lw['ffn_w13'], lw['ffn_w2'], final_norm_g, 512, last)
    return x
```

```python
import functools

import jax
import jax.numpy as jnp
from jax import lax
from jax.experimental import pallas as pl
from jax.experimental.pallas import tpu as pltpu

F32 = jnp.float32
BF16 = jnp.bfloat16

D_MODEL = 1024
GRID_W = 64
A_HEADS = 4
A_DK = 128
A_WIDTH = A_HEADS * A_DK
A_CHUNK = 64
A_SUB = 32
B_GROUPS = 4
B_WIDTH = 256
B_CHUNK = 128
C_GROUPS = 4
C_WIDTH = 256
C_KERNEL = 31
C_PAD = 16
SUBLANES = 8
PROJ_CHUNK = 256
PA_COLS = 4 * A_WIDTH
PM_COLS = A_WIDTH + 2 * B_WIDTH + 2 * C_WIDTH
EPS = 1e-6
EXP2_CLAMP = 80.0
LOG2E = 1.4426950408889634
MOD_ROWS = 8

V7X_VMEM_LIMIT = 56 * 1024 * 1024

_NT = (((1,), (1,)), ((), ()))
_TN = (((0,), (0,)), ((), ()))


def _cparams(sem, flags=None):
    return pltpu.CompilerParams(dimension_semantics=sem, vmem_limit_bytes=V7X_VMEM_LIMIT, flags=flags)


def _full(shape):
    n = len(shape)
    return pl.BlockSpec(shape, lambda *_: (0,) * n)


def _mod_kernel(cs_ref, w_ref, b_ref, o_ref):
    s = cs_ref[...]
    s = s * jax.nn.sigmoid(s)
    o_ref[0] = jnp.dot(s, w_ref[0], preferred_element_type=F32,
                       precision=lax.Precision.HIGHEST) + b_ref[0]


def _modulation(cs, ada_w, ada_b):
    depth, d, n = ada_w.shape
    tn = 1024
    return pl.pallas_call(
        _mod_kernel,
        grid=(depth, n // tn),
        in_specs=[
            pl.BlockSpec((MOD_ROWS, d), lambda l, j: (0, 0)),
            pl.BlockSpec((1, d, tn), lambda l, j: (l, 0, j)),
            pl.BlockSpec((1, 1, tn), lambda l, j: (l, 0, j)),
        ],
        out_specs=pl.BlockSpec((1, MOD_ROWS, tn), lambda l, j: (l, 0, j)),
        out_shape=jax.ShapeDtypeStruct((depth, MOD_ROWS, n), F32),
        compiler_params=_cparams(("arbitrary", "arbitrary")),
        name="modulation",
    )(cs, ada_w, ada_b.reshape(depth, 1, n))


def _norm_modulate(x, g, shift, scale):
    ms = jnp.mean(x * x, axis=-1, keepdims=True)
    y = x * lax.rsqrt(ms + EPS) * g
    return y * (1.0 + scale) + shift


def _group_mean(x, gavg):
    hi = x.astype(BF16)
    lo = (x - hi.astype(F32)).astype(BF16)
    return (jnp.dot(hi, gavg, preferred_element_type=F32)
            + jnp.dot(lo, gavg, preferred_element_type=F32))


def _group_layer_norm(x, gavg, g, b):
    mu = _group_mean(x, gavg)
    xc = x - mu
    var = _group_mean(xc * xc, gavg)
    return xc * lax.rsqrt(var + EPS) * g + b


def _gmlp_branch(u, v, gavg, ln_g, ln_b, ws_ref, bs_ref):
    tile = u.shape[0]
    vn = _group_layer_norm(jax.nn.gelu(v), gavg, ln_g, ln_b).astype(BF16)
    lane_group = lax.broadcasted_iota(jnp.int32, (B_CHUNK, B_WIDTH), 1) // (B_WIDTH // B_GROUPS)
    mixed = []
    for n in range(tile // B_CHUNK):
        r = jnp.dot(ws_ref[...], vn[n * B_CHUNK:(n + 1) * B_CHUNK], preferred_element_type=F32)
        m = bs_ref[...]
        for g in range(B_GROUPS):
            m = m + jnp.where(lane_group == g, r[g * B_CHUNK:(g + 1) * B_CHUNK], 0.0)
        mixed.append(m)
    return jax.nn.gelu(u) * jnp.concatenate(mixed, axis=0)


def _conv_branch(ca, cgate, gavg, dww_ref, dw_b, ln_g, ln_b, pww_ref, pw_b, zpad_ref, zsh_ref, seg,
                 between):
    tile = ca.shape[0]
    z = ca * jax.nn.sigmoid(cgate)
    nseg = tile // seg
    pad_zeros = jnp.zeros((C_PAD, C_WIDTH), F32)
    for s in range(nseg):
        zpad_ref[s, 0:C_PAD, :] = pad_zeros
        zpad_ref[s, C_PAD + seg:2 * C_PAD + seg, :] = pad_zeros
        zpad_ref[s, C_PAD:C_PAD + seg, :] = z[s * seg:(s + 1) * seg]
    span = seg + 2 * C_PAD - SUBLANES
    convs = []
    for s in range(nseg):
        for r in range(1, SUBLANES):
            zsh_ref[s, r] = zpad_ref[s, r:r + span, :]
        acc = jnp.zeros((seg, C_WIDTH), F32)
        for j in range(C_KERNEL):
            start = C_PAD - C_KERNEL // 2 + j
            r, a = start % SUBLANES, start - start % SUBLANES
            tap = zpad_ref[s, a:a + seg, :] if r == 0 else zsh_ref[s, r, a:a + seg, :]
            acc = acc + dww_ref[j:j + 1, :] * tap
        convs.append(acc)
        between()
    zc = jnp.concatenate(convs, axis=0) + dw_b
    zc = _group_layer_norm(zc, gavg, ln_g, ln_b)
    zc = (zc * jax.nn.sigmoid(zc)).astype(BF16)
    return jnp.dot(zc, pww_ref[...], preferred_element_type=F32) + pw_b


def _in_proj_kernel(x_ref, mod_ref, g_ref, w_ref, pa_ref):
    h = _norm_modulate(x_ref[0], g_ref[...], mod_ref[0, 0:1, :], mod_ref[0, 1:2, :]).astype(BF16)
    pa_ref[0] = jnp.dot(h, w_ref[...], preferred_element_type=F32)


def _in_proj_mix_kernel(x_ref, mod_ref, g_ref, w_ref, gavg_ref, bg_ref, bb_ref, ws_ref, bs_ref,
                        dww_ref, dwb_ref, cg_ref, cb_ref, pww_ref, pwb_ref,
                        pa_ref, gate_ref, ybc_ref, zpad_ref, zsh_ref, *, seg):
    h = _norm_modulate(x_ref[0], g_ref[...], mod_ref[0, 0:1, :], mod_ref[0, 1:2, :]).astype(BF16)
    gavg = gavg_ref[...]

    def proj(lo, n):
        return jnp.dot(h, w_ref[:, lo:lo + n], preferred_element_type=F32)

    chunks = [(pa_ref, lo) for lo in range(0, PA_COLS, PROJ_CHUNK)]
    chunks += [(gate_ref, lo) for lo in range(0, A_WIDTH, PROJ_CHUNK)]
    pending = iter(chunks)

    def project_next():
        nxt = next(pending, None)
        if nxt is not None:
            ref, lo = nxt
            w_lo = lo if ref is pa_ref else PA_COLS + lo
            ref[0, :, lo:lo + PROJ_CHUNK] = proj(w_lo, PROJ_CHUNK)

    c0 = PA_COLS + A_WIDTH + 2 * B_WIDTH
    yc = _conv_branch(proj(c0, C_WIDTH), proj(c0 + C_WIDTH, C_WIDTH), gavg, dww_ref, dwb_ref[...],
                      cg_ref[...], cb_ref[...], pww_ref, pwb_ref[...], zpad_ref, zsh_ref, seg,
                      project_next)
    project_next()
    b0 = PA_COLS + A_WIDTH
    yb = _gmlp_branch(proj(b0, B_WIDTH), proj(b0 + B_WIDTH, B_WIDTH), gavg, bg_ref[...], bb_ref[...],
                      ws_ref, bs_ref)
    ybc_ref[0] = jnp.concatenate([yb, yc], axis=1).astype(BF16)
    for _ in chunks:
        project_next()


def _in_proj(x, mod, norm_g, w_in, tile):
    bsz, t_len, d = x.shape
    cols = w_in.shape[1]
    return pl.pallas_call(
        _in_proj_kernel,
        grid=(bsz, t_len // tile),
        in_specs=[
            pl.BlockSpec((1, tile, d), lambda b, i: (b, i, 0)),
            pl.BlockSpec((1, 6, d), lambda b, i: (b, 0, 0)),
            _full((1, d)),
            _full((d, cols)),
        ],
        out_specs=pl.BlockSpec((1, tile, cols), lambda b, i: (b, i, 0)),
        out_shape=jax.ShapeDtypeStruct((bsz, t_len, cols), F32),
        compiler_params=_cparams(("arbitrary", "arbitrary")),
        name="in_proj",
    )(x, mod, norm_g.reshape(1, d), w_in)


def _in_proj_mix(x, mod, lw, tile, seg):
    bsz, t_len, d = x.shape
    nseg = tile // seg

    def tok(cols):
        return pl.BlockSpec((1, tile, cols), lambda b, i: (b, i, 0))

    params = [lw['norm1_g'], lw['w_in'], lw['gavg'], lw['gmlp_ln_g'], lw['gmlp_ln_b'], lw['ws_cat'],
              lw['bs_tile'], lw['dw_w'], lw['dw_b'], lw['conv_ln_g'], lw['conv_ln_b'], lw['pw_w'], lw['pw_b']]
    return pl.pallas_call(
        functools.partial(_in_proj_mix_kernel, seg=seg),
        grid=(bsz, t_len // tile),
        in_specs=[tok(d), pl.BlockSpec((1, 6, d), lambda b, i: (b, 0, 0))] + [_full(p.shape) for p in params],
        out_specs=[tok(PA_COLS), tok(A_WIDTH), tok(B_WIDTH + C_WIDTH)],
        out_shape=[jax.ShapeDtypeStruct((bsz, t_len, PA_COLS), F32),
                   jax.ShapeDtypeStruct((bsz, t_len, A_WIDTH), F32),
                   jax.ShapeDtypeStruct((bsz, t_len, B_WIDTH + C_WIDTH), BF16)],
        scratch_shapes=[pltpu.VMEM((nseg, seg + 2 * C_PAD, C_WIDTH), F32),
                        pltpu.VMEM((nseg, SUBLANES, seg + 2 * C_PAD - SUBLANES, C_WIDTH), F32)],
        compiler_params=_cparams(("arbitrary", "arbitrary")),
        name="in_proj_mix",
    )(x, mod, *params)


def _lower_bounds(logits, layer):
    depth = logits.shape[0]
    rows = [logits[j] for j in range(depth)]
    m = rows[0]
    for r in rows[1:]:
        m = jnp.maximum(m, r)
    es = [jnp.exp(r - m) for r in rows]
    tot = es[0]
    for e in es[1:]:
        tot = tot + e
    lb = jnp.zeros_like(m)
    for j in range(1, layer + 1):
        lb = lb + es[j] / tot
    return lb


def _chunk_rows(c, lo=0, n=A_CHUNK):
    return slice(c * A_CHUNK + lo, c * A_CHUNK + lo + n)


def _hgrn_factors(q, z, v, lb, rev, n_chunks):
    width = q.shape[1]
    sub = A_SUB
    row = lax.broadcasted_iota(jnp.int32, (A_CHUNK, A_CHUNK), 0)
    col = lax.broadcasted_iota(jnp.int32, (A_CHUNK, A_CHUNK), 1)
    tri = ((col >= row) if rev else (col <= row)).astype(F32).astype(BF16)
    oml = 1.0 - lb
    sg = jax.nn.sigmoid(z)
    lf = jnp.log(lb + oml * sg)
    kk = oml * (1.0 - sg)
    hi = lf.astype(BF16)
    lo = (lf - hi.astype(F32)).astype(BF16)
    bs = [(jnp.dot(tri, hi[_chunk_rows(c)], preferred_element_type=F32)
           + jnp.dot(tri, lo[_chunk_rows(c)], preferred_element_type=F32)) * LOG2E
          for c in range(n_chunks)]
    wide, narrow = (0, 1) if rev else (1, 0)
    mid_row = sub // 2 if rev else sub // 2 - 1
    last_row = 0 if rev else A_CHUNK - 1
    b_last, b_own, b_wide, b_narrow, decs = [], [], [], [], []
    for bc in bs:
        mids = [bc[i * sub + mid_row:i * sub + mid_row + 1] for i in range(2)]
        b_last.append(jnp.broadcast_to(bc[last_row:last_row + 1], (A_CHUNK, width)))
        b_own += [jnp.broadcast_to(m, (sub, width)) for m in mids]
        b_wide.append(jnp.broadcast_to(mids[wide], (A_CHUNK, width)))
        b_narrow.append(mids[narrow] - bc[narrow * sub:(narrow + 1) * sub])
        decs.append(jnp.exp2(bc[last_row:last_row + 1]))
    b = jnp.concatenate(bs, axis=0)
    kk_narrow = jnp.concatenate([kk[_chunk_rows(c, narrow * sub, sub)] for c in range(n_chunks)], axis=0)
    return {
        'qe': (q * jnp.exp2(b)).astype(BF16),
        'kd': (kk * jnp.exp2(jnp.concatenate(b_last, axis=0) - b)).astype(BF16),
        'qd': (q * jnp.exp2(jnp.minimum(b - jnp.concatenate(b_own, axis=0), EXP2_CLAMP))).astype(BF16),
        'k_wide': (kk * jnp.exp2(jnp.minimum(jnp.concatenate(b_wide, axis=0) - b, EXP2_CLAMP))).astype(BF16),
        'k_narrow': (kk_narrow * jnp.exp2(jnp.minimum(jnp.concatenate(b_narrow, axis=0), EXP2_CLAMP))).astype(BF16),
        'v': v.astype(BF16),
        'dec': decs,
    }


def _hgrn_scores(f, rev, n_chunks):
    sub = A_SUB
    wide, narrow = (0, 1) if rev else (1, 0)
    row_n = lax.broadcasted_iota(jnp.int32, (sub, sub), 0)
    col_n = lax.broadcasted_iota(jnp.int32, (sub, sub), 1)
    row_w = lax.broadcasted_iota(jnp.int32, (sub, A_CHUNK), 0)
    col_w = lax.broadcasted_iota(jnp.int32, (sub, A_CHUNK), 1)
    mask_n = (col_n >= row_n) if rev else (col_n <= row_n)
    mask_w = (col_w >= row_w) if rev else (col_w <= row_w + sub)
    out = []
    for c in range(n_chunks):
        per_head = []
        for h in range(A_HEADS):
            hc = slice(h * A_DK, (h + 1) * A_DK)
            s_n = lax.dot_general(f['qd'][_chunk_rows(c, narrow * sub, sub), hc],
                                  f['k_narrow'][c * sub:(c + 1) * sub, hc], _NT, preferred_element_type=F32)
            s_w = lax.dot_general(f['qd'][_chunk_rows(c, wide * sub, sub), hc],
                                  f['k_wide'][_chunk_rows(c), hc], _NT, preferred_element_type=F32)
            per_head.append((jnp.where(mask_n, s_n, 0.0).astype(BF16),
                             jnp.where(mask_w, s_w, 0.0).astype(BF16)))
        out.append(per_head)
    return out


def _hgrn_intra(f, scores, rev, n_chunks):
    sub = A_SUB
    wide, narrow = (0, 1) if rev else (1, 0)
    out = []
    for c in range(n_chunks):
        per_head = []
        for h in range(A_HEADS):
            hc = slice(h * A_DK, (h + 1) * A_DK)
            s_n, s_w = scores[c][h]
            o_n = jnp.dot(s_n, f['v'][_chunk_rows(c, narrow * sub, sub), hc], preferred_element_type=F32)
            o_w = jnp.dot(s_w, f['v'][_chunk_rows(c), hc], preferred_element_type=F32)
            intra = jnp.concatenate([o_w, o_n] if rev else [o_n, o_w], axis=0)
            upd = lax.dot_general(f['v'][_chunk_rows(c), hc], f['kd'][_chunk_rows(c), hc], _TN,
                                  preferred_element_type=F32)
            per_head.append((intra, upd))
        out.append(per_head)
    return out


def _hgrn_scan(f, intra, st_ref, d, o_ref, rev, n_chunks):
    states = [st_ref[0, d, h] for h in range(A_HEADS)]
    for c in (range(n_chunks - 1, -1, -1) if rev else range(n_chunks)):
        outs = []
        for h in range(A_HEADS):
            hc = slice(h * A_DK, (h + 1) * A_DK)
            inter = lax.dot_general(f['qe'][_chunk_rows(c), hc], states[h].astype(BF16), _NT,
                                    preferred_element_type=F32)
            outs.append(inter + intra[c][h][0])
            states[h] = states[h] * f['dec'][c][:, hc] + intra[c][h][1]
        o_ref[0, _chunk_rows(c), :] = jnp.concatenate(outs, axis=1)
    for h in range(A_HEADS):
        st_ref[0, d, h] = states[h]


def _hgrn_kernel(qf_ref, zf_ref, vf_ref, qb_ref, zb_ref, vb_ref, lbl_ref, s0_ref,
                 of_ref, ob_ref, st_ref, *, layer, n_chunks):
    @pl.when(pl.program_id(1) == 0)
    def _():
        st_ref[...] = s0_ref[...]

    lb = _lower_bounds(lbl_ref[...], layer)
    dirs = ((qf_ref, zf_ref, vf_ref, of_ref, False), (qb_ref, zb_ref, vb_ref, ob_ref, True))
    facs = [_hgrn_factors(q[0], z[0], v[0], lb[d:d + 1], rev, n_chunks)
            for d, (q, z, v, _, rev) in enumerate(dirs)]
    scores = [_hgrn_scores(facs[d], dirs[d][4], n_chunks) for d in range(2)]
    intra = [_hgrn_intra(facs[d], scores[d], dirs[d][4], n_chunks) for d in range(2)]
    for d in range(2):
        _hgrn_scan(facs[d], intra[d], st_ref, d, dirs[d][3], dirs[d][4], n_chunks)


def _hgrn(pa, lb_logits, s0, layer, tile):
    bsz, t_len, _ = pa.shape
    nb = t_len // tile
    depth = lb_logits.shape[0]

    def col_spec(colblk, rev):
        if rev:
            return pl.BlockSpec((1, tile, A_WIDTH), lambda b, i: (b, nb - 1 - i, colblk))
        return pl.BlockSpec((1, tile, A_WIDTH), lambda b, i: (b, i, colblk))

    st_spec = pl.BlockSpec((1, 2, A_HEADS, A_DK, A_DK), lambda b, i: (b, 0, 0, 0, 0))
    o_sds = jax.ShapeDtypeStruct((bsz, t_len, A_WIDTH), F32)
    return pl.pallas_call(
        functools.partial(_hgrn_kernel, layer=layer, n_chunks=tile // A_CHUNK),
        grid=(bsz, nb),
        in_specs=[col_spec(0, False), col_spec(1, False), col_spec(3, False),
                  col_spec(0, True), col_spec(2, True), col_spec(3, True),
                  _full((depth, 2, A_WIDTH)), st_spec],
        out_specs=[col_spec(0, False), col_spec(0, True), st_spec],
        out_shape=[o_sds, o_sds, jax.ShapeDtypeStruct(s0.shape, F32)],
        compiler_params=_cparams(("arbitrary", "arbitrary")),
        name="hgrn",
    )(pa, pa, pa, pa, pa, pa, lb_logits, s0)


def _mixer_kernel(x_ref, of_ref, ob_ref, gate_ref, ybc_ref, mod_ref, og_ref, wo_ref, o_ref):
    o = of_ref[0] + ob_ref[0]
    parts = []
    for h in range(A_HEADS):
        oh = o[:, h * A_DK:(h + 1) * A_DK]
        parts.append(oh * lax.rsqrt(jnp.mean(oh * oh, axis=-1, keepdims=True) + EPS))
    gate = gate_ref[0]
    ya = jnp.concatenate(parts, axis=1) * og_ref[...] * (gate * jax.nn.sigmoid(gate))
    y = (jnp.dot(ya.astype(BF16), wo_ref[0:A_WIDTH, :], preferred_element_type=F32)
         + jnp.dot(ybc_ref[0], wo_ref[A_WIDTH:, :], preferred_element_type=F32))
    o_ref[0] = x_ref[0] + mod_ref[0, 2:3, :] * y


def _mixer(x, o_f, o_b, gate, ybc, mod, lw, tile):
    bsz, t_len, d = x.shape

    def tok(cols):
        return pl.BlockSpec((1, tile, cols), lambda b, i: (b, i, 0))

    return pl.pallas_call(
        _mixer_kernel,
        grid=(bsz, t_len // tile),
        in_specs=[tok(d), tok(A_WIDTH), tok(A_WIDTH), tok(A_WIDTH), tok(B_WIDTH + C_WIDTH),
                  pl.BlockSpec((1, 6, d), lambda b, i: (b, 0, 0)),
                  _full(lw['onorm_g'].shape), _full(lw['w_out'].shape)],
        out_specs=tok(d),
        out_shape=jax.ShapeDtypeStruct(x.shape, F32),
        compiler_params=_cparams(("arbitrary", "arbitrary")),
        name="mixer",
    )(x, o_f, o_b, gate, ybc, mod, lw['onorm_g'], lw['w_out'])


def _ffn_kernel(x_ref, mod_ref, g_ref, w13_ref, w2_ref, fg_ref, o_ref, *, d_ff, chunk, final_norm):
    x = x_ref[0]
    h = _norm_modulate(x, g_ref[...], mod_ref[0, 3:4, :], mod_ref[0, 4:5, :]).astype(BF16)
    acc = jnp.zeros(x.shape, F32)
    for c in range(d_ff // chunk):
        a = jnp.dot(h, w13_ref[:, c * chunk:(c + 1) * chunk], preferred_element_type=F32)
        b = jnp.dot(h, w13_ref[:, d_ff + c * chunk:d_ff + (c + 1) * chunk], preferred_element_type=F32)
        act = (a * jax.nn.sigmoid(a) * b).astype(BF16)
        acc = acc + jnp.dot(act, w2_ref[c * chunk:(c + 1) * chunk, :], preferred_element_type=F32)
    y = x + mod_ref[0, 5:6, :] * acc
    if final_norm:
        y = y * lax.rsqrt(jnp.mean(y * y, axis=-1, keepdims=True) + EPS) * fg_ref[...]
    o_ref[0] = y


def _ffn(x, mod, norm_g, w13, w2, final_g, tile, final_norm):
    bsz, t_len, d = x.shape
    d_ff = w2.shape[0]
    return pl.pallas_call(
        functools.partial(_ffn_kernel, d_ff=d_ff, chunk=256, final_norm=final_norm),
        grid=(bsz, t_len // tile),
        in_specs=[
            pl.BlockSpec((1, tile, d), lambda b, i: (b, i, 0)),
            pl.BlockSpec((1, 6, d), lambda b, i: (b, 0, 0)),
            _full((1, d)), _full(w13.shape), _full(w2.shape), _full((1, d)),
        ],
        out_specs=pl.BlockSpec((1, tile, d), lambda b, i: (b, i, 0)),
        out_shape=jax.ShapeDtypeStruct(x.shape, F32),
        compiler_params=_cparams(("arbitrary", "arbitrary")),
        name="ffn",
    )(x, mod, norm_g.reshape(1, d), w13, w2, final_g.reshape(1, d))


def _layer_params(l, gavg, norm1_g, w_in, hgrn_onorm_g, gmlp_ln_g, gmlp_ln_b, gmlp_w_s, gmlp_b_s,
                  conv_dw_w, conv_dw_b, conv_ln_g, conv_ln_b, conv_pw_w, conv_pw_b, w_out,
                  norm2_g, ffn_w13, ffn_w2):
    row = lambda a: a.reshape(1, -1).astype(F32)
    return {
        'norm1_g': row(norm1_g[l]), 'w_in': w_in[l].astype(BF16),
        'onorm_g': row(hgrn_onorm_g[l]), 'gavg': gavg,
        'gmlp_ln_g': row(gmlp_ln_g[l]), 'gmlp_ln_b': row(gmlp_ln_b[l]),
        'ws_cat': gmlp_w_s[l].reshape(B_GROUPS * B_CHUNK, B_CHUNK).astype(BF16),
        'bs_tile': jnp.repeat(gmlp_b_s[l].T, B_WIDTH // B_GROUPS, axis=1).astype(F32),
        'dw_w': conv_dw_w[l].astype(F32), 'dw_b': row(conv_dw_b[l]),
        'conv_ln_g': row(conv_ln_g[l]), 'conv_ln_b': row(conv_ln_b[l]),
        'pw_w': conv_pw_w[l].astype(BF16), 'pw_b': row(conv_pw_b[l]),
        'w_out': w_out[l].astype(BF16),
        'norm2_g': norm2_g[l], 'ffn_w13': ffn_w13[l].astype(BF16), 'ffn_w2': ffn_w2[l].astype(BF16),
    }


def kernel(x, c, ctx, c_ctx, ada_w, ada_b, norm1_g, w_in, hgrn_lb_logits, hgrn_onorm_g, gmlp_ln_g, gmlp_ln_b, gmlp_w_s, gmlp_b_s, conv_dw_w, conv_dw_b, conv_ln_g, conv_ln_b, conv_pw_w, conv_pw_b, w_out, norm2_g, ffn_w13, ffn_w2, final_norm_g):
    bsz, t_len, d = x.shape
    ctx_len = ctx.shape[1]
    depth = ada_w.shape[0]
    lb_logits = hgrn_lb_logits.astype(F32)

    cs = jnp.concatenate([c, c_ctx[None, :], jnp.zeros((MOD_ROWS - bsz - 1, d), F32)], axis=0)
    mod_all = _modulation(cs, ada_w, ada_b)
    grp = jnp.arange(B_WIDTH) // (B_WIDTH // B_GROUPS)
    gavg = ((grp[:, None] == grp[None, :]).astype(F32) / (B_WIDTH // B_GROUPS)).astype(BF16)
    zero_state = jnp.zeros((bsz, 2, A_HEADS, A_DK, A_DK), F32)

    for l in range(depth):
        last = l == depth - 1
        lw = _layer_params(l, gavg, norm1_g, w_in, hgrn_onorm_g, gmlp_ln_g, gmlp_ln_b, gmlp_w_s,
                           gmlp_b_s, conv_dw_w, conv_dw_b, conv_ln_g, conv_ln_b, conv_pw_w,
                           conv_pw_b, w_out, norm2_g, ffn_w13, ffn_w2)
        mod = mod_all[l, :bsz].reshape(bsz, 6, d)
        mod_ctx = jnp.broadcast_to(mod_all[l, bsz].reshape(1, 6, d), (bsz, 6, d))
        if last:
            pa_c = _in_proj(ctx, mod_ctx, lw['norm1_g'], lw['w_in'][:, :PA_COLS], ctx_len)
            _, _, state = _hgrn(pa_c, lb_logits, zero_state, l, ctx_len)
        else:
            pa_c, gate_c, ybc_c = _in_proj_mix(ctx, mod_ctx, lw, ctx_len, ctx_len)
            of_c, ob_c, state = _hgrn(pa_c, lb_logits, zero_state, l, ctx_len)
            ctx = _mixer(ctx, of_c, ob_c, gate_c, ybc_c, mod_ctx, lw, ctx_len)
            ctx = _ffn(ctx, mod_ctx, lw['norm2_g'], lw['ffn_w13'], lw['ffn_w2'], final_norm_g,
                       ctx_len, False)
        pa, gate, ybc = _in_proj_mix(x, mod, lw, 512, t_len // (t_len // GRID_W))
        o_f, o_b, _ = _hgrn(pa, lb_logits, state, l, 512)
        x = _mixer(x, o_f, o_b, gate, ybc, mod, lw, 512)
        x = _ffn(x, mod, lw['norm2_g'], lw['ffn_w13'], lw['ffn_w2'], final_norm_g, 512, last)
    return x
```

```python
import functools

import jax
import jax.numpy as jnp
from jax import lax
from jax.experimental import pallas as pl
from jax.experimental.pallas import tpu as pltpu

F32 = jnp.float32
BF16 = jnp.bfloat16

D_MODEL = 1024
GRID_W = 64
A_HEADS = 4
A_DK = 128
A_WIDTH = A_HEADS * A_DK
A_CHUNK = 64
A_SUB = 32
B_GROUPS = 4
B_WIDTH = 256
B_CHUNK = 128
C_GROUPS = 4
C_WIDTH = 256
C_KERNEL = 31
C_PAD = 16
SUBLANES = 8
PROJ_CHUNK = 256
FFN_CHUNK = 256
PA_COLS = 4 * A_WIDTH
PM_COLS = A_WIDTH + 2 * B_WIDTH + 2 * C_WIDTH
EPS = 1e-6
EXP2_CLAMP = 80.0
LOG2E = 1.4426950408889634
MOD_ROWS = 8

V7X_VMEM_LIMIT = 56 * 1024 * 1024

_NT = (((1,), (1,)), ((), ()))
_TN = (((0,), (0,)), ((), ()))


def _cparams(sem, flags=None):
    return pltpu.CompilerParams(dimension_semantics=sem, vmem_limit_bytes=V7X_VMEM_LIMIT, flags=flags)


def _full(shape):
    n = len(shape)
    return pl.BlockSpec(shape, lambda *_: (0,) * n)


def _mod_kernel(cs_ref, w_ref, b_ref, o_ref):
    s = cs_ref[...]
    s = s * jax.nn.sigmoid(s)
    o_ref[0] = jnp.dot(s, w_ref[0], preferred_element_type=F32,
                       precision=lax.Precision.HIGHEST) + b_ref[0]


def _modulation(cs, ada_w, ada_b):
    depth, d, n = ada_w.shape
    tn = 1024
    return pl.pallas_call(
        _mod_kernel,
        grid=(depth, n // tn),
        in_specs=[
            pl.BlockSpec((MOD_ROWS, d), lambda l, j: (0, 0)),
            pl.BlockSpec((1, d, tn), lambda l, j: (l, 0, j)),
            pl.BlockSpec((1, 1, tn), lambda l, j: (l, 0, j)),
        ],
        out_specs=pl.BlockSpec((1, MOD_ROWS, tn), lambda l, j: (l, 0, j)),
        out_shape=jax.ShapeDtypeStruct((depth, MOD_ROWS, n), F32),
        compiler_params=_cparams(("arbitrary", "arbitrary")),
        name="modulation",
    )(cs, ada_w, ada_b.reshape(depth, 1, n))


def _norm_modulate(x, g, shift, scale):
    ms = jnp.mean(x * x, axis=-1, keepdims=True)
    y = x * lax.rsqrt(ms + EPS) * g
    return y * (1.0 + scale) + shift


def _group_mean(x, gavg):
    hi = x.astype(BF16)
    lo = (x - hi.astype(F32)).astype(BF16)
    return (jnp.dot(hi, gavg, preferred_element_type=F32)
            + jnp.dot(lo, gavg, preferred_element_type=F32))


def _group_layer_norm(x, gavg, g, b):
    mu = _group_mean(x, gavg)
    xc = x - mu
    var = _group_mean(xc * xc, gavg)
    return xc * lax.rsqrt(var + EPS) * g + b


def _gmlp_branch(u, v, gavg, ln_g, ln_b, ws_ref, bs_ref):
    tile = u.shape[0]
    vn = _group_layer_norm(jax.nn.gelu(v), gavg, ln_g, ln_b).astype(BF16)
    lane_group = lax.broadcasted_iota(jnp.int32, (B_CHUNK, B_WIDTH), 1) // (B_WIDTH // B_GROUPS)
    mixed = []
    for n in range(tile // B_CHUNK):
        r = jnp.dot(ws_ref[...], vn[n * B_CHUNK:(n + 1) * B_CHUNK], preferred_element_type=F32)
        m = bs_ref[...]
        for g in range(B_GROUPS):
            m = m + jnp.where(lane_group == g, r[g * B_CHUNK:(g + 1) * B_CHUNK], 0.0)
        mixed.append(m)
    return jax.nn.gelu(u) * jnp.concatenate(mixed, axis=0)


def _conv_branch(ca, cgate, gavg, dww_ref, dw_b, ln_g, ln_b, pww_ref, pw_b, zpad_ref, zsh_ref, seg,
                 between):
    tile = ca.shape[0]
    z = ca * jax.nn.sigmoid(cgate)
    nseg = tile // seg
    pad_zeros = jnp.zeros((C_PAD, C_WIDTH), F32)
    for s in range(nseg):
        zpad_ref[s, 0:C_PAD, :] = pad_zeros
        zpad_ref[s, C_PAD + seg:2 * C_PAD + seg, :] = pad_zeros
        zpad_ref[s, C_PAD:C_PAD + seg, :] = z[s * seg:(s + 1) * seg]
    span = seg + 2 * C_PAD - SUBLANES
    convs = []
    for s in range(nseg):
        for r in range(1, SUBLANES):
            zsh_ref[s, r] = zpad_ref[s, r:r + span, :]
        acc = jnp.zeros((seg, C_WIDTH), F32)
        for j in range(C_KERNEL):
            start = C_PAD - C_KERNEL // 2 + j
            r, a = start % SUBLANES, start - start % SUBLANES
            tap = zpad_ref[s, a:a + seg, :] if r == 0 else zsh_ref[s, r, a:a + seg, :]
            acc = acc + dww_ref[j:j + 1, :] * tap
        convs.append(acc)
        between()
    zc = jnp.concatenate(convs, axis=0) + dw_b
    zc = _group_layer_norm(zc, gavg, ln_g, ln_b)
    zc = (zc * jax.nn.sigmoid(zc)).astype(BF16)
    return jnp.dot(zc, pww_ref[...], preferred_element_type=F32) + pw_b


def _in_proj_kernel(x_ref, mod_ref, g_ref, w_ref, pa_ref):
    h = _norm_modulate(x_ref[0], g_ref[...], mod_ref[0, 0:1, :], mod_ref[0, 1:2, :]).astype(BF16)
    pa_ref[0] = jnp.dot(h, w_ref[...], preferred_element_type=F32)


def _in_proj_mix_kernel(x_ref, mod_ref, g_ref, w_ref, gavg_ref, bg_ref, bb_ref, ws_ref, bs_ref,
                        dww_ref, dwb_ref, cg_ref, cb_ref, pww_ref, pwb_ref,
                        pa_ref, gate_ref, ybc_ref, zpad_ref, zsh_ref, *, seg):
    h = _norm_modulate(x_ref[0], g_ref[...], mod_ref[0, 0:1, :], mod_ref[0, 1:2, :]).astype(BF16)
    gavg = gavg_ref[...]

    def proj(lo, n):
        return jnp.dot(h, w_ref[:, lo:lo + n], preferred_element_type=F32)

    chunks = [(pa_ref, lo) for lo in range(0, PA_COLS, PROJ_CHUNK)]
    chunks += [(gate_ref, lo) for lo in range(0, A_WIDTH, PROJ_CHUNK)]
    pending = iter(chunks)

    def project_next():
        nxt = next(pending, None)
        if nxt is not None:
            ref, lo = nxt
            w_lo = lo if ref is pa_ref else PA_COLS + lo
            ref[0, :, lo:lo + PROJ_CHUNK] = proj(w_lo, PROJ_CHUNK)

    c0 = PA_COLS + A_WIDTH + 2 * B_WIDTH
    yc = _conv_branch(proj(c0, C_WIDTH), proj(c0 + C_WIDTH, C_WIDTH), gavg, dww_ref, dwb_ref[...],
                      cg_ref[...], cb_ref[...], pww_ref, pwb_ref[...], zpad_ref, zsh_ref, seg,
                      project_next)
    project_next()
    b0 = PA_COLS + A_WIDTH
    yb = _gmlp_branch(proj(b0, B_WIDTH), proj(b0 + B_WIDTH, B_WIDTH), gavg, bg_ref[...], bb_ref[...],
                      ws_ref, bs_ref)
    ybc_ref[0] = jnp.concatenate([yb, yc], axis=1).astype(BF16)
    for _ in chunks:
        project_next()


def _in_proj(x, mod, norm_g, w_in, tile):
    bsz, t_len, d = x.shape
    cols = w_in.shape[1]
    return pl.pallas_call(
        _in_proj_kernel,
        grid=(bsz, t_len // tile),
        in_specs=[
            pl.BlockSpec((1, tile, d), lambda b, i: (b, i, 0)),
            pl.BlockSpec((1, 6, d), lambda b, i: (b, 0, 0)),
            _full((1, d)),
            _full((d, cols)),
        ],
        out_specs=pl.BlockSpec((1, tile, cols), lambda b, i: (b, i, 0)),
        out_shape=jax.ShapeDtypeStruct((bsz, t_len, cols), F32),
        compiler_params=_cparams(("arbitrary", "arbitrary")),
        name="in_proj",
    )(x, mod, norm_g.reshape(1, d), w_in)


def _in_proj_mix(x, mod, lw, tile, seg):
    bsz, t_len, d = x.shape
    nseg = tile // seg

    def tok(cols):
        return pl.BlockSpec((1, tile, cols), lambda b, i: (b, i, 0))

    params = [lw['norm1_g'], lw['w_in'], lw['gavg'], lw['gmlp_ln_g'], lw['gmlp_ln_b'], lw['ws_cat'],
              lw['bs_tile'], lw['dw_w'], lw['dw_b'], lw['conv_ln_g'], lw['conv_ln_b'], lw['pw_w'], lw['pw_b']]
    return pl.pallas_call(
        functools.partial(_in_proj_mix_kernel, seg=seg),
        grid=(bsz, t_len // tile),
        in_specs=[tok(d), pl.BlockSpec((1, 6, d), lambda b, i: (b, 0, 0))] + [_full(p.shape) for p in params],
        out_specs=[tok(PA_COLS), tok(A_WIDTH), tok(B_WIDTH + C_WIDTH)],
        out_shape=[jax.ShapeDtypeStruct((bsz, t_len, PA_COLS), F32),
                   jax.ShapeDtypeStruct((bsz, t_len, A_WIDTH), F32),
                   jax.ShapeDtypeStruct((bsz, t_len, B_WIDTH + C_WIDTH), BF16)],
        scratch_shapes=[pltpu.VMEM((nseg, seg + 2 * C_PAD, C_WIDTH), F32),
                        pltpu.VMEM((nseg, SUBLANES, seg + 2 * C_PAD - SUBLANES, C_WIDTH), F32)],
        compiler_params=_cparams(("arbitrary", "arbitrary")),
        name="in_proj_mix",
    )(x, mod, *params)


def _lower_bounds(logits, layer):
    depth = logits.shape[0]
    rows = [logits[j] for j in range(depth)]
    m = rows[0]
    for r in rows[1:]:
        m = jnp.maximum(m, r)
    es = [jnp.exp(r - m) for r in rows]
    tot = es[0]
    for e in es[1:]:
        tot = tot + e
    lb = jnp.zeros_like(m)
    for j in range(1, layer + 1):
        lb = lb + es[j] / tot
    return lb


def _chunk_rows(c, lo=0, n=A_CHUNK):
    return slice(c * A_CHUNK + lo, c * A_CHUNK + lo + n)


def _hgrn_factors(q, z, v, lb, rev, n_chunks):
    width = q.shape[1]
    sub = A_SUB
    row = lax.broadcasted_iota(jnp.int32, (A_CHUNK, A_CHUNK), 0)
    col = lax.broadcasted_iota(jnp.int32, (A_CHUNK, A_CHUNK), 1)
    tri = ((col >= row) if rev else (col <= row)).astype(F32).astype(BF16)
    oml = 1.0 - lb
    sg = jax.nn.sigmoid(z)
    lf = jnp.log(lb + oml * sg)
    kk = oml * (1.0 - sg)
    hi = lf.astype(BF16)
    lo = (lf - hi.astype(F32)).astype(BF16)
    bs = [(jnp.dot(tri, hi[_chunk_rows(c)], preferred_element_type=F32)
           + jnp.dot(tri, lo[_chunk_rows(c)], preferred_element_type=F32)) * LOG2E
          for c in range(n_chunks)]
    wide, narrow = (0, 1) if rev else (1, 0)
    mid_row = sub // 2 if rev else sub // 2 - 1
    last_row = 0 if rev else A_CHUNK - 1
    b_last, b_own, b_wide, b_narrow, decs = [], [], [], [], []
    for bc in bs:
        mids = [bc[i * sub + mid_row:i * sub + mid_row + 1] for i in range(2)]
        b_last.append(jnp.broadcast_to(bc[last_row:last_row + 1], (A_CHUNK, width)))
        b_own += [jnp.broadcast_to(m, (sub, width)) for m in mids]
        b_wide.append(jnp.broadcast_to(mids[wide], (A_CHUNK, width)))
        b_narrow.append(mids[narrow] - bc[narrow * sub:(narrow + 1) * sub])
        decs.append(jnp.exp2(bc[last_row:last_row + 1]))
    b = jnp.concatenate(bs, axis=0)
    kk_narrow = jnp.concatenate([kk[_chunk_rows(c, narrow * sub, sub)] for c in range(n_chunks)], axis=0)
    return {
        'qe': (q * jnp.exp2(b)).astype(BF16),
        'kd': (kk * jnp.exp2(jnp.concatenate(b_last, axis=0) - b)).astype(BF16),
        'qd': (q * jnp.exp2(jnp.minimum(b - jnp.concatenate(b_own, axis=0), EXP2_CLAMP))).astype(BF16),
        'k_wide': (kk * jnp.exp2(jnp.minimum(jnp.concatenate(b_wide, axis=0) - b, EXP2_CLAMP))).astype(BF16),
        'k_narrow': (kk_narrow * jnp.exp2(jnp.minimum(jnp.concatenate(b_narrow, axis=0), EXP2_CLAMP))).astype(BF16),
        'v': v.astype(BF16),
        'dec': decs,
    }


def _hgrn_scores(f, rev, n_chunks):
    sub = A_SUB
    wide, narrow = (0, 1) if rev else (1, 0)
    row_n = lax.broadcasted_iota(jnp.int32, (sub, sub), 0)
    col_n = lax.broadcasted_iota(jnp.int32, (sub, sub), 1)
    row_w = lax.broadcasted_iota(jnp.int32, (sub, A_CHUNK), 0)
    col_w = lax.broadcasted_iota(jnp.int32, (sub, A_CHUNK), 1)
    mask_n = (col_n >= row_n) if rev else (col_n <= row_n)
    mask_w = (col_w >= row_w) if rev else (col_w <= row_w + sub)
    out = []
    for c in range(n_chunks):
        per_head = []
        for h in range(A_HEADS):
            hc = slice(h * A_DK, (h + 1) * A_DK)
            s_n = lax.dot_general(f['qd'][_chunk_rows(c, narrow * sub, sub), hc],
                                  f['k_narrow'][c * sub:(c + 1) * sub, hc], _NT, preferred_element_type=F32)
            s_w = lax.dot_general(f['qd'][_chunk_rows(c, wide * sub, sub), hc],
                                  f['k_wide'][_chunk_rows(c), hc], _NT, preferred_element_type=F32)
            per_head.append((jnp.where(mask_n, s_n, 0.0).astype(BF16),
                             jnp.where(mask_w, s_w, 0.0).astype(BF16)))
        out.append(per_head)
    return out


def _hgrn_intra(f, scores, rev, n_chunks):
    sub = A_SUB
    wide, narrow = (0, 1) if rev else (1, 0)
    out = []
    for c in range(n_chunks):
        per_head = []
        for h in range(A_HEADS):
            hc = slice(h * A_DK, (h + 1) * A_DK)
            s_n, s_w = scores[c][h]
            o_n = jnp.dot(s_n, f['v'][_chunk_rows(c, narrow * sub, sub), hc], preferred_element_type=F32)
            o_w = jnp.dot(s_w, f['v'][_chunk_rows(c), hc], preferred_element_type=F32)
            intra = jnp.concatenate([o_w, o_n] if rev else [o_n, o_w], axis=0)
            upd = lax.dot_general(f['v'][_chunk_rows(c), hc], f['kd'][_chunk_rows(c), hc], _TN,
                                  preferred_element_type=F32)
            per_head.append((intra, upd))
        out.append(per_head)
    return out


def _hgrn_scan(f, intra, st_ref, d, o_ref, rev, n_chunks):
    states = [st_ref[0, d, h] for h in range(A_HEADS)]
    for c in (range(n_chunks - 1, -1, -1) if rev else range(n_chunks)):
        outs = []
        for h in range(A_HEADS):
            hc = slice(h * A_DK, (h + 1) * A_DK)
            inter = lax.dot_general(f['qe'][_chunk_rows(c), hc], states[h].astype(BF16), _NT,
                                    preferred_element_type=F32)
            outs.append(inter + intra[c][h][0])
            states[h] = states[h] * f['dec'][c][:, hc] + intra[c][h][1]
        o_ref[0, _chunk_rows(c), :] = jnp.concatenate(outs, axis=1)
    for h in range(A_HEADS):
        st_ref[0, d, h] = states[h]


def _hgrn_kernel(qf_ref, zf_ref, vf_ref, qb_ref, zb_ref, vb_ref, lbl_ref, s0_ref,
                 of_ref, ob_ref, st_ref, *, layer, n_chunks):
    @pl.when(pl.program_id(1) == 0)
    def _():
        st_ref[...] = s0_ref[...]

    lb = _lower_bounds(lbl_ref[...], layer)
    dirs = ((qf_ref, zf_ref, vf_ref, of_ref, False), (qb_ref, zb_ref, vb_ref, ob_ref, True))
    facs = [_hgrn_factors(q[0], z[0], v[0], lb[d:d + 1], rev, n_chunks)
            for d, (q, z, v, _, rev) in enumerate(dirs)]
    scores = [_hgrn_scores(facs[d], dirs[d][4], n_chunks) for d in range(2)]
    intra = [_hgrn_intra(facs[d], scores[d], dirs[d][4], n_chunks) for d in range(2)]
    for d in range(2):
        _hgrn_scan(facs[d], intra[d], st_ref, d, dirs[d][3], dirs[d][4], n_chunks)


def _hgrn(pa, lb_logits, s0, layer, tile):
    bsz, t_len, _ = pa.shape
    nb = t_len // tile
    depth = lb_logits.shape[0]

    def col_spec(colblk, rev):
        if rev:
            return pl.BlockSpec((1, tile, A_WIDTH), lambda b, i: (b, nb - 1 - i, colblk))
        return pl.BlockSpec((1, tile, A_WIDTH), lambda b, i: (b, i, colblk))

    st_spec = pl.BlockSpec((1, 2, A_HEADS, A_DK, A_DK), lambda b, i: (b, 0, 0, 0, 0))
    o_sds = jax.ShapeDtypeStruct((bsz, t_len, A_WIDTH), F32)
    return pl.pallas_call(
        functools.partial(_hgrn_kernel, layer=layer, n_chunks=tile // A_CHUNK),
        grid=(bsz, nb),
        in_specs=[col_spec(0, False), col_spec(1, False), col_spec(3, False),
                  col_spec(0, True), col_spec(2, True), col_spec(3, True),
                  _full((depth, 2, A_WIDTH)), st_spec],
        out_specs=[col_spec(0, False), col_spec(0, True), st_spec],
        out_shape=[o_sds, o_sds, jax.ShapeDtypeStruct(s0.shape, F32)],
        compiler_params=_cparams(("arbitrary", "arbitrary")),
        name="hgrn",
    )(pa, pa, pa, pa, pa, pa, lb_logits, s0)


def _out_ffn_kernel(x_ref, of_ref, ob_ref, gate_ref, ybc_ref, mod_ref, og_ref, wo_ref,
                    g_ref, w13_ref, w2_ref, fg_ref, o_ref, *, d_ff, chunk, final_norm):
    o = of_ref[0] + ob_ref[0]
    parts = []
    for h in range(A_HEADS):
        oh = o[:, h * A_DK:(h + 1) * A_DK]
        parts.append(oh * lax.rsqrt(jnp.mean(oh * oh, axis=-1, keepdims=True) + EPS))
    gate = gate_ref[0]
    ya = jnp.concatenate(parts, axis=1) * og_ref[...] * (gate * jax.nn.sigmoid(gate))
    y = (jnp.dot(ya.astype(BF16), wo_ref[0:A_WIDTH, :], preferred_element_type=F32)
         + jnp.dot(ybc_ref[0], wo_ref[A_WIDTH:, :], preferred_element_type=F32))
    x = x_ref[0] + mod_ref[0, 2:3, :] * y
    h = _norm_modulate(x, g_ref[...], mod_ref[0, 3:4, :], mod_ref[0, 4:5, :]).astype(BF16)
    acc = jnp.zeros(x.shape, F32)
    for c in range(d_ff // chunk):
        a = jnp.dot(h, w13_ref[:, c * chunk:(c + 1) * chunk], preferred_element_type=F32)
        b = jnp.dot(h, w13_ref[:, d_ff + c * chunk:d_ff + (c + 1) * chunk], preferred_element_type=F32)
        act = (a * jax.nn.sigmoid(a) * b).astype(BF16)
        acc = acc + jnp.dot(act, w2_ref[c * chunk:(c + 1) * chunk, :], preferred_element_type=F32)
    y = x + mod_ref[0, 5:6, :] * acc
    if final_norm:
        y = y * lax.rsqrt(jnp.mean(y * y, axis=-1, keepdims=True) + EPS) * fg_ref[...]
    o_ref[0] = y


def _out_ffn(x, o_f, o_b, gate, ybc, mod, lw, final_g, tile, final_norm):
    bsz, t_len, d = x.shape
    d_ff = lw['ffn_w2'].shape[0]

    def tok(cols):
        return pl.BlockSpec((1, tile, cols), lambda b, i: (b, i, 0))

    params = [lw['onorm_g'], lw['w_out'], lw['norm2_g'], lw['ffn_w13'], lw['ffn_w2'], final_g.reshape(1, d)]
    return pl.pallas_call(
        functools.partial(_out_ffn_kernel, d_ff=d_ff, chunk=FFN_CHUNK, final_norm=final_norm),
        grid=(bsz, t_len // tile),
        in_specs=[tok(d), tok(A_WIDTH), tok(A_WIDTH), tok(A_WIDTH), tok(B_WIDTH + C_WIDTH),
                  pl.BlockSpec((1, 6, d), lambda b, i: (b, 0, 0))] + [_full(p.shape) for p in params],
        out_specs=tok(d),
        out_shape=jax.ShapeDtypeStruct(x.shape, F32),
        compiler_params=_cparams(("arbitrary", "arbitrary")),
        name="out_ffn",
    )(x, o_f, o_b, gate, ybc, mod, *params)


def _layer_params(l, gavg, norm1_g, w_in, hgrn_onorm_g, gmlp_ln_g, gmlp_ln_b, gmlp_w_s, gmlp_b_s,
                  conv_dw_w, conv_dw_b, conv_ln_g, conv_ln_b, conv_pw_w, conv_pw_b, w_out,
                  norm2_g, ffn_w13, ffn_w2):
    row = lambda a: a.reshape(1, -1).astype(F32)
    return {
        'norm1_g': row(norm1_g[l]), 'w_in': w_in[l].astype(BF16),
        'onorm_g': row(hgrn_onorm_g[l]), 'gavg': gavg,
        'gmlp_ln_g': row(gmlp_ln_g[l]), 'gmlp_ln_b': row(gmlp_ln_b[l]),
        'ws_cat': gmlp_w_s[l].reshape(B_GROUPS * B_CHUNK, B_CHUNK).astype(BF16),
        'bs_tile': jnp.repeat(gmlp_b_s[l].T, B_WIDTH // B_GROUPS, axis=1).astype(F32),
        'dw_w': conv_dw_w[l].astype(F32), 'dw_b': row(conv_dw_b[l]),
        'conv_ln_g': row(conv_ln_g[l]), 'conv_ln_b': row(conv_ln_b[l]),
        'pw_w': conv_pw_w[l].astype(BF16), 'pw_b': row(conv_pw_b[l]),
        'w_out': w_out[l].astype(BF16),
        'norm2_g': row(norm2_g[l]), 'ffn_w13': ffn_w13[l].astype(BF16), 'ffn_w2': ffn_w2[l].astype(BF16),
    }


def kernel(x, c, ctx, c_ctx, ada_w, ada_b, norm1_g, w_in, hgrn_lb_logits, hgrn_onorm_g, gmlp_ln_g, gmlp_ln_b, gmlp_w_s, gmlp_b_s, conv_dw_w, conv_dw_b, conv_ln_g, conv_ln_b, conv_pw_w, conv_pw_b, w_out, norm2_g, ffn_w13, ffn_w2, final_norm_g):
    bsz, t_len, d = x.shape
    ctx_len = ctx.shape[1]
    depth = ada_w.shape[0]
    lb_logits = hgrn_lb_logits.astype(F32)

    cs = jnp.concatenate([c, c_ctx[None, :], jnp.zeros((MOD_ROWS - bsz - 1, d), F32)], axis=0)
    mod_all = _modulation(cs, ada_w, ada_b)
    grp = jnp.arange(B_WIDTH) // (B_WIDTH // B_GROUPS)
    gavg = ((grp[:, None] == grp[None, :]).astype(F32) / (B_WIDTH // B_GROUPS)).astype(BF16)
    zero_state = jnp.zeros((bsz, 2, A_HEADS, A_DK, A_DK), F32)

    for l in range(depth):
        last = l == depth - 1
        lw = _layer_params(l, gavg, norm1_g, w_in, hgrn_onorm_g, gmlp_ln_g, gmlp_ln_b, gmlp_w_s,
                           gmlp_b_s, conv_dw_w, conv_dw_b, conv_ln_g, conv_ln_b, conv_pw_w,
                           conv_pw_b, w_out, norm2_g, ffn_w13, ffn_w2)
        mod = mod_all[l, :bsz].reshape(bsz, 6, d)
        mod_ctx = jnp.broadcast_to(mod_all[l, bsz].reshape(1, 6, d), (bsz, 6, d))
        if last:
            pa_c = _in_proj(ctx, mod_ctx, lw['norm1_g'], lw['w_in'][:, :PA_COLS], ctx_len)
            _, _, state = _hgrn(pa_c, lb_logits, zero_state, l, ctx_len)
        else:
            pa_c, gate_c, ybc_c = _in_proj_mix(ctx, mod_ctx, lw, ctx_len, ctx_len)
            of_c, ob_c, state = _hgrn(pa_c, lb_logits, zero_state, l, ctx_len)
            ctx = _out_ffn(ctx, of_c, ob_c, gate_c, ybc_c, mod_ctx, lw, final_norm_g, ctx_len, False)
        pa, gate, ybc = _in_proj_mix(x, mod, lw, 512, t_len // (t_len // GRID_W))
        o_f, o_b, _ = _hgrn(pa, lb_logits, state, l, 512)
        x = _out_ffn(x, o_f, o_b, gate, ybc, mod, lw, final_norm_g, 512, last)
    return x
```

```python
import functools

import jax
import jax.numpy as jnp
from jax import lax
from jax.experimental import pallas as pl
from jax.experimental.pallas import tpu as pltpu

F32 = jnp.float32
BF16 = jnp.bfloat16

D_MODEL = 1024
GRID_W = 64
A_HEADS = 4
A_DK = 128
A_WIDTH = A_HEADS * A_DK
A_CHUNK = 64
A_SUB = 32
B_GROUPS = 4
B_WIDTH = 256
B_CHUNK = 128
C_GROUPS = 4
C_WIDTH = 256
C_KERNEL = 31
C_PAD = 16
SUBLANES = 8
PROJ_CHUNK = 256
FFN_CHUNK = 256
PA_COLS = 4 * A_WIDTH
PM_COLS = A_WIDTH + 2 * B_WIDTH + 2 * C_WIDTH
EPS = 1e-6
EXP2_CLAMP = 80.0
LOG2E = 1.4426950408889634
MOD_ROWS = 8

V7X_VMEM_LIMIT = 56 * 1024 * 1024

_NT = (((1,), (1,)), ((), ()))
_TN = (((0,), (0,)), ((), ()))


def _cparams(sem, flags=None):
    return pltpu.CompilerParams(dimension_semantics=sem, vmem_limit_bytes=V7X_VMEM_LIMIT, flags=flags)


def _full(shape):
    n = len(shape)
    return pl.BlockSpec(shape, lambda *_: (0,) * n)


def _mod_kernel(cs_ref, w_ref, b_ref, o_ref):
    s = cs_ref[...]
    s = s * jax.nn.sigmoid(s)
    o_ref[0] = jnp.dot(s, w_ref[0], preferred_element_type=F32,
                       precision=lax.Precision.HIGHEST) + b_ref[0]


def _modulation(cs, ada_w, ada_b):
    depth, d, n = ada_w.shape
    tn = 1024
    return pl.pallas_call(
        _mod_kernel,
        grid=(depth, n // tn),
        in_specs=[
            pl.BlockSpec((MOD_ROWS, d), lambda l, j: (0, 0)),
            pl.BlockSpec((1, d, tn), lambda l, j: (l, 0, j)),
            pl.BlockSpec((1, 1, tn), lambda l, j: (l, 0, j)),
        ],
        out_specs=pl.BlockSpec((1, MOD_ROWS, tn), lambda l, j: (l, 0, j)),
        out_shape=jax.ShapeDtypeStruct((depth, MOD_ROWS, n), F32),
        compiler_params=_cparams(("arbitrary", "arbitrary")),
        name="modulation",
    )(cs, ada_w, ada_b.reshape(depth, 1, n))


def _norm_modulate(x, g, shift, scale):
    ms = jnp.mean(x * x, axis=-1, keepdims=True)
    y = x * lax.rsqrt(ms + EPS) * g
    return y * (1.0 + scale) + shift


def _group_mean(x, gavg):
    return jnp.dot(x.astype(BF16), gavg, preferred_element_type=F32)


def _group_layer_norm_stages(x, gavg, g, b, out):
    mu = _group_mean(x, gavg)
    yield
    xc = x - mu
    var = _group_mean(xc * xc, gavg)
    yield
    out['ln'] = xc * lax.rsqrt(var + EPS) * g + b


def _gmlp_stages(project, gavg, ln_g, ln_b, ws_ref, bs_ref, out):
    u = project(0)
    v = project(B_WIDTH)
    tile = u.shape[0]
    yield
    ln = {}
    yield from _group_layer_norm_stages(jax.nn.gelu(v), gavg, ln_g, ln_b, ln)
    vn = ln['ln'].astype(BF16)
    gu = jax.nn.gelu(u)
    lane_group = lax.broadcasted_iota(jnp.int32, (B_CHUNK, B_WIDTH), 1) // (B_WIDTH // B_GROUPS)
    mixed = []
    for n in range(tile // B_CHUNK):
        yield
        r = jnp.dot(ws_ref[...], vn[n * B_CHUNK:(n + 1) * B_CHUNK], preferred_element_type=F32)
        m = bs_ref[...]
        for g in range(B_GROUPS):
            m = m + jnp.where(lane_group == g, r[g * B_CHUNK:(g + 1) * B_CHUNK], 0.0)
        mixed.append(m)
    out['yb'] = gu * jnp.concatenate(mixed, axis=0)


def _conv_stages(ca, cgate, gavg, dww_ref, dw_b, ln_g, ln_b, pww_ref, pw_b, zpad_ref, zsh_ref, seg, out):
    tile = ca.shape[0]
    z = ca * jax.nn.sigmoid(cgate)
    nseg = tile // seg
    pad_zeros = jnp.zeros((C_PAD, C_WIDTH), F32)
    for s in range(nseg):
        zpad_ref[s, 0:C_PAD, :] = pad_zeros
        zpad_ref[s, C_PAD + seg:2 * C_PAD + seg, :] = pad_zeros
        zpad_ref[s, C_PAD:C_PAD + seg, :] = z[s * seg:(s + 1) * seg]
    span = seg + 2 * C_PAD - SUBLANES
    convs = []
    for s in range(nseg):
        for r in range(1, SUBLANES):
            zsh_ref[s, r] = zpad_ref[s, r:r + span, :]
        acc = jnp.zeros((seg, C_WIDTH), F32)
        for j in range(C_KERNEL):
            start = C_PAD - C_KERNEL // 2 + j
            r, a = start % SUBLANES, start - start % SUBLANES
            tap = zpad_ref[s, a:a + seg, :] if r == 0 else zsh_ref[s, r, a:a + seg, :]
            acc = acc + dww_ref[j:j + 1, :] * tap
        convs.append(acc)
        yield
    ln = {}
    yield from _group_layer_norm_stages(jnp.concatenate(convs, axis=0) + dw_b, gavg, ln_g, ln_b, ln)
    zc = ln['ln']
    zc = (zc * jax.nn.sigmoid(zc)).astype(BF16)
    out['yc'] = jnp.dot(zc, pww_ref[...], preferred_element_type=F32) + pw_b


def _in_proj_kernel(x_ref, mod_ref, g_ref, w_ref, pa_ref):
    h = _norm_modulate(x_ref[0], g_ref[...], mod_ref[0, 0:1, :], mod_ref[0, 1:2, :]).astype(BF16)
    pa_ref[0] = jnp.dot(h, w_ref[...], preferred_element_type=F32)


def _in_proj_mix_kernel(x_ref, mod_ref, g_ref, w_ref, gavg_ref, bg_ref, bb_ref, ws_ref, bs_ref,
                        dww_ref, dwb_ref, cg_ref, cb_ref, pww_ref, pwb_ref,
                        pa_ref, gate_ref, ybc_ref, zpad_ref, zsh_ref, *, seg):
    h = _norm_modulate(x_ref[0], g_ref[...], mod_ref[0, 0:1, :], mod_ref[0, 1:2, :]).astype(BF16)
    gavg = gavg_ref[...]

    def proj(lo, n):
        return jnp.dot(h, w_ref[:, lo:lo + n], preferred_element_type=F32)

    chunks = [(pa_ref, lo) for lo in range(0, PA_COLS, PROJ_CHUNK)]
    chunks += [(gate_ref, lo) for lo in range(0, A_WIDTH, PROJ_CHUNK)]
    pending = iter(chunks)

    def project_next():
        nxt = next(pending, None)
        if nxt is not None:
            ref, lo = nxt
            w_lo = lo if ref is pa_ref else PA_COLS + lo
            ref[0, :, lo:lo + PROJ_CHUNK] = proj(w_lo, PROJ_CHUNK)

    c0 = PA_COLS + A_WIDTH + 2 * B_WIDTH
    b0 = PA_COLS + A_WIDTH
    out = {}
    branches = [
        _conv_stages(proj(c0, C_WIDTH), proj(c0 + C_WIDTH, C_WIDTH), gavg, dww_ref, dwb_ref[...],
                     cg_ref[...], cb_ref[...], pww_ref, pwb_ref[...], zpad_ref, zsh_ref, seg, out),
        _gmlp_stages(lambda lo: proj(b0 + lo, B_WIDTH), gavg, bg_ref[...], bb_ref[...], ws_ref, bs_ref, out),
    ]
    finished = object()
    while branches:
        branches = [g for g in branches if next(g, finished) is not finished]
        project_next()
    ybc_ref[0] = jnp.concatenate([out['yb'], out['yc']], axis=1).astype(BF16)
    for _ in chunks:
        project_next()


def _in_proj(x, mod, norm_g, w_in, tile):
    bsz, t_len, d = x.shape
    cols = w_in.shape[1]
    return pl.pallas_call(
        _in_proj_kernel,
        grid=(bsz, t_len // tile),
        in_specs=[
            pl.BlockSpec((1, tile, d), lambda b, i: (b, i, 0)),
            pl.BlockSpec((1, 6, d), lambda b, i: (b, 0, 0)),
            _full((1, d)),
            _full((d, cols)),
        ],
        out_specs=pl.BlockSpec((1, tile, cols), lambda b, i: (b, i, 0)),
        out_shape=jax.ShapeDtypeStruct((bsz, t_len, cols), F32),
        compiler_params=_cparams(("arbitrary", "arbitrary")),
        name="in_proj",
    )(x, mod, norm_g.reshape(1, d), w_in)


def _in_proj_mix(x, mod, lw, tile, seg):
    bsz, t_len, d = x.shape
    nseg = tile // seg

    def tok(cols):
        return pl.BlockSpec((1, tile, cols), lambda b, i: (b, i, 0))

    params = [lw['norm1_g'], lw['w_in'], lw['gavg'], lw['gmlp_ln_g'], lw['gmlp_ln_b'], lw['ws_cat'],
              lw['bs_tile'], lw['dw_w'], lw['dw_b'], lw['conv_ln_g'], lw['conv_ln_b'], lw['pw_w'], lw['pw_b']]
    return pl.pallas_call(
        functools.partial(_in_proj_mix_kernel, seg=seg),
        grid=(bsz, t_len // tile),
        in_specs=[tok(d), pl.BlockSpec((1, 6, d), lambda b, i: (b, 0, 0))] + [_full(p.shape) for p in params],
        out_specs=[tok(PA_COLS), tok(A_WIDTH), tok(B_WIDTH + C_WIDTH)],
        out_shape=[jax.ShapeDtypeStruct((bsz, t_len, PA_COLS), F32),
                   jax.ShapeDtypeStruct((bsz, t_len, A_WIDTH), F32),
                   jax.ShapeDtypeStruct((bsz, t_len, B_WIDTH + C_WIDTH), BF16)],
        scratch_shapes=[pltpu.VMEM((nseg, seg + 2 * C_PAD, C_WIDTH), F32),
                        pltpu.VMEM((nseg, SUBLANES, seg + 2 * C_PAD - SUBLANES, C_WIDTH), F32)],
        compiler_params=_cparams(("arbitrary", "arbitrary")),
        name="in_proj_mix",
    )(x, mod, *params)


def _lower_bounds(logits, layer):
    depth = logits.shape[0]
    rows = [logits[j] for j in range(depth)]
    m = rows[0]
    for r in rows[1:]:
        m = jnp.maximum(m, r)
    es = [jnp.exp(r - m) for r in rows]
    tot = es[0]
    for e in es[1:]:
        tot = tot + e
    lb = jnp.zeros_like(m)
    for j in range(1, layer + 1):
        lb = lb + es[j] / tot
    return lb


def _chunk_rows(c, lo=0, n=A_CHUNK):
    return slice(c * A_CHUNK + lo, c * A_CHUNK + lo + n)


def _hgrn_factors(q, z, v, lb, rev, n_chunks):
    width = q.shape[1]
    sub = A_SUB
    row = lax.broadcasted_iota(jnp.int32, (A_CHUNK, A_CHUNK), 0)
    col = lax.broadcasted_iota(jnp.int32, (A_CHUNK, A_CHUNK), 1)
    tri = ((col >= row) if rev else (col <= row)).astype(F32).astype(BF16)
    oml = 1.0 - lb
    sg = jax.nn.sigmoid(z)
    lf = jnp.log(lb + oml * sg)
    kk = oml * (1.0 - sg)
    hi = lf.astype(BF16)
    lo = (lf - hi.astype(F32)).astype(BF16)
    bs = [(jnp.dot(tri, hi[_chunk_rows(c)], preferred_element_type=F32)
           + jnp.dot(tri, lo[_chunk_rows(c)], preferred_element_type=F32)) * LOG2E
          for c in range(n_chunks)]
    wide, narrow = (0, 1) if rev else (1, 0)
    mid_row = sub // 2 if rev else sub // 2 - 1
    last_row = 0 if rev else A_CHUNK - 1
    b_last, b_own, b_wide, b_narrow, decs = [], [], [], [], []
    for bc in bs:
        mids = [bc[i * sub + mid_row:i * sub + mid_row + 1] for i in range(2)]
        b_last.append(jnp.broadcast_to(bc[last_row:last_row + 1], (A_CHUNK, width)))
        b_own += [jnp.broadcast_to(m, (sub, width)) for m in mids]
        b_wide.append(jnp.broadcast_to(mids[wide], (A_CHUNK, width)))
        b_narrow.append(mids[narrow] - bc[narrow * sub:(narrow + 1) * sub])
        decs.append(jnp.exp2(bc[last_row:last_row + 1]))
    b = jnp.concatenate(bs, axis=0)
    kk_narrow = jnp.concatenate([kk[_chunk_rows(c, narrow * sub, sub)] for c in range(n_chunks)], axis=0)
    return {
        'qe': (q * jnp.exp2(b)).astype(BF16),
        'kd': (kk * jnp.exp2(jnp.concatenate(b_last, axis=0) - b)).astype(BF16),
        'qd': (q * jnp.exp2(jnp.minimum(b - jnp.concatenate(b_own, axis=0), EXP2_CLAMP))).astype(BF16),
        'k_wide': (kk * jnp.exp2(jnp.minimum(jnp.concatenate(b_wide, axis=0) - b, EXP2_CLAMP))).astype(BF16),
        'k_narrow': (kk_narrow * jnp.exp2(jnp.minimum(jnp.concatenate(b_narrow, axis=0), EXP2_CLAMP))).astype(BF16),
        'v': v.astype(BF16),
        'dec': decs,
    }


def _hgrn_scores(f, rev, n_chunks):
    sub = A_SUB
    wide, narrow = (0, 1) if rev else (1, 0)
    row_n = lax.broadcasted_iota(jnp.int32, (sub, sub), 0)
    col_n = lax.broadcasted_iota(jnp.int32, (sub, sub), 1)
    row_w = lax.broadcasted_iota(jnp.int32, (sub, A_CHUNK), 0)
    col_w = lax.broadcasted_iota(jnp.int32, (sub, A_CHUNK), 1)
    mask_n = (col_n >= row_n) if rev else (col_n <= row_n)
    mask_w = (col_w >= row_w) if rev else (col_w <= row_w + sub)
    out = []
    for c in range(n_chunks):
        per_head = []
        for h in range(A_HEADS):
            hc = slice(h * A_DK, (h + 1) * A_DK)
            s_n = lax.dot_general(f['qd'][_chunk_rows(c, narrow * sub, sub), hc],
                                  f['k_narrow'][c * sub:(c + 1) * sub, hc], _NT, preferred_element_type=F32)
            s_w = lax.dot_general(f['qd'][_chunk_rows(c, wide * sub, sub), hc],
                                  f['k_wide'][_chunk_rows(c), hc], _NT, preferred_element_type=F32)
            per_head.append((jnp.where(mask_n, s_n, 0.0).astype(BF16),
                             jnp.where(mask_w, s_w, 0.0).astype(BF16)))
        out.append(per_head)
    return out


def _hgrn_intra(f, scores, rev, n_chunks):
    sub = A_SUB
    wide, narrow = (0, 1) if rev else (1, 0)
    out = []
    for c in range(n_chunks):
        per_head = []
        for h in range(A_HEADS):
            hc = slice(h * A_DK, (h + 1) * A_DK)
            s_n, s_w = scores[c][h]
            o_n = jnp.dot(s_n, f['v'][_chunk_rows(c, narrow * sub, sub), hc], preferred_element_type=F32)
            o_w = jnp.dot(s_w, f['v'][_chunk_rows(c), hc], preferred_element_type=F32)
            intra = jnp.concatenate([o_w, o_n] if rev else [o_n, o_w], axis=0)
            upd = lax.dot_general(f['v'][_chunk_rows(c), hc], f['kd'][_chunk_rows(c), hc], _TN,
                                  preferred_element_type=F32)
            per_head.append((intra, upd))
        out.append(per_head)
    return out


def _hgrn_scan(f, intra, st_ref, d, o_ref, rev, n_chunks):
    states = [st_ref[0, d, h] for h in range(A_HEADS)]
    for c in (range(n_chunks - 1, -1, -1) if rev else range(n_chunks)):
        outs = []
        for h in range(A_HEADS):
            hc = slice(h * A_DK, (h + 1) * A_DK)
            inter = lax.dot_general(f['qe'][_chunk_rows(c), hc], states[h].astype(BF16), _NT,
                                    preferred_element_type=F32)
            outs.append(inter + intra[c][h][0])
            states[h] = states[h] * f['dec'][c][:, hc] + intra[c][h][1]
        o_ref[0, _chunk_rows(c), :] = jnp.concatenate(outs, axis=1)
    for h in range(A_HEADS):
        st_ref[0, d, h] = states[h]


def _hgrn_kernel(qf_ref, zf_ref, vf_ref, qb_ref, zb_ref, vb_ref, lbl_ref, s0_ref,
                 of_ref, ob_ref, st_ref, *, layer, n_chunks):
    @pl.when(pl.program_id(1) == 0)
    def _():
        st_ref[...] = s0_ref[...]

    lb = _lower_bounds(lbl_ref[...], layer)
    dirs = ((qf_ref, zf_ref, vf_ref, of_ref, False), (qb_ref, zb_ref, vb_ref, ob_ref, True))
    facs = [_hgrn_factors(q[0], z[0], v[0], lb[d:d + 1], rev, n_chunks)
            for d, (q, z, v, _, rev) in enumerate(dirs)]
    scores = [_hgrn_scores(facs[d], dirs[d][4], n_chunks) for d in range(2)]
    intra = [_hgrn_intra(facs[d], scores[d], dirs[d][4], n_chunks) for d in range(2)]
    for d in range(2):
        _hgrn_scan(facs[d], intra[d], st_ref, d, dirs[d][3], dirs[d][4], n_chunks)


def _hgrn(pa, lb_logits, s0, layer, tile):
    bsz, t_len, _ = pa.shape
    nb = t_len // tile
    depth = lb_logits.shape[0]

    def col_spec(colblk, rev):
        if rev:
            return pl.BlockSpec((1, tile, A_WIDTH), lambda b, i: (b, nb - 1 - i, colblk))
        return pl.BlockSpec((1, tile, A_WIDTH), lambda b, i: (b, i, colblk))

    st_spec = pl.BlockSpec((1, 2, A_HEADS, A_DK, A_DK), lambda b, i: (b, 0, 0, 0, 0))
    o_sds = jax.ShapeDtypeStruct((bsz, t_len, A_WIDTH), F32)
    return pl.pallas_call(
        functools.partial(_hgrn_kernel, layer=layer, n_chunks=tile // A_CHUNK),
        grid=(bsz, nb),
        in_specs=[col_spec(0, False), col_spec(1, False), col_spec(3, False),
                  col_spec(0, True), col_spec(2, True), col_spec(3, True),
                  _full((depth, 2, A_WIDTH)), st_spec],
        out_specs=[col_spec(0, False), col_spec(0, True), st_spec],
        out_shape=[o_sds, o_sds, jax.ShapeDtypeStruct(s0.shape, F32)],
        compiler_params=_cparams(("arbitrary", "arbitrary")),
        name="hgrn",
    )(pa, pa, pa, pa, pa, pa, lb_logits, s0)


def _out_ffn_kernel(x_ref, of_ref, ob_ref, gate_ref, ybc_ref, mod_ref, og_ref, wo_ref,
                    g_ref, w13_ref, w2_ref, fg_ref, o_ref, *, d_ff, chunk, final_norm):
    o = of_ref[0] + ob_ref[0]
    parts = []
    for h in range(A_HEADS):
        oh = o[:, h * A_DK:(h + 1) * A_DK]
        parts.append(oh * lax.rsqrt(jnp.mean(oh * oh, axis=-1, keepdims=True) + EPS))
    gate = gate_ref[0]
    ya = jnp.concatenate(parts, axis=1) * og_ref[...] * (gate * jax.nn.sigmoid(gate))
    y = (jnp.dot(ya.astype(BF16), wo_ref[0:A_WIDTH, :], preferred_element_type=F32)
         + jnp.dot(ybc_ref[0], wo_ref[A_WIDTH:, :], preferred_element_type=F32))
    x = x_ref[0] + mod_ref[0, 2:3, :] * y
    h = _norm_modulate(x, g_ref[...], mod_ref[0, 3:4, :], mod_ref[0, 4:5, :]).astype(BF16)
    acc = jnp.zeros(x.shape, F32)
    for c in range(d_ff // chunk):
        a = jnp.dot(h, w13_ref[:, c * chunk:(c + 1) * chunk], preferred_element_type=F32)
        b = jnp.dot(h, w13_ref[:, d_ff + c * chunk:d_ff + (c + 1) * chunk], preferred_element_type=F32)
        act = (a * jax.nn.sigmoid(a) * b).astype(BF16)
        acc = acc + jnp.dot(act, w2_ref[c * chunk:(c + 1) * chunk, :], preferred_element_type=F32)
    y = x + mod_ref[0, 5:6, :] * acc
    if final_norm:
        y = y * lax.rsqrt(jnp.mean(y * y, axis=-1, keepdims=True) + EPS) * fg_ref[...]
    o_ref[0] = y


def _out_ffn(x, o_f, o_b, gate, ybc, mod, lw, final_g, tile, final_norm):
    bsz, t_len, d = x.shape
    d_ff = lw['ffn_w2'].shape[0]

    def tok(cols):
        return pl.BlockSpec((1, tile, cols), lambda b, i: (b, i, 0))

    params = [lw['onorm_g'], lw['w_out'], lw['norm2_g'], lw['ffn_w13'], lw['ffn_w2'], final_g.reshape(1, d)]
    return pl.pallas_call(
        functools.partial(_out_ffn_kernel, d_ff=d_ff, chunk=FFN_CHUNK, final_norm=final_norm),
        grid=(bsz, t_len // tile),
        in_specs=[tok(d), tok(A_WIDTH), tok(A_WIDTH), tok(A_WIDTH), tok(B_WIDTH + C_WIDTH),
                  pl.BlockSpec((1, 6, d), lambda b, i: (b, 0, 0))] + [_full(p.shape) for p in params],
        out_specs=tok(d),
        out_shape=jax.ShapeDtypeStruct(x.shape, F32),
        compiler_params=_cparams(("arbitrary", "arbitrary")),
        name="out_ffn",
    )(x, o_f, o_b, gate, ybc, mod, *params)


def _layer_params(l, gavg, norm1_g, w_in, hgrn_onorm_g, gmlp_ln_g, gmlp_ln_b, gmlp_w_s, gmlp_b_s,
                  conv_dw_w, conv_dw_b, conv_ln_g, conv_ln_b, conv_pw_w, conv_pw_b, w_out,
                  norm2_g, ffn_w13, ffn_w2):
    row = lambda a: a.reshape(1, -1).astype(F32)
    return {
        'norm1_g': row(norm1_g[l]), 'w_in': w_in[l].astype(BF16),
        'onorm_g': row(hgrn_onorm_g[l]), 'gavg': gavg,
        'gmlp_ln_g': row(gmlp_ln_g[l]), 'gmlp_ln_b': row(gmlp_ln_b[l]),
        'ws_cat': gmlp_w_s[l].reshape(B_GROUPS * B_CHUNK, B_CHUNK).astype(BF16),
        'bs_tile': jnp.repeat(gmlp_b_s[l].T, B_WIDTH // B_GROUPS, axis=1).astype(F32),
        'dw_w': conv_dw_w[l].astype(F32), 'dw_b': row(conv_dw_b[l]),
        'conv_ln_g': row(conv_ln_g[l]), 'conv_ln_b': row(conv_ln_b[l]),
        'pw_w': conv_pw_w[l].astype(BF16), 'pw_b': row(conv_pw_b[l]),
        'w_out': w_out[l].astype(BF16),
        'norm2_g': row(norm2_g[l]), 'ffn_w13': ffn_w13[l].astype(BF16), 'ffn_w2': ffn_w2[l].astype(BF16),
    }


def kernel(x, c, ctx, c_ctx, ada_w, ada_b, norm1_g, w_in, hgrn_lb_logits, hgrn_onorm_g, gmlp_ln_g, gmlp_ln_b, gmlp_w_s, gmlp_b_s, conv_dw_w, conv_dw_b, conv_ln_g, conv_ln_b, conv_pw_w, conv_pw_b, w_out, norm2_g, ffn_w13, ffn_w2, final_norm_g):
    bsz, t_len, d = x.shape
    ctx_len = ctx.shape[1]
    depth = ada_w.shape[0]
    lb_logits = hgrn_lb_logits.astype(F32)

    cs = jnp.concatenate([c, c_ctx[None, :], jnp.zeros((MOD_ROWS - bsz - 1, d), F32)], axis=0)
    mod_all = _modulation(cs, ada_w, ada_b)
    grp = jnp.arange(B_WIDTH) // (B_WIDTH // B_GROUPS)
    gavg = ((grp[:, None] == grp[None, :]).astype(F32) / (B_WIDTH // B_GROUPS)).astype(BF16)
    zero_state = jnp.zeros((bsz, 2, A_HEADS, A_DK, A_DK), F32)

    for l in range(depth):
        last = l == depth - 1
        lw = _layer_params(l, gavg, norm1_g, w_in, hgrn_onorm_g, gmlp_ln_g, gmlp_ln_b, gmlp_w_s,
                           gmlp_b_s, conv_dw_w, conv_dw_b, conv_ln_g, conv_ln_b, conv_pw_w,
                           conv_pw_b, w_out, norm2_g, ffn_w13, ffn_w2)
        mod = mod_all[l, :bsz].reshape(bsz, 6, d)
        mod_ctx = jnp.broadcast_to(mod_all[l, bsz].reshape(1, 6, d), (bsz, 6, d))
        if last:
            pa_c = _in_proj(ctx, mod_ctx, lw['norm1_g'], lw['w_in'][:, :PA_COLS], ctx_len)
            _, _, state = _hgrn(pa_c, lb_logits, zero_state, l, ctx_len)
        else:
            pa_c, gate_c, ybc_c = _in_proj_mix(ctx, mod_ctx, lw, ctx_len, ctx_len)
            of_c, ob_c, state = _hgrn(pa_c, lb_logits, zero_state, l, ctx_len)
            ctx = _out_ffn(ctx, of_c, ob_c, gate_c, ybc_c, mod_ctx, lw, final_norm_g, ctx_len, False)
        pa, gate, ybc = _in_proj_mix(x, mod, lw, 512, t_len // (t_len // GRID_W))
        o_f, o_b, _ = _hgrn(pa, lb_logits, state, l, 512)
        x = _out_ffn(x, o_f, o_b, gate, ybc, mod, lw, final_norm_g, 512, last)
    return x
```

```python
import functools
from typing import NamedTuple

import jax
import jax.numpy as jnp
from jax import lax
from jax.experimental import pallas as pl
from jax.experimental.pallas import tpu as pltpu

F32 = jnp.float32
BF16 = jnp.bfloat16

D_MODEL = 1024
GRID_W = 64
A_HEADS = 4
A_DK = 128
A_WIDTH = A_HEADS * A_DK
A_CHUNK = 64
A_SUB = 32
B_GROUPS = 4
B_WIDTH = 256
B_CHUNK = 128
C_GROUPS = 4
C_WIDTH = 256
C_KERNEL = 31
C_PAD = 16
SUBLANES = 8
PROJ_CHUNK = 256
FFN_CHUNK = 256
PA_COLS = 4 * A_WIDTH
PM_COLS = A_WIDTH + 2 * B_WIDTH + 2 * C_WIDTH
EPS = 1e-6
EXP2_CLAMP = 80.0
LOG2E = 1.4426950408889634
MOD_ROWS = 8

V7X_VMEM_LIMIT = 56 * 1024 * 1024

_NT = (((1,), (1,)), ((), ()))
_TN = (((0,), (0,)), ((), ()))


def _cparams(sem, flags=None):
    return pltpu.CompilerParams(dimension_semantics=sem, vmem_limit_bytes=V7X_VMEM_LIMIT, flags=flags)


def _full(shape):
    n = len(shape)
    return pl.BlockSpec(shape, lambda *_: (0,) * n)


class _LayerSlice(NamedTuple):
    stacked: jax.Array
    layer: int

    @property
    def shape(self):
        return self.stacked.shape[1:]


def _param_spec(p, block=None):
    if not isinstance(p, _LayerSlice):
        return _full(p.shape)
    block = p.shape if block is None else block
    index = (p.layer,) + (0,) * len(block)
    return pl.BlockSpec((None,) + tuple(block), lambda *_: index, pipeline_mode=pl.Buffered(1))


def _param_arg(p):
    return p.stacked if isinstance(p, _LayerSlice) else p


def _mod_kernel(cs_ref, w_ref, b_ref, o_ref):
    s = cs_ref[...]
    s = s * jax.nn.sigmoid(s)
    w = w_ref[0]
    s_hi = s.astype(BF16)
    s_lo = (s - s_hi.astype(F32)).astype(BF16)
    w_hi = w.astype(BF16)
    w_lo = (w - w_hi.astype(F32)).astype(BF16)
    r = jnp.dot(jnp.concatenate([s_hi, s_lo], axis=0), w_hi, preferred_element_type=F32)
    o_ref[0] = (r[:MOD_ROWS] + r[MOD_ROWS:] + jnp.dot(s_hi, w_lo, preferred_element_type=F32)
                + b_ref[0])


def _modulation(cs, ada_w, ada_b):
    depth, d, n = ada_w.shape
    tn = 1024
    return pl.pallas_call(
        _mod_kernel,
        grid=(depth, n // tn),
        in_specs=[
            pl.BlockSpec((MOD_ROWS, d), lambda l, j: (0, 0)),
            pl.BlockSpec((1, d, tn), lambda l, j: (l, 0, j)),
            pl.BlockSpec((1, 1, tn), lambda l, j: (l, 0, j)),
        ],
        out_specs=pl.BlockSpec((1, MOD_ROWS, tn), lambda l, j: (l, 0, j)),
        out_shape=jax.ShapeDtypeStruct((depth, MOD_ROWS, n), F32),
        compiler_params=_cparams(("arbitrary", "arbitrary")),
        name="modulation",
    )(cs, ada_w, ada_b.reshape(depth, 1, n))


def _norm_modulate(x, g, shift, scale):
    ms = jnp.mean(x * x, axis=-1, keepdims=True)
    y = x * lax.rsqrt(ms + EPS) * g
    return y * (1.0 + scale) + shift


def _group_mean(x, gavg):
    return jnp.dot(x.astype(BF16), gavg, preferred_element_type=F32)


def _group_layer_norm_stages(x, gavg, g, b, out):
    mu = _group_mean(x, gavg)
    yield
    xc = x - mu
    var = _group_mean(xc * xc, gavg)
    yield
    out['ln'] = xc * lax.rsqrt(var + EPS) * g + b


def _gmlp_stages(project, gavg, ln_g, ln_b, ws_ref, bs_ref, out):
    u = project(0)
    v = project(B_WIDTH)
    tile = u.shape[0]
    yield
    ln = {}
    yield from _group_layer_norm_stages(jax.nn.gelu(v), gavg, ln_g, ln_b, ln)
    vn = ln['ln'].astype(BF16)
    gu = jax.nn.gelu(u)
    lane_group = lax.broadcasted_iota(jnp.int32, (B_CHUNK, B_WIDTH), 1) // (B_WIDTH // B_GROUPS)
    mixed = []
    for n in range(tile // B_CHUNK):
        yield
        r = jnp.dot(ws_ref[...], vn[n * B_CHUNK:(n + 1) * B_CHUNK], preferred_element_type=F32)
        m = bs_ref[...]
        for g in range(B_GROUPS):
            m = m + jnp.where(lane_group == g, r[g * B_CHUNK:(g + 1) * B_CHUNK], 0.0)
        mixed.append(m)
    out['yb'] = gu * jnp.concatenate(mixed, axis=0)


def _conv_stages(ca, cgate, gavg, dww_ref, dw_b, ln_g, ln_b, pww_ref, pw_b, zpad_ref, zsh_ref, seg, out):
    tile = ca.shape[0]
    z = ca * jax.nn.sigmoid(cgate)
    nseg = tile // seg
    pad_zeros = jnp.zeros((C_PAD, C_WIDTH), F32)
    for s in range(nseg):
        zpad_ref[s, 0:C_PAD, :] = pad_zeros
        zpad_ref[s, C_PAD + seg:2 * C_PAD + seg, :] = pad_zeros
        zpad_ref[s, C_PAD:C_PAD + seg, :] = z[s * seg:(s + 1) * seg]
    span = seg + 2 * C_PAD - SUBLANES
    convs = []
    for s in range(nseg):
        for r in range(1, SUBLANES):
            zsh_ref[s, r] = zpad_ref[s, r:r + span, :]
        acc = jnp.zeros((seg, C_WIDTH), F32)
        for j in range(C_KERNEL):
            start = C_PAD - C_KERNEL // 2 + j
            r, a = start % SUBLANES, start - start % SUBLANES
            tap = zpad_ref[s, a:a + seg, :] if r == 0 else zsh_ref[s, r, a:a + seg, :]
            acc = acc + dww_ref[j:j + 1, :] * tap
        convs.append(acc)
        yield
    ln = {}
    yield from _group_layer_norm_stages(jnp.concatenate(convs, axis=0) + dw_b, gavg, ln_g, ln_b, ln)
    zc = ln['ln']
    zc = (zc * jax.nn.sigmoid(zc)).astype(BF16)
    out['yc'] = jnp.dot(zc, pww_ref[...], preferred_element_type=F32) + pw_b


def _in_proj_kernel(x_ref, mod_ref, g_ref, w_ref, pa_ref):
    h = _norm_modulate(x_ref[0], g_ref[...], mod_ref[0, 0:1, :], mod_ref[0, 1:2, :]).astype(BF16)
    pa_ref[0] = jnp.dot(h, w_ref[...], preferred_element_type=F32)


def _in_proj_mix_kernel(x_ref, mod_ref, g_ref, w_ref, gavg_ref, bg_ref, bb_ref, ws_ref, bs_ref,
                        dww_ref, dwb_ref, cg_ref, cb_ref, pww_ref, pwb_ref,
                        pa_ref, gate_ref, ybc_ref, zpad_ref, zsh_ref, *, seg):
    h = _norm_modulate(x_ref[0], g_ref[...], mod_ref[0, 0:1, :], mod_ref[0, 1:2, :]).astype(BF16)
    gavg = gavg_ref[...]

    def proj(lo, n):
        return jnp.dot(h, w_ref[:, lo:lo + n], preferred_element_type=F32)

    chunks = [(pa_ref, lo) for lo in range(0, PA_COLS, PROJ_CHUNK)]
    chunks += [(gate_ref, lo) for lo in range(0, A_WIDTH, PROJ_CHUNK)]
    pending = iter(chunks)

    def project_next():
        nxt = next(pending, None)
        if nxt is not None:
            ref, lo = nxt
            w_lo = lo if ref is pa_ref else PA_COLS + lo
            ref[0, :, lo:lo + PROJ_CHUNK] = proj(w_lo, PROJ_CHUNK)

    c0 = PA_COLS + A_WIDTH + 2 * B_WIDTH
    b0 = PA_COLS + A_WIDTH
    out = {}
    branches = [
        _conv_stages(proj(c0, C_WIDTH), proj(c0 + C_WIDTH, C_WIDTH), gavg, dww_ref, dwb_ref[...],
                     cg_ref[...], cb_ref[...], pww_ref, pwb_ref[...], zpad_ref, zsh_ref, seg, out),
        _gmlp_stages(lambda lo: proj(b0 + lo, B_WIDTH), gavg, bg_ref[...], bb_ref[...], ws_ref, bs_ref, out),
    ]
    finished = object()
    while branches:
        branches = [g for g in branches if next(g, finished) is not finished]
        project_next()
    ybc_ref[0] = jnp.concatenate([out['yb'], out['yc']], axis=1).astype(BF16)
    for _ in chunks:
        project_next()


def _in_proj(x, mod, lw, tile):
    bsz, t_len, d = x.shape
    return pl.pallas_call(
        _in_proj_kernel,
        grid=(bsz, t_len // tile),
        in_specs=[
            pl.BlockSpec((1, tile, d), lambda b, i: (b, i, 0)),
            pl.BlockSpec((1, 6, d), lambda b, i: (b, 0, 0)),
            _full((1, d)),
            _param_spec(lw['w_in'], (d, PA_COLS)),
        ],
        out_specs=pl.BlockSpec((1, tile, PA_COLS), lambda b, i: (b, i, 0)),
        out_shape=jax.ShapeDtypeStruct((bsz, t_len, PA_COLS), F32),
        compiler_params=_cparams(("arbitrary", "arbitrary")),
        name="in_proj",
    )(x, mod, lw['norm1_g'], _param_arg(lw['w_in']))


def _in_proj_mix(x, mod, lw, tile, seg):
    bsz, t_len, d = x.shape
    nseg = tile // seg

    def tok(cols):
        return pl.BlockSpec((1, tile, cols), lambda b, i: (b, i, 0))

    params = [lw['norm1_g'], lw['w_in'], lw['gavg'], lw['gmlp_ln_g'], lw['gmlp_ln_b'], lw['ws_cat'],
              lw['bs_tile'], lw['dw_w'], lw['dw_b'], lw['conv_ln_g'], lw['conv_ln_b'], lw['pw_w'], lw['pw_b']]
    return pl.pallas_call(
        functools.partial(_in_proj_mix_kernel, seg=seg),
        grid=(bsz, t_len // tile),
        in_specs=[tok(d), pl.BlockSpec((1, 6, d), lambda b, i: (b, 0, 0))] + [_param_spec(p) for p in params],
        out_specs=[tok(PA_COLS), tok(A_WIDTH), tok(B_WIDTH + C_WIDTH)],
        out_shape=[jax.ShapeDtypeStruct((bsz, t_len, PA_COLS), F32),
                   jax.ShapeDtypeStruct((bsz, t_len, A_WIDTH), F32),
                   jax.ShapeDtypeStruct((bsz, t_len, B_WIDTH + C_WIDTH), BF16)],
        scratch_shapes=[pltpu.VMEM((nseg, seg + 2 * C_PAD, C_WIDTH), F32),
                        pltpu.VMEM((nseg, SUBLANES, seg + 2 * C_PAD - SUBLANES, C_WIDTH), F32)],
        compiler_params=_cparams(("arbitrary", "arbitrary")),
        name="in_proj_mix",
    )(x, mod, *[_param_arg(p) for p in params])


def _lower_bounds(logits, layer):
    depth = logits.shape[0]
    rows = [logits[j] for j in range(depth)]
    m = rows[0]
    for r in rows[1:]:
        m = jnp.maximum(m, r)
    es = [jnp.exp(r - m) for r in rows]
    tot = es[0]
    for e in es[1:]:
        tot = tot + e
    lb = jnp.zeros_like(m)
    for j in range(1, layer + 1):
        lb = lb + es[j] / tot
    return lb


def _chunk_rows(c, lo=0, n=A_CHUNK):
    return slice(c * A_CHUNK + lo, c * A_CHUNK + lo + n)


def _hgrn_factors(q, z, v, lb, rev, n_chunks):
    width = q.shape[1]
    sub = A_SUB
    row = lax.broadcasted_iota(jnp.int32, (A_CHUNK, A_CHUNK), 0)
    col = lax.broadcasted_iota(jnp.int32, (A_CHUNK, A_CHUNK), 1)
    tri = ((col >= row) if rev else (col <= row)).astype(F32).astype(BF16)
    oml = 1.0 - lb
    sg = jax.nn.sigmoid(z)
    lf = jnp.log(lb + oml * sg)
    kk = oml * (1.0 - sg)
    hi = lf.astype(BF16)
    lo = (lf - hi.astype(F32)).astype(BF16)
    bs = [(jnp.dot(tri, hi[_chunk_rows(c)], preferred_element_type=F32)
           + jnp.dot(tri, lo[_chunk_rows(c)], preferred_element_type=F32)) * LOG2E
          for c in range(n_chunks)]
    wide, narrow = (0, 1) if rev else (1, 0)
    mid_row = sub // 2 if rev else sub // 2 - 1
    last_row = 0 if rev else A_CHUNK - 1
    b_last, b_own, b_wide, b_narrow, decs = [], [], [], [], []
    for bc in bs:
        mids = [bc[i * sub + mid_row:i * sub + mid_row + 1] for i in range(2)]
        b_last.append(jnp.broadcast_to(bc[last_row:last_row + 1], (A_CHUNK, width)))
        b_own += [jnp.broadcast_to(m, (sub, width)) for m in mids]
        b_wide.append(jnp.broadcast_to(mids[wide], (A_CHUNK, width)))
        b_narrow.append(mids[narrow] - bc[narrow * sub:(narrow + 1) * sub])
        decs.append(jnp.exp2(bc[last_row:last_row + 1]))
    b = jnp.concatenate(bs, axis=0)
    kk_narrow = jnp.concatenate([kk[_chunk_rows(c, narrow * sub, sub)] for c in range(n_chunks)], axis=0)
    return {
        'qe': (q * jnp.exp2(b)).astype(BF16),
        'kd': (kk * jnp.exp2(jnp.concatenate(b_last, axis=0) - b)).astype(BF16),
        'qd': (q * jnp.exp2(jnp.minimum(b - jnp.concatenate(b_own, axis=0), EXP2_CLAMP))).astype(BF16),
        'k_wide': (kk * jnp.exp2(jnp.minimum(jnp.concatenate(b_wide, axis=0) - b, EXP2_CLAMP))).astype(BF16),
        'k_narrow': (kk_narrow * jnp.exp2(jnp.minimum(jnp.concatenate(b_narrow, axis=0), EXP2_CLAMP))).astype(BF16),
        'v': v.astype(BF16),
        'dec': decs,
    }


def _hgrn_scores(f, rev, n_chunks):
    sub = A_SUB
    wide, narrow = (0, 1) if rev else (1, 0)
    row_n = lax.broadcasted_iota(jnp.int32, (sub, sub), 0)
    col_n = lax.broadcasted_iota(jnp.int32, (sub, sub), 1)
    row_w = lax.broadcasted_iota(jnp.int32, (sub, A_CHUNK), 0)
    col_w = lax.broadcasted_iota(jnp.int32, (sub, A_CHUNK), 1)
    mask_n = (col_n >= row_n) if rev else (col_n <= row_n)
    mask_w = (col_w >= row_w) if rev else (col_w <= row_w + sub)
    out = []
    for c in range(n_chunks):
        per_head = []
        for h in range(A_HEADS):
            hc = slice(h * A_DK, (h + 1) * A_DK)
            s_n = lax.dot_general(f['qd'][_chunk_rows(c, narrow * sub, sub), hc],
                                  f['k_narrow'][c * sub:(c + 1) * sub, hc], _NT, preferred_element_type=F32)
            s_w = lax.dot_general(f['qd'][_chunk_rows(c, wide * sub, sub), hc],
                                  f['k_wide'][_chunk_rows(c), hc], _NT, preferred_element_type=F32)
            per_head.append((jnp.where(mask_n, s_n, 0.0).astype(BF16),
                             jnp.where(mask_w, s_w, 0.0).astype(BF16)))
        out.append(per_head)
    return out


def _hgrn_intra(f, scores, rev, n_chunks):
    sub = A_SUB
    wide, narrow = (0, 1) if rev else (1, 0)
    out = []
    for c in range(n_chunks):
        per_head = []
        for h in range(A_HEADS):
            hc = slice(h * A_DK, (h + 1) * A_DK)
            s_n, s_w = scores[c][h]
            o_n = jnp.dot(s_n, f['v'][_chunk_rows(c, narrow * sub, sub), hc], preferred_element_type=F32)
            o_w = jnp.dot(s_w, f['v'][_chunk_rows(c), hc], preferred_element_type=F32)
            intra = jnp.concatenate([o_w, o_n] if rev else [o_n, o_w], axis=0)
            upd = lax.dot_general(f['v'][_chunk_rows(c), hc], f['kd'][_chunk_rows(c), hc], _TN,
                                  preferred_element_type=F32)
            per_head.append((intra, upd))
        out.append(per_head)
    return out


def _hgrn_scan(f, intra, st_ref, d, o_ref, rev, n_chunks):
    states = [st_ref[0, d, h] for h in range(A_HEADS)]
    for c in (range(n_chunks - 1, -1, -1) if rev else range(n_chunks)):
        outs = []
        for h in range(A_HEADS):
            hc = slice(h * A_DK, (h + 1) * A_DK)
            inter = lax.dot_general(f['qe'][_chunk_rows(c), hc], states[h].astype(BF16), _NT,
                                    preferred_element_type=F32)
            outs.append(inter + intra[c][h][0])
            states[h] = states[h] * f['dec'][c][:, hc] + intra[c][h][1]
        o_ref[0, _chunk_rows(c), :] = jnp.concatenate(outs, axis=1)
    for h in range(A_HEADS):
        st_ref[0, d, h] = states[h]


def _hgrn_kernel(qf_ref, zf_ref, vf_ref, qb_ref, zb_ref, vb_ref, lbl_ref, s0_ref,
                 of_ref, ob_ref, st_ref, *, layer, n_chunks):
    @pl.when(pl.program_id(1) == 0)
    def _():
        st_ref[...] = s0_ref[...]

    lb = _lower_bounds(lbl_ref[...], layer)
    dirs = ((qf_ref, zf_ref, vf_ref, of_ref, False), (qb_ref, zb_ref, vb_ref, ob_ref, True))
    facs = [_hgrn_factors(q[0], z[0], v[0], lb[d:d + 1], rev, n_chunks)
            for d, (q, z, v, _, rev) in enumerate(dirs)]
    scores = [_hgrn_scores(facs[d], dirs[d][4], n_chunks) for d in range(2)]
    intra = [_hgrn_intra(facs[d], scores[d], dirs[d][4], n_chunks) for d in range(2)]
    for d in range(2):
        _hgrn_scan(facs[d], intra[d], st_ref, d, dirs[d][3], dirs[d][4], n_chunks)


def _hgrn(pa, lb_logits, s0, layer, tile):
    bsz, t_len, _ = pa.shape
    nb = t_len // tile
    depth = lb_logits.shape[0]

    def col_spec(colblk, rev):
        if rev:
            return pl.BlockSpec((1, tile, A_WIDTH), lambda b, i: (b, nb - 1 - i, colblk))
        return pl.BlockSpec((1, tile, A_WIDTH), lambda b, i: (b, i, colblk))

    st_spec = pl.BlockSpec((1, 2, A_HEADS, A_DK, A_DK), lambda b, i: (b, 0, 0, 0, 0))
    o_sds = jax.ShapeDtypeStruct((bsz, t_len, A_WIDTH), F32)
    return pl.pallas_call(
        functools.partial(_hgrn_kernel, layer=layer, n_chunks=tile // A_CHUNK),
        grid=(bsz, nb),
        in_specs=[col_spec(0, False), col_spec(1, False), col_spec(3, False),
                  col_spec(0, True), col_spec(2, True), col_spec(3, True),
                  _full((depth, 2, A_WIDTH)), st_spec],
        out_specs=[col_spec(0, False), col_spec(0, True), st_spec],
        out_shape=[o_sds, o_sds, jax.ShapeDtypeStruct(s0.shape, F32)],
        compiler_params=_cparams(("arbitrary", "arbitrary")),
        name="hgrn",
    )(pa, pa, pa, pa, pa, pa, lb_logits, s0)


def _out_ffn_kernel(x_ref, of_ref, ob_ref, gate_ref, ybc_ref, mod_ref, og_ref, wo_ref,
                    g_ref, w13_ref, w2_ref, fg_ref, o_ref, *, d_ff, chunk, final_norm):
    o = of_ref[0] + ob_ref[0]
    parts = []
    for h in range(A_HEADS):
        oh = o[:, h * A_DK:(h + 1) * A_DK]
        parts.append(oh * lax.rsqrt(jnp.mean(oh * oh, axis=-1, keepdims=True) + EPS))
    gate = gate_ref[0]
    ya = jnp.concatenate(parts, axis=1) * og_ref[...] * (gate * jax.nn.sigmoid(gate))
    y = (jnp.dot(ya.astype(BF16), wo_ref[0:A_WIDTH, :], preferred_element_type=F32)
         + jnp.dot(ybc_ref[0], wo_ref[A_WIDTH:, :], preferred_element_type=F32))
    x = x_ref[0] + mod_ref[0, 2:3, :] * y
    h = _norm_modulate(x, g_ref[...], mod_ref[0, 3:4, :], mod_ref[0, 4:5, :]).astype(BF16)
    acts = []
    for c in range(d_ff // chunk):
        a = jnp.dot(h, w13_ref[:, c * chunk:(c + 1) * chunk], preferred_element_type=F32)
        b = jnp.dot(h, w13_ref[:, d_ff + c * chunk:d_ff + (c + 1) * chunk], preferred_element_type=F32)
        acts.append((a * jax.nn.sigmoid(a) * b).astype(BF16))
    acc = jnp.dot(jnp.concatenate(acts, axis=1), w2_ref[...], preferred_element_type=F32)
    y = x + mod_ref[0, 5:6, :] * acc
    if final_norm:
        y = y * lax.rsqrt(jnp.mean(y * y, axis=-1, keepdims=True) + EPS) * fg_ref[...]
    o_ref[0] = y


def _out_ffn(x, o_f, o_b, gate, ybc, mod, lw, final_g, tile, final_norm):
    bsz, t_len, d = x.shape
    d_ff = lw['ffn_w2'].shape[0]

    def tok(cols):
        return pl.BlockSpec((1, tile, cols), lambda b, i: (b, i, 0))

    params = [lw['onorm_g'], lw['w_out'], lw['norm2_g'], lw['ffn_w13'], lw['ffn_w2'], final_g.reshape(1, d)]
    return pl.pallas_call(
        functools.partial(_out_ffn_kernel, d_ff=d_ff, chunk=FFN_CHUNK, final_norm=final_norm),
        grid=(bsz, t_len // tile),
        in_specs=[tok(d), tok(A_WIDTH), tok(A_WIDTH), tok(A_WIDTH), tok(B_WIDTH + C_WIDTH),
                  pl.BlockSpec((1, 6, d), lambda b, i: (b, 0, 0))] + [_param_spec(p) for p in params],
        out_specs=tok(d),
        out_shape=jax.ShapeDtypeStruct(x.shape, F32),
        compiler_params=_cparams(("arbitrary", "arbitrary")),
        name="out_ffn",
    )(x, o_f, o_b, gate, ybc, mod, *[_param_arg(p) for p in params])


def _layer_params(l, gavg, mxu_weights, norm1_g, hgrn_onorm_g, gmlp_ln_g, gmlp_ln_b, gmlp_b_s,
                  conv_dw_w, conv_dw_b, conv_ln_g, conv_ln_b, conv_pw_b, norm2_g):
    row = lambda a: a.reshape(1, -1).astype(F32)
    lw = {name: _LayerSlice(w, l) for name, w in mxu_weights.items()}
    lw.update({
        'norm1_g': row(norm1_g[l]), 'onorm_g': row(hgrn_onorm_g[l]), 'gavg': gavg,
        'gmlp_ln_g': row(gmlp_ln_g[l]), 'gmlp_ln_b': row(gmlp_ln_b[l]),
        'bs_tile': jnp.repeat(gmlp_b_s[l].T, B_WIDTH // B_GROUPS, axis=1).astype(F32),
        'dw_w': conv_dw_w[l].astype(F32), 'dw_b': row(conv_dw_b[l]),
        'conv_ln_g': row(conv_ln_g[l]), 'conv_ln_b': row(conv_ln_b[l]),
        'pw_b': row(conv_pw_b[l]), 'norm2_g': row(norm2_g[l]),
    })
    return lw


def kernel(x, c, ctx, c_ctx, ada_w, ada_b, norm1_g, w_in, hgrn_lb_logits, hgrn_onorm_g, gmlp_ln_g, gmlp_ln_b, gmlp_w_s, gmlp_b_s, conv_dw_w, conv_dw_b, conv_ln_g, conv_ln_b, conv_pw_w, conv_pw_b, w_out, norm2_g, ffn_w13, ffn_w2, final_norm_g):
    bsz, t_len, d = x.shape
    ctx_len = ctx.shape[1]
    depth = ada_w.shape[0]
    lb_logits = hgrn_lb_logits.astype(F32)

    cs = jnp.concatenate([c, c_ctx[None, :], jnp.zeros((MOD_ROWS - bsz - 1, d), F32)], axis=0)
    mod_all = _modulation(cs, ada_w, ada_b)
    grp = jnp.arange(B_WIDTH) // (B_WIDTH // B_GROUPS)
    gavg = ((grp[:, None] == grp[None, :]).astype(F32) / (B_WIDTH // B_GROUPS)).astype(BF16)
    zero_state = jnp.zeros((bsz, 2, A_HEADS, A_DK, A_DK), F32)
    mxu_weights = {
        'w_in': w_in.astype(BF16), 'w_out': w_out.astype(BF16),
        'ffn_w13': ffn_w13.astype(BF16), 'ffn_w2': ffn_w2.astype(BF16),
        'pw_w': conv_pw_w.astype(BF16),
        'ws_cat': gmlp_w_s.reshape(depth, B_GROUPS * B_CHUNK, B_CHUNK).astype(BF16),
    }

    for l in range(depth):
        last = l == depth - 1
        lw = _layer_params(l, gavg, mxu_weights, norm1_g, hgrn_onorm_g, gmlp_ln_g, gmlp_ln_b, gmlp_b_s,
                           conv_dw_w, conv_dw_b, conv_ln_g, conv_ln_b, conv_pw_b, norm2_g)
        mod = mod_all[l, :bsz].reshape(bsz, 6, d)
        mod_ctx = jnp.broadcast_to(mod_all[l, bsz].reshape(1, 6, d), (bsz, 6, d))
        if last:
            pa_c = _in_proj(ctx, mod_ctx, lw, ctx_len)
            _, _, state = _hgrn(pa_c, lb_logits, zero_state, l, ctx_len)
        else:
            pa_c, gate_c, ybc_c = _in_proj_mix(ctx, mod_ctx, lw, ctx_len, ctx_len)
            of_c, ob_c, state = _hgrn(pa_c, lb_logits, zero_state, l, ctx_len)
            ctx = _out_ffn(ctx, of_c, ob_c, gate_c, ybc_c, mod_ctx, lw, final_norm_g, ctx_len, False)
        pa, gate, ybc = _in_proj_mix(x, mod, lw, 512, t_len // (t_len // GRID_W))
        o_f, o_b, _ = _hgrn(pa, lb_logits, state, l, 512)
        x = _out_ffn(x, o_f, o_b, gate, ybc, mod, lw, final_norm_g, 512, last)
    return x
```

```python
import functools
from typing import NamedTuple

import jax
import jax.numpy as jnp
from jax import lax
from jax.experimental import pallas as pl
from jax.experimental.pallas import tpu as pltpu

F32 = jnp.float32
BF16 = jnp.bfloat16

D_MODEL = 1024
GRID_W = 64
A_HEADS = 4
A_DK = 128
A_WIDTH = A_HEADS * A_DK
A_CHUNK = 64
A_SUB = 32
B_GROUPS = 4
B_WIDTH = 256
B_CHUNK = 128
C_GROUPS = 4
C_WIDTH = 256
C_KERNEL = 31
C_PAD = 16
SUBLANES = 8
PROJ_CHUNK = 256
FFN_CHUNK = 256
PA_COLS = 4 * A_WIDTH
PB_COLS = 3 * A_WIDTH
EPS = 1e-6
EXP2_CLAMP = 80.0
LOG2E = 1.4426950408889634
MOD_ROWS = 8
LATENT_TILE = 512

V7X_VMEM_LIMIT = 56 * 1024 * 1024

_NT = (((1,), (1,)), ((), ()))
_TN = (((0,), (0,)), ((), ()))
_FINISHED = object()


def _cparams(sem):
    return pltpu.CompilerParams(dimension_semantics=sem, vmem_limit_bytes=V7X_VMEM_LIMIT)


def _full(shape):
    n = len(shape)
    return pl.BlockSpec(shape, lambda *_: (0,) * n)


class _LayerSlice(NamedTuple):
    stacked: jax.Array
    layer: int

    @property
    def shape(self):
        return self.stacked.shape[1:]


def _param_spec(p, block=None):
    if not isinstance(p, _LayerSlice):
        return _full(p.shape)
    block = p.shape if block is None else block
    index = (p.layer,) + (0,) * len(block)
    return pl.BlockSpec((None,) + tuple(block), lambda *_: index, pipeline_mode=pl.Buffered(1))


def _param_arg(p):
    return p.stacked if isinstance(p, _LayerSlice) else p


def _round_robin(branches, after_vector_stage=None):
    while branches:
        alive = []
        for g in branches:
            produced = next(g, _FINISHED)
            if produced is _FINISHED:
                continue
            alive.append(g)
            if produced is not None and after_vector_stage is not None:
                after_vector_stage(produced)
        branches = alive


def _order_after(x, produced):
    rows = 2 * SUBLANES
    bits = produced[:SUBLANES, :128].astype(F32).astype(jnp.int32)
    zero = lax.shift_right_logical(lax.shift_right_logical(bits, 16), 16).astype(F32).astype(x.dtype)
    zeros = jnp.tile(zero, (rows // SUBLANES, x.shape[1] // 128))
    return jnp.concatenate([x[:rows] + zeros, x[rows:]], axis=0)


def _mod_kernel(cs_ref, w_ref, b_ref, o_ref):
    s = cs_ref[...]
    s = s * jax.nn.sigmoid(s)
    w = w_ref[0]
    s_hi = s.astype(BF16)
    s_lo = (s - s_hi.astype(F32)).astype(BF16)
    w_hi = w.astype(BF16)
    w_lo = (w - w_hi.astype(F32)).astype(BF16)
    r = jnp.dot(jnp.concatenate([s_hi, s_lo], axis=0), w_hi, preferred_element_type=F32)
    o_ref[0] = (r[:MOD_ROWS] + r[MOD_ROWS:] + jnp.dot(s_hi, w_lo, preferred_element_type=F32)
                + b_ref[0])


def _modulation(cs, ada_w, ada_b):
    depth, d, n = ada_w.shape
    tn = 1024
    return pl.pallas_call(
        _mod_kernel,
        grid=(depth, n // tn),
        in_specs=[
            pl.BlockSpec((MOD_ROWS, d), lambda l, j: (0, 0)),
            pl.BlockSpec((1, d, tn), lambda l, j: (l, 0, j)),
            pl.BlockSpec((1, 1, tn), lambda l, j: (l, 0, j)),
        ],
        out_specs=pl.BlockSpec((1, MOD_ROWS, tn), lambda l, j: (l, 0, j)),
        out_shape=jax.ShapeDtypeStruct((depth, MOD_ROWS, n), F32),
        compiler_params=_cparams(("arbitrary", "arbitrary")),
        name="modulation",
    )(cs, ada_w, ada_b.reshape(depth, 1, n))


def _norm_modulate(x, g, shift, scale):
    ms = jnp.mean(x * x, axis=-1, keepdims=True)
    y = x * lax.rsqrt(ms + EPS) * g
    return y * (1.0 + scale) + shift


def _group_mean(x, gavg):
    return jnp.dot(x.astype(BF16), gavg, preferred_element_type=F32)


def _group_layer_norm_stages(x, gavg, g, b, out):
    mu = _group_mean(x, gavg)
    yield
    xc = x - mu
    var = _group_mean(xc * xc, gavg)
    yield
    out['ln'] = xc * lax.rsqrt(var + EPS) * g + b


def _gmlp_stages(project, gavg, ln_g, ln_b, ws_ref, bs_ref, out):
    u = project(0)
    v = project(B_WIDTH)
    tile = u.shape[0]
    yield
    ln = {}
    yield from _group_layer_norm_stages(jax.nn.gelu(v), gavg, ln_g, ln_b, ln)
    vn = ln['ln'].astype(BF16)
    gu = jax.nn.gelu(u)
    lane_group = lax.broadcasted_iota(jnp.int32, (B_CHUNK, B_WIDTH), 1) // (B_WIDTH // B_GROUPS)
    mixed = []
    for n in range(tile // B_CHUNK):
        yield
        r = jnp.dot(ws_ref[...], vn[n * B_CHUNK:(n + 1) * B_CHUNK], preferred_element_type=F32)
        m = bs_ref[...]
        for g in range(B_GROUPS):
            m = m + jnp.where(lane_group == g, r[g * B_CHUNK:(g + 1) * B_CHUNK], 0.0)
        mixed.append(m)
    out['yb'] = gu * jnp.concatenate(mixed, axis=0)


def _conv_stages(ca, cgate, gavg, dww_ref, dw_b, ln_g, ln_b, pww_ref, pw_b, zpad_ref, zsh_ref, seg, out):
    tile = ca.shape[0]
    z = ca * jax.nn.sigmoid(cgate)
    nseg = tile // seg
    pad_zeros = jnp.zeros((C_PAD, C_WIDTH), F32)
    for s in range(nseg):
        zpad_ref[s, 0:C_PAD, :] = pad_zeros
        zpad_ref[s, C_PAD + seg:2 * C_PAD + seg, :] = pad_zeros
        zpad_ref[s, C_PAD:C_PAD + seg, :] = z[s * seg:(s + 1) * seg]
    span = seg + 2 * C_PAD - SUBLANES
    convs = []
    for s in range(nseg):
        for r in range(1, SUBLANES):
            zsh_ref[s, r] = zpad_ref[s, r:r + span, :]
        acc = jnp.zeros((seg, C_WIDTH), F32)
        for j in range(C_KERNEL):
            start = C_PAD - C_KERNEL // 2 + j
            r, a = start % SUBLANES, start - start % SUBLANES
            tap = zpad_ref[s, a:a + seg, :] if r == 0 else zsh_ref[s, r, a:a + seg, :]
            acc = acc + dww_ref[j:j + 1, :] * tap
        convs.append(acc)
        yield acc
    ln = {}
    yield from _group_layer_norm_stages(jnp.concatenate(convs, axis=0) + dw_b, gavg, ln_g, ln_b, ln)
    zc = ln['ln']
    zc = (zc * jax.nn.sigmoid(zc)).astype(BF16)
    out['yc'] = jnp.dot(zc, pww_ref[...], preferred_element_type=F32) + pw_b


_MIX_PARAMS = ('norm1_g', 'w_in', 'gavg', 'gmlp_ln_g', 'gmlp_ln_b', 'ws_cat', 'bs_tile',
               'dw_w', 'dw_b', 'conv_ln_g', 'conv_ln_b', 'pw_w', 'pw_b')


def _project_and_mix(x_ref, mod_ref, p, ybc_ref, zpad_ref, zsh_ref, seg, sinks, extra_branches=()):
    h = _norm_modulate(x_ref[0], p['norm1_g'][...], mod_ref[0, 0:1, :], mod_ref[0, 1:2, :]).astype(BF16)
    gavg = p['gavg'][...]

    def proj(lo, n, lhs=h):
        return jnp.dot(lhs, p['w_in'][:, lo:lo + n], preferred_element_type=F32)

    pending = iter(sinks)

    def project_next(produced=None):
        nxt = next(pending, None)
        if nxt is not None:
            lo, store = nxt
            store(proj(lo, PROJ_CHUNK, h if produced is None else _order_after(h, produced)))

    b0 = PA_COLS + A_WIDTH
    c0 = b0 + 2 * B_WIDTH
    out = {}
    branches = [
        _conv_stages(proj(c0, C_WIDTH), proj(c0 + C_WIDTH, C_WIDTH), gavg, p['dw_w'], p['dw_b'][...],
                     p['conv_ln_g'][...], p['conv_ln_b'][...], p['pw_w'], p['pw_b'][...],
                     zpad_ref, zsh_ref, seg, out),
        _gmlp_stages(lambda lo: proj(b0 + lo, B_WIDTH), gavg, p['gmlp_ln_g'][...], p['gmlp_ln_b'][...],
                     p['ws_cat'], p['bs_tile'], out),
    ]
    _round_robin(branches + list(extra_branches), project_next)
    ybc_ref[0] = jnp.concatenate([out['yb'], out['yc']], axis=1).astype(BF16)
    for _ in sinks:
        project_next()


def _column_sinks(ranges):
    sinks = []
    for w_lo, width, dests in ranges:
        for off in range(0, width, PROJ_CHUNK):
            def store(val, dests=dests, off=off):
                for write, lo in dests:
                    write(lo + off, val)
            sinks.append((w_lo + off, store))
    return sinks


def _in_proj_kernel(x_ref, mod_ref, g_ref, w_ref, pa_ref):
    h = _norm_modulate(x_ref[0], g_ref[...], mod_ref[0, 0:1, :], mod_ref[0, 1:2, :]).astype(BF16)
    pa_ref[0] = jnp.dot(h, w_ref[...], preferred_element_type=F32)


def _in_proj(x, mod, lw, tile):
    bsz, t_len, d = x.shape
    return pl.pallas_call(
        _in_proj_kernel,
        grid=(bsz, t_len // tile),
        in_specs=[
            pl.BlockSpec((1, tile, d), lambda b, i: (b, i, 0)),
            pl.BlockSpec((1, 6, d), lambda b, i: (b, 0, 0)),
            _full((1, d)),
            _param_spec(lw['w_in'], (d, PA_COLS)),
        ],
        out_specs=pl.BlockSpec((1, tile, PA_COLS), lambda b, i: (b, i, 0)),
        out_shape=jax.ShapeDtypeStruct((bsz, t_len, PA_COLS), F32),
        compiler_params=_cparams(("arbitrary", "arbitrary")),
        name="in_proj",
    )(x, mod, lw['norm1_g'], _param_arg(lw['w_in']))


def _ref_writer(ref):
    def write(lo, val):
        ref[0, :, lo:lo + val.shape[1]] = val
    return write


def _in_proj_mix_kernel(x_ref, mod_ref, *refs, seg):
    p = dict(zip(_MIX_PARAMS, refs))
    pa_ref, gate_ref, ybc_ref, zpad_ref, zsh_ref = refs[len(_MIX_PARAMS):]
    sinks = _column_sinks([(0, PA_COLS, [(_ref_writer(pa_ref), 0)]),
                           (PA_COLS, A_WIDTH, [(_ref_writer(gate_ref), 0)])])
    _project_and_mix(x_ref, mod_ref, p, ybc_ref, zpad_ref, zsh_ref, seg, sinks)


def _conv_scratch(tile, seg):
    nseg = tile // seg
    return [pltpu.VMEM((nseg, seg + 2 * C_PAD, C_WIDTH), F32),
            pltpu.VMEM((nseg, SUBLANES, seg + 2 * C_PAD - SUBLANES, C_WIDTH), F32)]


def _in_proj_mix(x, mod, lw, tile, seg):
    bsz, t_len, d = x.shape

    def tok(cols):
        return pl.BlockSpec((1, tile, cols), lambda b, i: (b, i, 0))

    params = [lw[name] for name in _MIX_PARAMS]
    return pl.pallas_call(
        functools.partial(_in_proj_mix_kernel, seg=seg),
        grid=(bsz, t_len // tile),
        in_specs=[tok(d), pl.BlockSpec((1, 6, d), lambda b, i: (b, 0, 0))] + [_param_spec(p) for p in params],
        out_specs=[tok(PA_COLS), tok(A_WIDTH), tok(B_WIDTH + C_WIDTH)],
        out_shape=[jax.ShapeDtypeStruct((bsz, t_len, PA_COLS), F32),
                   jax.ShapeDtypeStruct((bsz, t_len, A_WIDTH), F32),
                   jax.ShapeDtypeStruct((bsz, t_len, B_WIDTH + C_WIDTH), BF16)],
        scratch_shapes=_conv_scratch(tile, seg),
        compiler_params=_cparams(("arbitrary", "arbitrary")),
        name="in_proj_mix",
    )(x, mod, *[_param_arg(p) for p in params])


def _lower_bounds(logits, layer):
    depth = logits.shape[0]
    rows = [logits[j] for j in range(depth)]
    m = rows[0]
    for r in rows[1:]:
        m = jnp.maximum(m, r)
    es = [jnp.exp(r - m) for r in rows]
    tot = es[0]
    for e in es[1:]:
        tot = tot + e
    lb = jnp.zeros_like(m)
    for j in range(1, layer + 1):
        lb = lb + es[j] / tot
    return lb


def _chunk_rows(c, lo=0, n=A_CHUNK):
    return slice(c * A_CHUNK + lo, c * A_CHUNK + lo + n)


def _hgrn_factor_stages(q, z, v, lb, rev, n_chunks):
    width = q.shape[1]
    sub = A_SUB
    row = lax.broadcasted_iota(jnp.int32, (A_CHUNK, A_CHUNK), 0)
    col = lax.broadcasted_iota(jnp.int32, (A_CHUNK, A_CHUNK), 1)
    tri = ((col >= row) if rev else (col <= row)).astype(F32).astype(BF16)
    oml = 1.0 - lb
    sg = jax.nn.sigmoid(z)
    lf = jnp.log(lb + oml * sg)
    kk = oml * (1.0 - sg)
    hi = lf.astype(BF16)
    lo = (lf - hi.astype(F32)).astype(BF16)
    yield lo
    bs = [(jnp.dot(tri, hi[_chunk_rows(c)], preferred_element_type=F32)
           + jnp.dot(tri, lo[_chunk_rows(c)], preferred_element_type=F32)) * LOG2E
          for c in range(n_chunks)]
    yield
    wide, narrow = (0, 1) if rev else (1, 0)
    mid_row = sub // 2 if rev else sub // 2 - 1
    last_row = 0 if rev else A_CHUNK - 1
    b_last, b_own, b_wide, b_narrow, decs = [], [], [], [], []
    for bc in bs:
        mids = [bc[i * sub + mid_row:i * sub + mid_row + 1] for i in range(2)]
        b_last.append(jnp.broadcast_to(bc[last_row:last_row + 1], (A_CHUNK, width)))
        b_own += [jnp.broadcast_to(m, (sub, width)) for m in mids]
        b_wide.append(jnp.broadcast_to(mids[wide], (A_CHUNK, width)))
        b_narrow.append(mids[narrow] - bc[narrow * sub:(narrow + 1) * sub])
        decs.append(jnp.exp2(bc[last_row:last_row + 1]))
    b = jnp.concatenate(bs, axis=0)
    f = {'dec': decs, 'v': v.astype(BF16)}
    f['qe'] = (q * jnp.exp2(b)).astype(BF16)
    f['kd'] = (kk * jnp.exp2(jnp.concatenate(b_last, axis=0) - b)).astype(BF16)
    yield f['kd']
    f['qd'] = (q * jnp.exp2(jnp.minimum(b - jnp.concatenate(b_own, axis=0), EXP2_CLAMP))).astype(BF16)
    yield f['qd']
    kk_narrow = jnp.concatenate([kk[_chunk_rows(c, narrow * sub, sub)] for c in range(n_chunks)], axis=0)
    f['k_wide'] = (kk * jnp.exp2(jnp.minimum(jnp.concatenate(b_wide, axis=0) - b, EXP2_CLAMP))).astype(BF16)
    f['k_narrow'] = (kk_narrow
                     * jnp.exp2(jnp.minimum(jnp.concatenate(b_narrow, axis=0), EXP2_CLAMP))).astype(BF16)
    return f


def _hgrn_scores(f, rev, c):
    sub = A_SUB
    wide, narrow = (0, 1) if rev else (1, 0)
    row_n = lax.broadcasted_iota(jnp.int32, (sub, sub), 0)
    col_n = lax.broadcasted_iota(jnp.int32, (sub, sub), 1)
    row_w = lax.broadcasted_iota(jnp.int32, (sub, A_CHUNK), 0)
    col_w = lax.broadcasted_iota(jnp.int32, (sub, A_CHUNK), 1)
    mask_n = (col_n >= row_n) if rev else (col_n <= row_n)
    mask_w = (col_w >= row_w) if rev else (col_w <= row_w + sub)
    per_head = []
    for h in range(A_HEADS):
        hc = slice(h * A_DK, (h + 1) * A_DK)
        s_n = lax.dot_general(f['qd'][_chunk_rows(c, narrow * sub, sub), hc],
                              f['k_narrow'][c * sub:(c + 1) * sub, hc], _NT, preferred_element_type=F32)
        s_w = lax.dot_general(f['qd'][_chunk_rows(c, wide * sub, sub), hc],
                              f['k_wide'][_chunk_rows(c), hc], _NT, preferred_element_type=F32)
        per_head.append((jnp.where(mask_n, s_n, 0.0).astype(BF16),
                         jnp.where(mask_w, s_w, 0.0).astype(BF16)))
    return per_head


def _hgrn_intra(f, scores, rev, c):
    sub = A_SUB
    narrow = 1 if rev else 0
    per_head = []
    for h in range(A_HEADS):
        hc = slice(h * A_DK, (h + 1) * A_DK)
        s_n, s_w = scores[h]
        o_n = jnp.dot(s_n, f['v'][_chunk_rows(c, narrow * sub, sub), hc], preferred_element_type=F32)
        o_w = jnp.dot(s_w, f['v'][_chunk_rows(c), hc], preferred_element_type=F32)
        intra = jnp.concatenate([o_w, o_n] if rev else [o_n, o_w], axis=0)
        upd = lax.dot_general(f['v'][_chunk_rows(c), hc], f['kd'][_chunk_rows(c), hc], _TN,
                              preferred_element_type=F32)
        per_head.append((intra, upd))
    return per_head


def _hgrn_stages(q, z, v, lb, rev, n_chunks, get_state, set_state, put_out, stride=2):
    f = yield from _hgrn_factor_stages(q, z, v, lb, rev, n_chunks)
    scores, intra = [], []
    for c in range(n_chunks):
        scores.append(_hgrn_scores(f, rev, c))
        if c % stride == stride - 1:
            yield
    for c in range(n_chunks):
        intra.append(_hgrn_intra(f, scores[c], rev, c))
        if c % stride == stride - 1:
            yield
    states = [get_state(h) for h in range(A_HEADS)]
    for n, c in enumerate(range(n_chunks - 1, -1, -1) if rev else range(n_chunks)):
        outs = []
        for h in range(A_HEADS):
            hc = slice(h * A_DK, (h + 1) * A_DK)
            inter = lax.dot_general(f['qe'][_chunk_rows(c), hc], states[h].astype(BF16), _NT,
                                    preferred_element_type=F32)
            outs.append(inter + intra[c][h][0])
            states[h] = states[h] * f['dec'][c][:, hc] + intra[c][h][1]
        put_out(c, jnp.concatenate(outs, axis=1))
        if n % stride == stride - 1:
            yield
    for h in range(A_HEADS):
        set_state(h, states[h])


def _chunk_writer(ref):
    def put(c, val):
        ref[0, _chunk_rows(c), :] = val
    return put


def _hgrn_kernel(qf_ref, zf_ref, vf_ref, qb_ref, zb_ref, vb_ref, lbl_ref, s0_ref,
                 of_ref, ob_ref, st_ref, *, layer, n_chunks):
    @pl.when(pl.program_id(1) == 0)
    def _():
        st_ref[...] = s0_ref[...]

    lb = _lower_bounds(lbl_ref[...], layer)

    def direction(d, q_ref, z_ref, v_ref, o_ref):
        def set_state(h, val):
            st_ref[0, d, h] = val
        return _hgrn_stages(q_ref[0], z_ref[0], v_ref[0], lb[d:d + 1], d == 1, n_chunks,
                            lambda h: st_ref[0, d, h], set_state, _chunk_writer(o_ref))

    _round_robin([direction(0, qf_ref, zf_ref, vf_ref, of_ref),
                  direction(1, qb_ref, zb_ref, vb_ref, ob_ref)])


def _hgrn(pa, lb_logits, s0, layer, tile):
    bsz, t_len, _ = pa.shape
    nb = t_len // tile
    depth = lb_logits.shape[0]

    def col_spec(colblk, rev):
        if rev:
            return pl.BlockSpec((1, tile, A_WIDTH), lambda b, i: (b, nb - 1 - i, colblk))
        return pl.BlockSpec((1, tile, A_WIDTH), lambda b, i: (b, i, colblk))

    st_spec = pl.BlockSpec((1, 2, A_HEADS, A_DK, A_DK), lambda b, i: (b, 0, 0, 0, 0))
    o_sds = jax.ShapeDtypeStruct((bsz, t_len, A_WIDTH), F32)
    return pl.pallas_call(
        functools.partial(_hgrn_kernel, layer=layer, n_chunks=tile // A_CHUNK),
        grid=(bsz, nb),
        in_specs=[col_spec(0, False), col_spec(1, False), col_spec(3, False),
                  col_spec(0, True), col_spec(2, True), col_spec(3, True),
                  _full((depth, 2, A_WIDTH)), st_spec],
        out_specs=[col_spec(0, False), col_spec(0, True), st_spec],
        out_shape=[o_sds, o_sds, jax.ShapeDtypeStruct(s0.shape, F32)],
        compiler_params=_cparams(("arbitrary", "arbitrary")),
        name="hgrn",
    )(pa, pa, pa, pa, pa, pa, lb_logits, s0)


def _hgrn_bwd_kernel(q_ref, z_ref, v_ref, lbl_ref, s0_ref, o_ref, st_ref, *, layer, n_chunks):
    @pl.when(pl.program_id(1) == 0)
    def _():
        st_ref[...] = s0_ref[0, 0]

    lb = _lower_bounds(lbl_ref[...], layer)

    def set_state(h, val):
        st_ref[h] = val

    for _ in _hgrn_stages(q_ref[0], z_ref[0], v_ref[0], lb[1:2], True, n_chunks,
                          lambda h: st_ref[h], set_state, _chunk_writer(o_ref)):
        pass


def _hgrn_bwd(pb, lb_logits, s0, layer, tile):
    bsz, t_len, _ = pb.shape
    nb = t_len // tile
    depth = lb_logits.shape[0]

    def col_spec(colblk):
        return pl.BlockSpec((1, tile, A_WIDTH), lambda b, i: (b, nb - 1 - i, colblk))

    return pl.pallas_call(
        functools.partial(_hgrn_bwd_kernel, layer=layer, n_chunks=tile // A_CHUNK),
        grid=(bsz, nb),
        in_specs=[col_spec(0), col_spec(1), col_spec(2), _full((depth, 2, A_WIDTH)),
                  pl.BlockSpec((1, 1, A_HEADS, A_DK, A_DK), lambda b, i: (b, 1, 0, 0, 0))],
        out_specs=col_spec(0),
        out_shape=jax.ShapeDtypeStruct((bsz, t_len, A_WIDTH), F32),
        scratch_shapes=[pltpu.VMEM((A_HEADS, A_DK, A_DK), F32)],
        compiler_params=_cparams(("arbitrary", "arbitrary")),
        name="hgrn_bwd",
    )(pb, pb, pb, lb_logits, s0)


def _proj_fwd_kernel(x_ref, mod_ref, lbl_ref, s0_ref, *refs, seg, layer, tiles_per_seq):
    p = dict(zip(_MIX_PARAMS, refs))
    pb_ref, gate_ref, ybc_ref, of_ref, zpad_ref, zsh_ref, pf_ref, st_ref = refs[len(_MIX_PARAMS):]
    t = pl.program_id(0)
    tile = x_ref.shape[1]

    @pl.when(t == 0)
    def _():
        pf_ref[...] = jnp.zeros(pf_ref.shape, F32)
        st_ref[...] = jnp.zeros(st_ref.shape, F32)

    @pl.when(lax.rem(t - 1, tiles_per_seq) == 0)
    def _():
        st_ref[...] = s0_ref[0, 0]

    def scratch_writer(k):
        def write(lo, val):
            pf_ref[k, :, lo:lo + val.shape[1]] = val
        return write

    pb_write = _ref_writer(pb_ref)
    sinks = _column_sinks([
        (0, A_WIDTH, [(scratch_writer(0), 0), (pb_write, 0)]),
        (A_WIDTH, A_WIDTH, [(scratch_writer(1), 0)]),
        (2 * A_WIDTH, A_WIDTH, [(pb_write, A_WIDTH)]),
        (3 * A_WIDTH, A_WIDTH, [(scratch_writer(2), 0), (pb_write, 2 * A_WIDTH)]),
        (PA_COLS, A_WIDTH, [(_ref_writer(gate_ref), 0)]),
    ])
    lb = _lower_bounds(lbl_ref[...], layer)

    def set_state(h, val):
        st_ref[h] = val

    fwd_scan = _hgrn_stages(pf_ref[0], pf_ref[1], pf_ref[2], lb[0:1], False,
                            tile // A_CHUNK, lambda h: st_ref[h], set_state, _chunk_writer(of_ref))
    _project_and_mix(x_ref, mod_ref, p, ybc_ref, zpad_ref, zsh_ref, seg, sinks, [fwd_scan])


def _proj_fwd(x, mod, lw, lb_logits, s0, layer, tile, seg):
    bsz, t_len, d = x.shape
    nb = t_len // tile
    n_tiles = bsz * nb
    depth = lb_logits.shape[0]

    def fill_tile(t):
        return jnp.minimum(t, n_tiles - 1)

    def scan_tile(t):
        return jnp.maximum(t - 1, 0)

    def tok(cols, which):
        return pl.BlockSpec((1, tile, cols), lambda t: (which(t) // nb, which(t) % nb, 0))

    params = [lw[name] for name in _MIX_PARAMS]
    return pl.pallas_call(
        functools.partial(_proj_fwd_kernel, seg=seg, layer=layer, tiles_per_seq=nb),
        grid=(n_tiles + 1,),
        in_specs=[tok(d, fill_tile),
                  pl.BlockSpec((1, 6, d), lambda t: (fill_tile(t) // nb, 0, 0)),
                  _full((depth, 2, A_WIDTH)),
                  pl.BlockSpec((1, 1, A_HEADS, A_DK, A_DK), lambda t: (scan_tile(t) // nb, 0, 0, 0, 0))]
                 + [_param_spec(p) for p in params],
        out_specs=[tok(PB_COLS, fill_tile), tok(A_WIDTH, fill_tile), tok(B_WIDTH + C_WIDTH, fill_tile),
                   tok(A_WIDTH, scan_tile)],
        out_shape=[jax.ShapeDtypeStruct((bsz, t_len, PB_COLS), F32),
                   jax.ShapeDtypeStruct((bsz, t_len, A_WIDTH), F32),
                   jax.ShapeDtypeStruct((bsz, t_len, B_WIDTH + C_WIDTH), BF16),
                   jax.ShapeDtypeStruct((bsz, t_len, A_WIDTH), F32)],
        scratch_shapes=_conv_scratch(tile, seg) + [pltpu.VMEM((3, tile, A_WIDTH), F32),
                                                   pltpu.VMEM((A_HEADS, A_DK, A_DK), F32)],
        compiler_params=_cparams(("arbitrary",)),
        name="proj_fwd",
    )(x, mod, lb_logits, s0, *[_param_arg(p) for p in params])


def _out_ffn_kernel(x_ref, of_ref, ob_ref, gate_ref, ybc_ref, mod_ref, og_ref, wo_ref,
                    g_ref, w13_ref, w2_ref, fg_ref, o_ref, *, d_ff, chunk, final_norm):
    o = of_ref[0] + ob_ref[0]
    parts = []
    for h in range(A_HEADS):
        oh = o[:, h * A_DK:(h + 1) * A_DK]
        parts.append(oh * lax.rsqrt(jnp.mean(oh * oh, axis=-1, keepdims=True) + EPS))
    gate = gate_ref[0]
    ya = jnp.concatenate(parts, axis=1) * og_ref[...] * (gate * jax.nn.sigmoid(gate))
    y = (jnp.dot(ya.astype(BF16), wo_ref[0:A_WIDTH, :], preferred_element_type=F32)
         + jnp.dot(ybc_ref[0], wo_ref[A_WIDTH:, :], preferred_element_type=F32))
    x = x_ref[0] + mod_ref[0, 2:3, :] * y
    h = _norm_modulate(x, g_ref[...], mod_ref[0, 3:4, :], mod_ref[0, 4:5, :]).astype(BF16)
    acts = []
    for c in range(d_ff // chunk):
        a = jnp.dot(h, w13_ref[:, c * chunk:(c + 1) * chunk], preferred_element_type=F32)
        b = jnp.dot(h, w13_ref[:, d_ff + c * chunk:d_ff + (c + 1) * chunk], preferred_element_type=F32)
        acts.append((a * jax.nn.sigmoid(a) * b).astype(BF16))
    acc = jnp.dot(jnp.concatenate(acts, axis=1), w2_ref[...], preferred_element_type=F32)
    y = x + mod_ref[0, 5:6, :] * acc
    if final_norm:
        y = y * lax.rsqrt(jnp.mean(y * y, axis=-1, keepdims=True) + EPS) * fg_ref[...]
    o_ref[0] = y


def _out_ffn(x, o_f, o_b, gate, ybc, mod, lw, final_g, tile, final_norm):
    bsz, t_len, d = x.shape
    d_ff = lw['ffn_w2'].shape[0]

    def tok(cols):
        return pl.BlockSpec((1, tile, cols), lambda b, i: (b, i, 0))

    params = [lw['onorm_g'], lw['w_out'], lw['norm2_g'], lw['ffn_w13'], lw['ffn_w2'], final_g.reshape(1, d)]
    return pl.pallas_call(
        functools.partial(_out_ffn_kernel, d_ff=d_ff, chunk=FFN_CHUNK, final_norm=final_norm),
        grid=(bsz, t_len // tile),
        in_specs=[tok(d), tok(A_WIDTH), tok(A_WIDTH), tok(A_WIDTH), tok(B_WIDTH + C_WIDTH),
                  pl.BlockSpec((1, 6, d), lambda b, i: (b, 0, 0))] + [_param_spec(p) for p in params],
        out_specs=tok(d),
        out_shape=jax.ShapeDtypeStruct(x.shape, F32),
        compiler_params=_cparams(("arbitrary", "arbitrary")),
        name="out_ffn",
    )(x, o_f, o_b, gate, ybc, mod, *[_param_arg(p) for p in params])


def _layer_params(l, gavg, mxu_weights, norm1_g, hgrn_onorm_g, gmlp_ln_g, gmlp_ln_b, gmlp_b_s,
                  conv_dw_w, conv_dw_b, conv_ln_g, conv_ln_b, conv_pw_b, norm2_g):
    row = lambda a: a.reshape(1, -1).astype(F32)
    lw = {name: _LayerSlice(w, l) for name, w in mxu_weights.items()}
    lw.update({
        'norm1_g': row(norm1_g[l]), 'onorm_g': row(hgrn_onorm_g[l]), 'gavg': gavg,
        'gmlp_ln_g': row(gmlp_ln_g[l]), 'gmlp_ln_b': row(gmlp_ln_b[l]),
        'bs_tile': jnp.repeat(gmlp_b_s[l].T, B_WIDTH // B_GROUPS, axis=1).astype(F32),
        'dw_w': conv_dw_w[l].astype(F32), 'dw_b': row(conv_dw_b[l]),
        'conv_ln_g': row(conv_ln_g[l]), 'conv_ln_b': row(conv_ln_b[l]),
        'pw_b': row(conv_pw_b[l]), 'norm2_g': row(norm2_g[l]),
    })
    return lw


def kernel(x, c, ctx, c_ctx, ada_w, ada_b, norm1_g, w_in, hgrn_lb_logits, hgrn_onorm_g, gmlp_ln_g, gmlp_ln_b, gmlp_w_s, gmlp_b_s, conv_dw_w, conv_dw_b, conv_ln_g, conv_ln_b, conv_pw_w, conv_pw_b, w_out, norm2_g, ffn_w13, ffn_w2, final_norm_g):
    bsz, t_len, d = x.shape
    ctx_len = ctx.shape[1]
    depth = ada_w.shape[0]
    lb_logits = hgrn_lb_logits.astype(F32)

    cs = jnp.concatenate([c, c_ctx[None, :], jnp.zeros((MOD_ROWS - bsz - 1, d), F32)], axis=0)
    mod_all = _modulation(cs, ada_w, ada_b)
    grp = jnp.arange(B_WIDTH) // (B_WIDTH // B_GROUPS)
    gavg = ((grp[:, None] == grp[None, :]).astype(F32) / (B_WIDTH // B_GROUPS)).astype(BF16)
    zero_state = jnp.zeros((bsz, 2, A_HEADS, A_DK, A_DK), F32)
    mxu_weights = {
        'w_in': w_in.astype(BF16), 'w_out': w_out.astype(BF16),
        'ffn_w13': ffn_w13.astype(BF16), 'ffn_w2': ffn_w2.astype(BF16),
        'pw_w': conv_pw_w.astype(BF16),
        'ws_cat': gmlp_w_s.reshape(depth, B_GROUPS * B_CHUNK, B_CHUNK).astype(BF16),
    }
    row_len = t_len // (t_len // GRID_W)

    for l in range(depth):
        last = l == depth - 1
        lw = _layer_params(l, gavg, mxu_weights, norm1_g, hgrn_onorm_g, gmlp_ln_g, gmlp_ln_b, gmlp_b_s,
                           conv_dw_w, conv_dw_b, conv_ln_g, conv_ln_b, conv_pw_b, norm2_g)
        mod = mod_all[l, :bsz].reshape(bsz, 6, d)
        mod_ctx = jnp.broadcast_to(mod_all[l, bsz].reshape(1, 6, d), (bsz, 6, d))
        if last:
            pa_c = _in_proj(ctx, mod_ctx, lw, ctx_len)
            _, _, state = _hgrn(pa_c, lb_logits, zero_state, l, ctx_len)
        else:
            pa_c, gate_c, ybc_c = _in_proj_mix(ctx, mod_ctx, lw, ctx_len, ctx_len)
            of_c, ob_c, state = _hgrn(pa_c, lb_logits, zero_state, l, ctx_len)
            ctx = _out_ffn(ctx, of_c, ob_c, gate_c, ybc_c, mod_ctx, lw, final_norm_g, ctx_len, False)
        pb, gate, ybc, o_f = _proj_fwd(x, mod, lw, lb_logits, state, l, LATENT_TILE, row_len)
        o_b = _hgrn_bwd(pb, lb_logits, state, l, LATENT_TILE)
        x = _out_ffn(x, o_f, o_b, gate, ybc, mod, lw, final_norm_g, LATENT_TILE, last)
    return x
```

```python
import functools
from typing import NamedTuple

import jax
import jax.numpy as jnp
from jax import lax
from jax.experimental import pallas as pl
from jax.experimental.pallas import tpu as pltpu

F32 = jnp.float32
BF16 = jnp.bfloat16

D_MODEL = 1024
GRID_W = 64
A_HEADS = 4
A_DK = 128
A_WIDTH = A_HEADS * A_DK
A_CHUNK = 64
A_SUB = 32
B_GROUPS = 4
B_WIDTH = 256
B_CHUNK = 128
C_GROUPS = 4
C_WIDTH = 256
C_KERNEL = 31
C_PAD = 16
SUBLANES = 8
PROJ_CHUNK = 256
FFN_CHUNK = 256
PA_COLS = 4 * A_WIDTH
EPS = 1e-6
EXP2_CLAMP = 80.0
LOG2E = 1.4426950408889634
MOD_ROWS = 8
LATENT_TILE = 512

V7X_VMEM_LIMIT = 56 * 1024 * 1024

_NT = (((1,), (1,)), ((), ()))
_TN = (((0,), (0,)), ((), ()))
_FINISHED = object()


def _cparams(sem):
    return pltpu.CompilerParams(dimension_semantics=sem, vmem_limit_bytes=V7X_VMEM_LIMIT)


def _full(shape):
    n = len(shape)
    return pl.BlockSpec(shape, lambda *_: (0,) * n)


class _LayerSlice(NamedTuple):
    stacked: jax.Array
    layer: int

    @property
    def shape(self):
        return self.stacked.shape[1:]


def _param_spec(p, block=None):
    if not isinstance(p, _LayerSlice):
        return _full(p.shape)
    block = p.shape if block is None else block
    index = (p.layer,) + (0,) * len(block)
    return pl.BlockSpec((None,) + tuple(block), lambda *_: index, pipeline_mode=pl.Buffered(1))


def _param_arg(p):
    return p.stacked if isinstance(p, _LayerSlice) else p


def _round_robin(branches, between=None):
    while branches:
        branches = [g for g in branches if next(g, _FINISHED) is not _FINISHED]
        if between is not None:
            between()


def _mod_kernel(cs_ref, w_ref, b_ref, o_ref):
    s = cs_ref[...]
    s = s * jax.nn.sigmoid(s)
    w = w_ref[0]
    s_hi = s.astype(BF16)
    s_lo = (s - s_hi.astype(F32)).astype(BF16)
    w_hi = w.astype(BF16)
    w_lo = (w - w_hi.astype(F32)).astype(BF16)
    r = jnp.dot(jnp.concatenate([s_hi, s_lo], axis=0), w_hi, preferred_element_type=F32)
    o_ref[0] = (r[:MOD_ROWS] + r[MOD_ROWS:] + jnp.dot(s_hi, w_lo, preferred_element_type=F32)
                + b_ref[0])


def _modulation(cs, ada_w, ada_b):
    depth, d, n = ada_w.shape
    tn = 1024
    return pl.pallas_call(
        _mod_kernel,
        grid=(depth, n // tn),
        in_specs=[
            pl.BlockSpec((MOD_ROWS, d), lambda l, j: (0, 0)),
            pl.BlockSpec((1, d, tn), lambda l, j: (l, 0, j)),
            pl.BlockSpec((1, 1, tn), lambda l, j: (l, 0, j)),
        ],
        out_specs=pl.BlockSpec((1, MOD_ROWS, tn), lambda l, j: (l, 0, j)),
        out_shape=jax.ShapeDtypeStruct((depth, MOD_ROWS, n), F32),
        compiler_params=_cparams(("arbitrary", "arbitrary")),
        name="modulation",
    )(cs, ada_w, ada_b.reshape(depth, 1, n))


def _norm_modulate(x, g, shift, scale):
    ms = jnp.mean(x * x, axis=-1, keepdims=True)
    y = x * lax.rsqrt(ms + EPS) * g
    return y * (1.0 + scale) + shift


def _group_mean(x, gavg):
    return jnp.dot(x.astype(BF16), gavg, preferred_element_type=F32)


def _group_layer_norm_stages(x, gavg, g, b, out):
    mu = _group_mean(x, gavg)
    yield
    xc = x - mu
    var = _group_mean(xc * xc, gavg)
    yield
    out['ln'] = xc * lax.rsqrt(var + EPS) * g + b


def _gmlp_stages(project, gavg, ln_g, ln_b, ws_ref, bs_ref, out):
    u = project(0)
    v = project(B_WIDTH)
    tile = u.shape[0]
    yield
    ln = {}
    yield from _group_layer_norm_stages(jax.nn.gelu(v), gavg, ln_g, ln_b, ln)
    vn = ln['ln'].astype(BF16)
    gu = jax.nn.gelu(u)
    lane_group = lax.broadcasted_iota(jnp.int32, (B_CHUNK, B_WIDTH), 1) // (B_WIDTH // B_GROUPS)
    mixed = []
    for n in range(tile // B_CHUNK):
        yield
        r = jnp.dot(ws_ref[...], vn[n * B_CHUNK:(n + 1) * B_CHUNK], preferred_element_type=F32)
        m = bs_ref[...]
        for g in range(B_GROUPS):
            m = m + jnp.where(lane_group == g, r[g * B_CHUNK:(g + 1) * B_CHUNK], 0.0)
        mixed.append(m)
    out['yb'] = gu * jnp.concatenate(mixed, axis=0)


def _conv_stages(ca, cgate, gavg, dww_ref, dw_b, ln_g, ln_b, pww_ref, pw_b, zpad_ref, zsh_ref, seg, out):
    tile = ca.shape[0]
    z = ca * jax.nn.sigmoid(cgate)
    nseg = tile // seg
    pad_zeros = jnp.zeros((C_PAD, C_WIDTH), F32)
    for s in range(nseg):
        zpad_ref[s, 0:C_PAD, :] = pad_zeros
        zpad_ref[s, C_PAD + seg:2 * C_PAD + seg, :] = pad_zeros
        zpad_ref[s, C_PAD:C_PAD + seg, :] = z[s * seg:(s + 1) * seg]
    span = seg + 2 * C_PAD - SUBLANES
    convs = []
    for s in range(nseg):
        for r in range(1, SUBLANES):
            zsh_ref[s, r] = zpad_ref[s, r:r + span, :]
        acc = jnp.zeros((seg, C_WIDTH), F32)
        for j in range(C_KERNEL):
            start = C_PAD - C_KERNEL // 2 + j
            r, a = start % SUBLANES, start - start % SUBLANES
            tap = zpad_ref[s, a:a + seg, :] if r == 0 else zsh_ref[s, r, a:a + seg, :]
            acc = acc + dww_ref[j:j + 1, :] * tap
        convs.append(acc)
        yield
    ln = {}
    yield from _group_layer_norm_stages(jnp.concatenate(convs, axis=0) + dw_b, gavg, ln_g, ln_b, ln)
    zc = ln['ln']
    zc = (zc * jax.nn.sigmoid(zc)).astype(BF16)
    out['yc'] = jnp.dot(zc, pww_ref[...], preferred_element_type=F32) + pw_b


_MIX_PARAMS = ('norm1_g', 'w_in', 'gavg', 'gmlp_ln_g', 'gmlp_ln_b', 'ws_cat', 'bs_tile',
               'dw_w', 'dw_b', 'conv_ln_g', 'conv_ln_b', 'pw_w', 'pw_b')


def _in_proj_kernel(x_ref, mod_ref, g_ref, w_ref, pa_ref):
    h = _norm_modulate(x_ref[0], g_ref[...], mod_ref[0, 0:1, :], mod_ref[0, 1:2, :]).astype(BF16)
    pa_ref[0] = jnp.dot(h, w_ref[...], preferred_element_type=F32)


def _in_proj(x, mod, lw, tile):
    bsz, t_len, d = x.shape
    return pl.pallas_call(
        _in_proj_kernel,
        grid=(bsz, t_len // tile),
        in_specs=[
            pl.BlockSpec((1, tile, d), lambda b, i: (b, i, 0)),
            pl.BlockSpec((1, 6, d), lambda b, i: (b, 0, 0)),
            _full((1, d)),
            _param_spec(lw['w_in'], (d, PA_COLS)),
        ],
        out_specs=pl.BlockSpec((1, tile, PA_COLS), lambda b, i: (b, i, 0)),
        out_shape=jax.ShapeDtypeStruct((bsz, t_len, PA_COLS), F32),
        compiler_params=_cparams(("arbitrary", "arbitrary")),
        name="in_proj",
    )(x, mod, lw['norm1_g'], _param_arg(lw['w_in']))


def _in_proj_mix_kernel(x_ref, mod_ref, *refs, seg):
    p = dict(zip(_MIX_PARAMS, refs))
    pa_ref, gate_ref, ybc_ref, zpad_ref, zsh_ref = refs[len(_MIX_PARAMS):]
    h = _norm_modulate(x_ref[0], p['norm1_g'][...], mod_ref[0, 0:1, :], mod_ref[0, 1:2, :]).astype(BF16)
    gavg = p['gavg'][...]

    def proj(lo, n):
        return jnp.dot(h, p['w_in'][:, lo:lo + n], preferred_element_type=F32)

    chunks = [(pa_ref, lo) for lo in range(0, PA_COLS, PROJ_CHUNK)]
    chunks += [(gate_ref, lo) for lo in range(0, A_WIDTH, PROJ_CHUNK)]
    pending = iter(chunks)

    def project_next():
        nxt = next(pending, None)
        if nxt is not None:
            ref, lo = nxt
            w_lo = lo if ref is pa_ref else PA_COLS + lo
            ref[0, :, lo:lo + PROJ_CHUNK] = proj(w_lo, PROJ_CHUNK)

    b0 = PA_COLS + A_WIDTH
    c0 = b0 + 2 * B_WIDTH
    out = {}
    branches = [
        _conv_stages(proj(c0, C_WIDTH), proj(c0 + C_WIDTH, C_WIDTH), gavg, p['dw_w'], p['dw_b'][...],
                     p['conv_ln_g'][...], p['conv_ln_b'][...], p['pw_w'], p['pw_b'][...],
                     zpad_ref, zsh_ref, seg, out),
        _gmlp_stages(lambda lo: proj(b0 + lo, B_WIDTH), gavg, p['gmlp_ln_g'][...], p['gmlp_ln_b'][...],
                     p['ws_cat'], p['bs_tile'], out),
    ]
    _round_robin(branches, project_next)
    ybc_ref[0] = jnp.concatenate([out['yb'], out['yc']], axis=1).astype(BF16)
    for _ in chunks:
        project_next()


def _in_proj_mix(x, mod, lw, tile, seg):
    bsz, t_len, d = x.shape
    nseg = tile // seg

    def tok(cols):
        return pl.BlockSpec((1, tile, cols), lambda b, i: (b, i, 0))

    params = [lw[name] for name in _MIX_PARAMS]
    return pl.pallas_call(
        functools.partial(_in_proj_mix_kernel, seg=seg),
        grid=(bsz, t_len // tile),
        in_specs=[tok(d), pl.BlockSpec((1, 6, d), lambda b, i: (b, 0, 0))] + [_param_spec(p) for p in params],
        out_specs=[tok(PA_COLS), tok(A_WIDTH), tok(B_WIDTH + C_WIDTH)],
        out_shape=[jax.ShapeDtypeStruct((bsz, t_len, PA_COLS), F32),
                   jax.ShapeDtypeStruct((bsz, t_len, A_WIDTH), F32),
                   jax.ShapeDtypeStruct((bsz, t_len, B_WIDTH + C_WIDTH), BF16)],
        scratch_shapes=[pltpu.VMEM((nseg, seg + 2 * C_PAD, C_WIDTH), F32),
                        pltpu.VMEM((nseg, SUBLANES, seg + 2 * C_PAD - SUBLANES, C_WIDTH), F32)],
        compiler_params=_cparams(("arbitrary", "arbitrary")),
        name="in_proj_mix",
    )(x, mod, *[_param_arg(p) for p in params])


def _lower_bounds(logits, layer):
    depth = logits.shape[0]
    rows = [logits[j] for j in range(depth)]
    m = rows[0]
    for r in rows[1:]:
        m = jnp.maximum(m, r)
    es = [jnp.exp(r - m) for r in rows]
    tot = es[0]
    for e in es[1:]:
        tot = tot + e
    lb = jnp.zeros_like(m)
    for j in range(1, layer + 1):
        lb = lb + es[j] / tot
    return lb


def _chunk_rows(c, lo=0, n=A_CHUNK):
    return slice(c * A_CHUNK + lo, c * A_CHUNK + lo + n)


def _hgrn_factor_stages(q, z, v, lb, rev, n_chunks):
    width = q.shape[1]
    sub = A_SUB
    row = lax.broadcasted_iota(jnp.int32, (A_CHUNK, A_CHUNK), 0)
    col = lax.broadcasted_iota(jnp.int32, (A_CHUNK, A_CHUNK), 1)
    tri = ((col >= row) if rev else (col <= row)).astype(F32).astype(BF16)
    oml = 1.0 - lb
    sg = jax.nn.sigmoid(z)
    lf = jnp.log(lb + oml * sg)
    kk = oml * (1.0 - sg)
    hi = lf.astype(BF16)
    lo = (lf - hi.astype(F32)).astype(BF16)
    yield
    bs = [(jnp.dot(tri, hi[_chunk_rows(c)], preferred_element_type=F32)
           + jnp.dot(tri, lo[_chunk_rows(c)], preferred_element_type=F32)) * LOG2E
          for c in range(n_chunks)]
    yield
    wide, narrow = (0, 1) if rev else (1, 0)
    mid_row = sub // 2 if rev else sub // 2 - 1
    last_row = 0 if rev else A_CHUNK - 1
    b_last, b_own, b_wide, b_narrow, decs = [], [], [], [], []
    for bc in bs:
        mids = [bc[i * sub + mid_row:i * sub + mid_row + 1] for i in range(2)]
        b_last.append(jnp.broadcast_to(bc[last_row:last_row + 1], (A_CHUNK, width)))
        b_own += [jnp.broadcast_to(m, (sub, width)) for m in mids]
        b_wide.append(jnp.broadcast_to(mids[wide], (A_CHUNK, width)))
        b_narrow.append(mids[narrow] - bc[narrow * sub:(narrow + 1) * sub])
        decs.append(jnp.exp2(bc[last_row:last_row + 1]))
    b = jnp.concatenate(bs, axis=0)
    f = {'dec': decs, 'v': v.astype(BF16)}
    f['qe'] = (q * jnp.exp2(b)).astype(BF16)
    f['kd'] = (kk * jnp.exp2(jnp.concatenate(b_last, axis=0) - b)).astype(BF16)
    yield
    f['qd'] = (q * jnp.exp2(jnp.minimum(b - jnp.concatenate(b_own, axis=0), EXP2_CLAMP))).astype(BF16)
    yield
    kk_narrow = jnp.concatenate([kk[_chunk_rows(c, narrow * sub, sub)] for c in range(n_chunks)], axis=0)
    f['k_wide'] = (kk * jnp.exp2(jnp.minimum(jnp.concatenate(b_wide, axis=0) - b, EXP2_CLAMP))).astype(BF16)
    f['k_narrow'] = (kk_narrow
                     * jnp.exp2(jnp.minimum(jnp.concatenate(b_narrow, axis=0), EXP2_CLAMP))).astype(BF16)
    return f


def _hgrn_scores(f, rev, c):
    sub = A_SUB
    wide, narrow = (0, 1) if rev else (1, 0)
    row_n = lax.broadcasted_iota(jnp.int32, (sub, sub), 0)
    col_n = lax.broadcasted_iota(jnp.int32, (sub, sub), 1)
    row_w = lax.broadcasted_iota(jnp.int32, (sub, A_CHUNK), 0)
    col_w = lax.broadcasted_iota(jnp.int32, (sub, A_CHUNK), 1)
    mask_n = (col_n >= row_n) if rev else (col_n <= row_n)
    mask_w = (col_w >= row_w) if rev else (col_w <= row_w + sub)
    per_head = []
    for h in range(A_HEADS):
        hc = slice(h * A_DK, (h + 1) * A_DK)
        s_n = lax.dot_general(f['qd'][_chunk_rows(c, narrow * sub, sub), hc],
                              f['k_narrow'][c * sub:(c + 1) * sub, hc], _NT, preferred_element_type=F32)
        s_w = lax.dot_general(f['qd'][_chunk_rows(c, wide * sub, sub), hc],
                              f['k_wide'][_chunk_rows(c), hc], _NT, preferred_element_type=F32)
        per_head.append((jnp.where(mask_n, s_n, 0.0).astype(BF16),
                         jnp.where(mask_w, s_w, 0.0).astype(BF16)))
    return per_head


def _hgrn_intra(f, scores, rev, c):
    sub = A_SUB
    narrow = 1 if rev else 0
    per_head = []
    for h in range(A_HEADS):
        hc = slice(h * A_DK, (h + 1) * A_DK)
        s_n, s_w = scores[h]
        o_n = jnp.dot(s_n, f['v'][_chunk_rows(c, narrow * sub, sub), hc], preferred_element_type=F32)
        o_w = jnp.dot(s_w, f['v'][_chunk_rows(c), hc], preferred_element_type=F32)
        intra = jnp.concatenate([o_w, o_n] if rev else [o_n, o_w], axis=0)
        upd = lax.dot_general(f['v'][_chunk_rows(c), hc], f['kd'][_chunk_rows(c), hc], _TN,
                              preferred_element_type=F32)
        per_head.append((intra, upd))
    return per_head


def _hgrn_stages(q, z, v, lb, rev, n_chunks, get_state, set_state, put_out, stride=1):
    f = yield from _hgrn_factor_stages(q, z, v, lb, rev, n_chunks)
    scores, intra = [], []
    for c in range(n_chunks):
        scores.append(_hgrn_scores(f, rev, c))
        if c % stride == stride - 1:
            yield
    for c in range(n_chunks):
        intra.append(_hgrn_intra(f, scores[c], rev, c))
        if c % stride == stride - 1:
            yield
    states = [get_state(h) for h in range(A_HEADS)]
    for n, c in enumerate(range(n_chunks - 1, -1, -1) if rev else range(n_chunks)):
        outs = []
        for h in range(A_HEADS):
            hc = slice(h * A_DK, (h + 1) * A_DK)
            inter = lax.dot_general(f['qe'][_chunk_rows(c), hc], states[h].astype(BF16), _NT,
                                    preferred_element_type=F32)
            outs.append(inter + intra[c][h][0])
            states[h] = states[h] * f['dec'][c][:, hc] + intra[c][h][1]
        put_out(c, jnp.concatenate(outs, axis=1))
        if n % stride == stride - 1:
            yield
    for h in range(A_HEADS):
        set_state(h, states[h])


def _hgrn_kernel(qf_ref, zf_ref, vf_ref, qb_ref, zb_ref, vb_ref, lbl_ref, s0_ref,
                 of_ref, ob_ref, st_ref, *, layer, n_chunks):
    @pl.when(pl.program_id(1) == 0)
    def _():
        st_ref[...] = s0_ref[...]

    lb = _lower_bounds(lbl_ref[...], layer)

    def direction(d, q_ref, z_ref, v_ref, o_ref):
        def set_state(h, val):
            st_ref[0, d, h] = val

        def put_out(c, val):
            o_ref[0, _chunk_rows(c), :] = val

        return _hgrn_stages(q_ref[0], z_ref[0], v_ref[0], lb[d:d + 1], d == 1, n_chunks,
                            lambda h: st_ref[0, d, h], set_state, put_out)

    _round_robin([direction(0, qf_ref, zf_ref, vf_ref, of_ref),
                  direction(1, qb_ref, zb_ref, vb_ref, ob_ref)])


def _hgrn(pa, lb_logits, s0, layer, tile):
    bsz, t_len, _ = pa.shape
    nb = t_len // tile
    depth = lb_logits.shape[0]

    def col_spec(colblk, rev):
        if rev:
            return pl.BlockSpec((1, tile, A_WIDTH), lambda b, i: (b, nb - 1 - i, colblk))
        return pl.BlockSpec((1, tile, A_WIDTH), lambda b, i: (b, i, colblk))

    st_spec = pl.BlockSpec((1, 2, A_HEADS, A_DK, A_DK), lambda b, i: (b, 0, 0, 0, 0))
    o_sds = jax.ShapeDtypeStruct((bsz, t_len, A_WIDTH), F32)
    return pl.pallas_call(
        functools.partial(_hgrn_kernel, layer=layer, n_chunks=tile // A_CHUNK),
        grid=(bsz, nb),
        in_specs=[col_spec(0, False), col_spec(1, False), col_spec(3, False),
                  col_spec(0, True), col_spec(2, True), col_spec(3, True),
                  _full((depth, 2, A_WIDTH)), st_spec],
        out_specs=[col_spec(0, False), col_spec(0, True), st_spec],
        out_shape=[o_sds, o_sds, jax.ShapeDtypeStruct(s0.shape, F32)],
        compiler_params=_cparams(("arbitrary", "arbitrary")),
        name="hgrn",
    )(pa, pa, pa, pa, pa, pa, lb_logits, s0)


def _out_ffn_kernel(x_ref, of_ref, ob_ref, gate_ref, ybc_ref, mod_ref, og_ref, wo_ref,
                    g_ref, w13_ref, w2_ref, fg_ref, o_ref, *, d_ff, chunk, final_norm):
    o = of_ref[0] + ob_ref[0]
    parts = []
    for h in range(A_HEADS):
        oh = o[:, h * A_DK:(h + 1) * A_DK]
        parts.append(oh * lax.rsqrt(jnp.mean(oh * oh, axis=-1, keepdims=True) + EPS))
    gate = gate_ref[0]
    ya = jnp.concatenate(parts, axis=1) * og_ref[...] * (gate * jax.nn.sigmoid(gate))
    y = (jnp.dot(ya.astype(BF16), wo_ref[0:A_WIDTH, :], preferred_element_type=F32)
         + jnp.dot(ybc_ref[0], wo_ref[A_WIDTH:, :], preferred_element_type=F32))
    x = x_ref[0] + mod_ref[0, 2:3, :] * y
    h = _norm_modulate(x, g_ref[...], mod_ref[0, 3:4, :], mod_ref[0, 4:5, :]).astype(BF16)
    acts = []
    for c in range(d_ff // chunk):
        a = jnp.dot(h, w13_ref[:, c * chunk:(c + 1) * chunk], preferred_element_type=F32)
        b = jnp.dot(h, w13_ref[:, d_ff + c * chunk:d_ff + (c + 1) * chunk], preferred_element_type=F32)
        acts.append((a * jax.nn.sigmoid(a) * b).astype(BF16))
    acc = jnp.dot(jnp.concatenate(acts, axis=1), w2_ref[...], preferred_element_type=F32)
    y = x + mod_ref[0, 5:6, :] * acc
    if final_norm:
        y = y * lax.rsqrt(jnp.mean(y * y, axis=-1, keepdims=True) + EPS) * fg_ref[...]
    o_ref[0] = y


def _out_ffn(x, o_f, o_b, gate, ybc, mod, lw, final_g, tile, final_norm):
    bsz, t_len, d = x.shape
    d_ff = lw['ffn_w2'].shape[0]

    def tok(cols):
        return pl.BlockSpec((1, tile, cols), lambda b, i: (b, i, 0))

    params = [lw['onorm_g'], lw['w_out'], lw['norm2_g'], lw['ffn_w13'], lw['ffn_w2'], final_g.reshape(1, d)]
    return pl.pallas_call(
        functools.partial(_out_ffn_kernel, d_ff=d_ff, chunk=FFN_CHUNK, final_norm=final_norm),
        grid=(bsz, t_len // tile),
        in_specs=[tok(d), tok(A_WIDTH), tok(A_WIDTH), tok(A_WIDTH), tok(B_WIDTH + C_WIDTH),
                  pl.BlockSpec((1, 6, d), lambda b, i: (b, 0, 0))] + [_param_spec(p) for p in params],
        out_specs=tok(d),
        out_shape=jax.ShapeDtypeStruct(x.shape, F32),
        compiler_params=_cparams(("arbitrary", "arbitrary")),
        name="out_ffn",
    )(x, o_f, o_b, gate, ybc, mod, *[_param_arg(p) for p in params])


def _layer_params(l, gavg, mxu_weights, norm1_g, hgrn_onorm_g, gmlp_ln_g, gmlp_ln_b, gmlp_b_s,
                  conv_dw_w, conv_dw_b, conv_ln_g, conv_ln_b, conv_pw_b, norm2_g):
    row = lambda a: a.reshape(1, -1).astype(F32)
    lw = {name: _LayerSlice(w, l) for name, w in mxu_weights.items()}
    lw.update({
        'norm1_g': row(norm1_g[l]), 'onorm_g': row(hgrn_onorm_g[l]), 'gavg': gavg,
        'gmlp_ln_g': row(gmlp_ln_g[l]), 'gmlp_ln_b': row(gmlp_ln_b[l]),
        'bs_tile': jnp.repeat(gmlp_b_s[l].T, B_WIDTH // B_GROUPS, axis=1).astype(F32),
        'dw_w': conv_dw_w[l].astype(F32), 'dw_b': row(conv_dw_b[l]),
        'conv_ln_g': row(conv_ln_g[l]), 'conv_ln_b': row(conv_ln_b[l]),
        'pw_b': row(conv_pw_b[l]), 'norm2_g': row(norm2_g[l]),
    })
    return lw


def kernel(x, c, ctx, c_ctx, ada_w, ada_b, norm1_g, w_in, hgrn_lb_logits, hgrn_onorm_g, gmlp_ln_g, gmlp_ln_b, gmlp_w_s, gmlp_b_s, conv_dw_w, conv_dw_b, conv_ln_g, conv_ln_b, conv_pw_w, conv_pw_b, w_out, norm2_g, ffn_w13, ffn_w2, final_norm_g):
    bsz, t_len, d = x.shape
    ctx_len = ctx.shape[1]
    depth = ada_w.shape[0]
    lb_logits = hgrn_lb_logits.astype(F32)

    cs = jnp.concatenate([c, c_ctx[None, :], jnp.zeros((MOD_ROWS - bsz - 1, d), F32)], axis=0)
    mod_all = _modulation(cs, ada_w, ada_b)
    grp = jnp.arange(B_WIDTH) // (B_WIDTH // B_GROUPS)
    gavg = ((grp[:, None] == grp[None, :]).astype(F32) / (B_WIDTH // B_GROUPS)).astype(BF16)
    zero_state = jnp.zeros((bsz, 2, A_HEADS, A_DK, A_DK), F32)
    mxu_weights = {
        'w_in': w_in.astype(BF16), 'w_out': w_out.astype(BF16),
        'ffn_w13': ffn_w13.astype(BF16), 'ffn_w2': ffn_w2.astype(BF16),
        'pw_w': conv_pw_w.astype(BF16),
        'ws_cat': gmlp_w_s.reshape(depth, B_GROUPS * B_CHUNK, B_CHUNK).astype(BF16),
    }
    row_len = t_len // (t_len // GRID_W)
    ctx_flat = ctx.reshape(1, bsz * ctx_len, d)
    ctx_tile = min(LATENT_TILE, bsz * ctx_len)

    for l in range(depth):
        last = l == depth - 1
        lw = _layer_params(l, gavg, mxu_weights, norm1_g, hgrn_onorm_g, gmlp_ln_g, gmlp_ln_b, gmlp_b_s,
                           conv_dw_w, conv_dw_b, conv_ln_g, conv_ln_b, conv_pw_b, norm2_g)
        mod = mod_all[l, :bsz].reshape(bsz, 6, d)
        mod_ctx = mod_all[l, bsz].reshape(1, 6, d)
        if last:
            pa_c = _in_proj(ctx_flat, mod_ctx, lw, ctx_tile)
            _, _, state = _hgrn(pa_c.reshape(bsz, ctx_len, PA_COLS), lb_logits, zero_state, l, ctx_len)
        else:
            pa_c, gate_c, ybc_c = _in_proj_mix(ctx_flat, mod_ctx, lw, ctx_tile, ctx_len)
            of_c, ob_c, state = _hgrn(pa_c.reshape(bsz, ctx_len, PA_COLS), lb_logits, zero_state, l, ctx_len)
            ctx_flat = _out_ffn(ctx_flat, of_c.reshape(1, -1, A_WIDTH), ob_c.reshape(1, -1, A_WIDTH),
                                gate_c, ybc_c, mod_ctx, lw, final_norm_g, ctx_tile, False)
        pa, gate, ybc = _in_proj_mix(x, mod, lw, LATENT_TILE, row_len)
        o_f, o_b, _ = _hgrn(pa, lb_logits, state, l, LATENT_TILE)
        x = _out_ffn(x, o_f, o_b, gate, ybc, mod, lw, final_norm_g, LATENT_TILE, last)
    return x
```

```python
import functools
from typing import NamedTuple

import jax
import jax.numpy as jnp
from jax import lax
from jax.experimental import pallas as pl
from jax.experimental.pallas import tpu as pltpu

F32 = jnp.float32
BF16 = jnp.bfloat16

D_MODEL = 1024
GRID_W = 64
A_HEADS = 4
A_DK = 128
A_WIDTH = A_HEADS * A_DK
A_CHUNK = 64
A_SUB = 32
HGRN_GROUP = 1
HGRN_LAG = 1
B_GROUPS = 4
B_WIDTH = 256
B_CHUNK = 128
C_GROUPS = 4
C_WIDTH = 256
C_KERNEL = 31
C_PAD = 16
SUBLANES = 8
PROJ_CHUNK = 256
FFN_CHUNK = 256
PA_COLS = 4 * A_WIDTH
EPS = 1e-6
EXP2_CLAMP = 80.0
LOG2E = 1.4426950408889634
MOD_ROWS = 8
LATENT_TILE = 512

V7X_VMEM_LIMIT = 56 * 1024 * 1024

_NT = (((1,), (1,)), ((), ()))
_TN = (((0,), (0,)), ((), ()))
_FINISHED = object()


def _cparams(sem):
    return pltpu.CompilerParams(dimension_semantics=sem, vmem_limit_bytes=V7X_VMEM_LIMIT)


def _full(shape):
    n = len(shape)
    return pl.BlockSpec(shape, lambda *_: (0,) * n)


class _LayerSlice(NamedTuple):
    stacked: jax.Array
    layer: int

    @property
    def shape(self):
        return self.stacked.shape[1:]


def _param_spec(p, block=None):
    if not isinstance(p, _LayerSlice):
        return _full(p.shape)
    block = p.shape if block is None else block
    index = (p.layer,) + (0,) * len(block)
    return pl.BlockSpec((None,) + tuple(block), lambda *_: index, pipeline_mode=pl.Buffered(1))


def _param_arg(p):
    return p.stacked if isinstance(p, _LayerSlice) else p


def _round_robin(branches, between=None):
    while branches:
        branches = [g for g in branches if next(g, _FINISHED) is not _FINISHED]
        if between is not None:
            between()


def _mod_kernel(cs_ref, w_ref, b_ref, o_ref):
    s = cs_ref[...]
    s = s * jax.nn.sigmoid(s)
    w = w_ref[0]
    s_hi = s.astype(BF16)
    s_lo = (s - s_hi.astype(F32)).astype(BF16)
    w_hi = w.astype(BF16)
    w_lo = (w - w_hi.astype(F32)).astype(BF16)
    r = jnp.dot(jnp.concatenate([s_hi, s_lo], axis=0), w_hi, preferred_element_type=F32)
    o_ref[0] = (r[:MOD_ROWS] + r[MOD_ROWS:] + jnp.dot(s_hi, w_lo, preferred_element_type=F32)
                + b_ref[0])


def _modulation(cs, ada_w, ada_b):
    depth, d, n = ada_w.shape
    tn = 1024
    return pl.pallas_call(
        _mod_kernel,
        grid=(depth, n // tn),
        in_specs=[
            pl.BlockSpec((MOD_ROWS, d), lambda l, j: (0, 0)),
            pl.BlockSpec((1, d, tn), lambda l, j: (l, 0, j)),
            pl.BlockSpec((1, 1, tn), lambda l, j: (l, 0, j)),
        ],
        out_specs=pl.BlockSpec((1, MOD_ROWS, tn), lambda l, j: (l, 0, j)),
        out_shape=jax.ShapeDtypeStruct((depth, MOD_ROWS, n), F32),
        compiler_params=_cparams(("arbitrary", "arbitrary")),
        name="modulation",
    )(cs, ada_w, ada_b.reshape(depth, 1, n))


def _norm_modulate(x, g, shift, scale):
    ms = jnp.mean(x * x, axis=-1, keepdims=True)
    y = x * lax.rsqrt(ms + EPS) * g
    return y * (1.0 + scale) + shift


def _group_mean(x, gavg):
    return jnp.dot(x.astype(BF16), gavg, preferred_element_type=F32)


def _group_layer_norm_stages(x, gavg, g, b, out):
    mu = _group_mean(x, gavg)
    yield
    xc = x - mu
    var = _group_mean(xc * xc, gavg)
    yield
    out['ln'] = xc * lax.rsqrt(var + EPS) * g + b


def _gmlp_stages(project, gavg, ln_g, ln_b, ws_ref, bs_ref, out):
    u = project(0)
    v = project(B_WIDTH)
    tile = u.shape[0]
    yield
    ln = {}
    yield from _group_layer_norm_stages(jax.nn.gelu(v), gavg, ln_g, ln_b, ln)
    vn = ln['ln'].astype(BF16)
    gu = jax.nn.gelu(u)
    lane_group = lax.broadcasted_iota(jnp.int32, (B_CHUNK, B_WIDTH), 1) // (B_WIDTH // B_GROUPS)
    mixed = []
    for n in range(tile // B_CHUNK):
        yield
        r = jnp.dot(ws_ref[...], vn[n * B_CHUNK:(n + 1) * B_CHUNK], preferred_element_type=F32)
        m = bs_ref[...]
        for g in range(B_GROUPS):
            m = m + jnp.where(lane_group == g, r[g * B_CHUNK:(g + 1) * B_CHUNK], 0.0)
        mixed.append(m)
    out['yb'] = gu * jnp.concatenate(mixed, axis=0)


def _conv_stages(ca, cgate, gavg, dww_ref, dw_b, ln_g, ln_b, pww_ref, pw_b, zpad_ref, zsh_ref, seg, out):
    tile = ca.shape[0]
    z = ca * jax.nn.sigmoid(cgate)
    nseg = tile // seg
    pad_zeros = jnp.zeros((C_PAD, C_WIDTH), F32)
    for s in range(nseg):
        zpad_ref[s, 0:C_PAD, :] = pad_zeros
        zpad_ref[s, C_PAD + seg:2 * C_PAD + seg, :] = pad_zeros
        zpad_ref[s, C_PAD:C_PAD + seg, :] = z[s * seg:(s + 1) * seg]
    span = seg + 2 * C_PAD - SUBLANES
    convs = []
    for s in range(nseg):
        for r in range(1, SUBLANES):
            zsh_ref[s, r] = zpad_ref[s, r:r + span, :]
        acc = jnp.zeros((seg, C_WIDTH), F32)
        for j in range(C_KERNEL):
            start = C_PAD - C_KERNEL // 2 + j
            r, a = start % SUBLANES, start - start % SUBLANES
            tap = zpad_ref[s, a:a + seg, :] if r == 0 else zsh_ref[s, r, a:a + seg, :]
            acc = acc + dww_ref[j:j + 1, :] * tap
        convs.append(acc)
        yield
    ln = {}
    yield from _group_layer_norm_stages(jnp.concatenate(convs, axis=0) + dw_b, gavg, ln_g, ln_b, ln)
    zc = ln['ln']
    zc = (zc * jax.nn.sigmoid(zc)).astype(BF16)
    out['yc'] = jnp.dot(zc, pww_ref[...], preferred_element_type=F32) + pw_b


_MIX_PARAMS = ('norm1_g', 'w_in', 'gavg', 'gmlp_ln_g', 'gmlp_ln_b', 'ws_cat', 'bs_tile',
               'dw_w', 'dw_b', 'conv_ln_g', 'conv_ln_b', 'pw_w', 'pw_b')


def _in_proj_kernel(x_ref, mod_ref, g_ref, w_ref, pa_ref):
    h = _norm_modulate(x_ref[0], g_ref[...], mod_ref[0, 0:1, :], mod_ref[0, 1:2, :]).astype(BF16)
    pa_ref[0] = jnp.dot(h, w_ref[...], preferred_element_type=F32)


def _in_proj(x, mod, lw, tile):
    bsz, t_len, d = x.shape
    return pl.pallas_call(
        _in_proj_kernel,
        grid=(bsz, t_len // tile),
        in_specs=[
            pl.BlockSpec((1, tile, d), lambda b, i: (b, i, 0)),
            pl.BlockSpec((1, 6, d), lambda b, i: (b, 0, 0)),
            _full((1, d)),
            _param_spec(lw['w_in'], (d, PA_COLS)),
        ],
        out_specs=pl.BlockSpec((1, tile, PA_COLS), lambda b, i: (b, i, 0)),
        out_shape=jax.ShapeDtypeStruct((bsz, t_len, PA_COLS), F32),
        compiler_params=_cparams(("arbitrary", "arbitrary")),
        name="in_proj",
    )(x, mod, lw['norm1_g'], _param_arg(lw['w_in']))


def _in_proj_mix_kernel(x_ref, mod_ref, *refs, seg):
    p = dict(zip(_MIX_PARAMS, refs))
    pa_ref, gate_ref, ybc_ref, zpad_ref, zsh_ref = refs[len(_MIX_PARAMS):]
    h = _norm_modulate(x_ref[0], p['norm1_g'][...], mod_ref[0, 0:1, :], mod_ref[0, 1:2, :]).astype(BF16)
    gavg = p['gavg'][...]

    def proj(lo, n):
        return jnp.dot(h, p['w_in'][:, lo:lo + n], preferred_element_type=F32)

    chunks = [(pa_ref, lo) for lo in range(0, PA_COLS, PROJ_CHUNK)]
    chunks += [(gate_ref, lo) for lo in range(0, A_WIDTH, PROJ_CHUNK)]
    pending = iter(chunks)

    def project_next():
        nxt = next(pending, None)
        if nxt is not None:
            ref, lo = nxt
            w_lo = lo if ref is pa_ref else PA_COLS + lo
            ref[0, :, lo:lo + PROJ_CHUNK] = proj(w_lo, PROJ_CHUNK)

    b0 = PA_COLS + A_WIDTH
    c0 = b0 + 2 * B_WIDTH
    out = {}
    branches = [
        _conv_stages(proj(c0, C_WIDTH), proj(c0 + C_WIDTH, C_WIDTH), gavg, p['dw_w'], p['dw_b'][...],
                     p['conv_ln_g'][...], p['conv_ln_b'][...], p['pw_w'], p['pw_b'][...],
                     zpad_ref, zsh_ref, seg, out),
        _gmlp_stages(lambda lo: proj(b0 + lo, B_WIDTH), gavg, p['gmlp_ln_g'][...], p['gmlp_ln_b'][...],
                     p['ws_cat'], p['bs_tile'], out),
    ]
    _round_robin(branches, project_next)
    ybc_ref[0] = jnp.concatenate([out['yb'], out['yc']], axis=1).astype(BF16)
    for _ in chunks:
        project_next()


def _in_proj_mix(x, mod, lw, tile, seg):
    bsz, t_len, d = x.shape
    nseg = tile // seg

    def tok(cols):
        return pl.BlockSpec((1, tile, cols), lambda b, i: (b, i, 0))

    params = [lw[name] for name in _MIX_PARAMS]
    return pl.pallas_call(
        functools.partial(_in_proj_mix_kernel, seg=seg),
        grid=(bsz, t_len // tile),
        in_specs=[tok(d), pl.BlockSpec((1, 6, d), lambda b, i: (b, 0, 0))] + [_param_spec(p) for p in params],
        out_specs=[tok(PA_COLS), tok(A_WIDTH), tok(B_WIDTH + C_WIDTH)],
        out_shape=[jax.ShapeDtypeStruct((bsz, t_len, PA_COLS), F32),
                   jax.ShapeDtypeStruct((bsz, t_len, A_WIDTH), F32),
                   jax.ShapeDtypeStruct((bsz, t_len, B_WIDTH + C_WIDTH), BF16)],
        scratch_shapes=[pltpu.VMEM((nseg, seg + 2 * C_PAD, C_WIDTH), F32),
                        pltpu.VMEM((nseg, SUBLANES, seg + 2 * C_PAD - SUBLANES, C_WIDTH), F32)],
        compiler_params=_cparams(("arbitrary", "arbitrary")),
        name="in_proj_mix",
    )(x, mod, *[_param_arg(p) for p in params])


def _lower_bounds(logits, layer):
    depth = logits.shape[0]
    rows = [logits[j] for j in range(depth)]
    m = rows[0]
    for r in rows[1:]:
        m = jnp.maximum(m, r)
    es = [jnp.exp(r - m) for r in rows]
    tot = es[0]
    for e in es[1:]:
        tot = tot + e
    lb = jnp.zeros_like(m)
    for j in range(1, layer + 1):
        lb = lb + es[j] / tot
    return lb


def _chunk_rows(c, lo=0, n=A_CHUNK):
    return slice(c * A_CHUNK + lo, c * A_CHUNK + lo + n)


def _hgrn_gates(z, lb, rev, n_chunks):
    row = lax.broadcasted_iota(jnp.int32, (A_CHUNK, A_CHUNK), 0)
    col = lax.broadcasted_iota(jnp.int32, (A_CHUNK, A_CHUNK), 1)
    tri = ((col >= row) if rev else (col <= row)).astype(F32).astype(BF16)
    oml = 1.0 - lb
    sg = jax.nn.sigmoid(z)
    lf = jnp.log(lb + oml * sg)
    kk = oml * (1.0 - sg)
    hi = lf.astype(BF16)
    lo = (lf - hi.astype(F32)).astype(BF16)
    bs = [(jnp.dot(tri, hi[_chunk_rows(c)], preferred_element_type=F32)
           + jnp.dot(tri, lo[_chunk_rows(c)], preferred_element_type=F32)) * LOG2E
          for c in range(n_chunks)]
    return kk, bs


def _hgrn_factors(q, v, kk, bs, rev):
    n_chunks = len(bs)
    width = q.shape[1]
    sub = A_SUB
    wide, narrow = (0, 1) if rev else (1, 0)
    mid_row = sub // 2 if rev else sub // 2 - 1
    last_row = 0 if rev else A_CHUNK - 1
    b_last, b_own, b_wide, b_narrow, decs = [], [], [], [], []
    for bc in bs:
        mids = [bc[i * sub + mid_row:i * sub + mid_row + 1] for i in range(2)]
        b_last.append(jnp.broadcast_to(bc[last_row:last_row + 1], (A_CHUNK, width)))
        b_own += [jnp.broadcast_to(m, (sub, width)) for m in mids]
        b_wide.append(jnp.broadcast_to(mids[wide], (A_CHUNK, width)))
        b_narrow.append(mids[narrow] - bc[narrow * sub:(narrow + 1) * sub])
        decs.append(jnp.exp2(bc[last_row:last_row + 1]))
    b = jnp.concatenate(bs, axis=0)
    f = {'dec': decs, 'v': v.astype(BF16)}
    f['qe'] = (q * jnp.exp2(b)).astype(BF16)
    f['kd'] = (kk * jnp.exp2(jnp.concatenate(b_last, axis=0) - b)).astype(BF16)
    f['qd'] = (q * jnp.exp2(jnp.minimum(b - jnp.concatenate(b_own, axis=0), EXP2_CLAMP))).astype(BF16)
    kk_narrow = jnp.concatenate([kk[_chunk_rows(c, narrow * sub, sub)] for c in range(n_chunks)], axis=0)
    f['k_wide'] = (kk * jnp.exp2(jnp.minimum(jnp.concatenate(b_wide, axis=0) - b, EXP2_CLAMP))).astype(BF16)
    f['k_narrow'] = (kk_narrow
                     * jnp.exp2(jnp.minimum(jnp.concatenate(b_narrow, axis=0), EXP2_CLAMP))).astype(BF16)
    return f


def _hgrn_scores(f, rev, c):
    sub = A_SUB
    wide, narrow = (0, 1) if rev else (1, 0)
    row_n = lax.broadcasted_iota(jnp.int32, (sub, sub), 0)
    col_n = lax.broadcasted_iota(jnp.int32, (sub, sub), 1)
    row_w = lax.broadcasted_iota(jnp.int32, (sub, A_CHUNK), 0)
    col_w = lax.broadcasted_iota(jnp.int32, (sub, A_CHUNK), 1)
    mask_n = (col_n >= row_n) if rev else (col_n <= row_n)
    mask_w = (col_w >= row_w) if rev else (col_w <= row_w + sub)
    per_head = []
    for h in range(A_HEADS):
        hc = slice(h * A_DK, (h + 1) * A_DK)
        s_n = lax.dot_general(f['qd'][_chunk_rows(c, narrow * sub, sub), hc],
                              f['k_narrow'][c * sub:(c + 1) * sub, hc], _NT, preferred_element_type=F32)
        s_w = lax.dot_general(f['qd'][_chunk_rows(c, wide * sub, sub), hc],
                              f['k_wide'][_chunk_rows(c), hc], _NT, preferred_element_type=F32)
        per_head.append((jnp.where(mask_n, s_n, 0.0).astype(BF16),
                         jnp.where(mask_w, s_w, 0.0).astype(BF16)))
    return per_head


def _hgrn_intra(f, scores, rev, c):
    sub = A_SUB
    narrow = 1 if rev else 0
    per_head = []
    for h in range(A_HEADS):
        hc = slice(h * A_DK, (h + 1) * A_DK)
        s_n, s_w = scores[h]
        o_n = jnp.dot(s_n, f['v'][_chunk_rows(c, narrow * sub, sub), hc], preferred_element_type=F32)
        o_w = jnp.dot(s_w, f['v'][_chunk_rows(c), hc], preferred_element_type=F32)
        intra = jnp.concatenate([o_w, o_n] if rev else [o_n, o_w], axis=0)
        upd = lax.dot_general(f['v'][_chunk_rows(c), hc], f['kd'][_chunk_rows(c), hc], _TN,
                              preferred_element_type=F32)
        per_head.append((intra, upd))
    return per_head


def _hgrn_stages(q, z, v, lb, rev, n_chunks, get_state, set_state, put_out, group=HGRN_GROUP, lag=HGRN_LAG):
    n_groups = n_chunks // group
    order = list(range(n_groups - 1, -1, -1) if rev else range(n_groups))
    facs, scores, intra = {}, {}, {}
    states = [get_state(h) for h in range(A_HEADS)]
    for step in range(n_groups + 3 * lag):
        if step < n_groups:
            g = order[step]
            rows = slice(g * group * A_CHUNK, (g + 1) * group * A_CHUNK)
            facs[g] = _hgrn_factors(q[rows], v[rows], *_hgrn_gates(z[rows], lb, rev, group), rev)
        if 0 <= step - lag < n_groups:
            g = order[step - lag]
            scores[g] = [_hgrn_scores(facs[g], rev, c) for c in range(group)]
        if 0 <= step - 2 * lag < n_groups:
            g = order[step - 2 * lag]
            intra[g] = [_hgrn_intra(facs[g], scores[g][c], rev, c) for c in range(group)]
        if 0 <= step - 3 * lag < n_groups:
            g = order[step - 3 * lag]
            f = facs[g]
            for c in (range(group - 1, -1, -1) if rev else range(group)):
                outs = []
                for h in range(A_HEADS):
                    hc = slice(h * A_DK, (h + 1) * A_DK)
                    inter = lax.dot_general(f['qe'][_chunk_rows(c), hc], states[h].astype(BF16), _NT,
                                            preferred_element_type=F32)
                    outs.append(inter + intra[g][c][h][0])
                    states[h] = states[h] * f['dec'][c][:, hc] + intra[g][c][h][1]
                put_out(g * group + c, jnp.concatenate(outs, axis=1))
        yield
    for h in range(A_HEADS):
        set_state(h, states[h])


def _hgrn_kernel(qf_ref, zf_ref, vf_ref, qb_ref, zb_ref, vb_ref, lbl_ref, s0_ref,
                 of_ref, ob_ref, st_ref, *, layer, n_chunks):
    @pl.when(pl.program_id(1) == 0)
    def _():
        st_ref[...] = s0_ref[...]

    lb = _lower_bounds(lbl_ref[...], layer)

    def direction(d, q_ref, z_ref, v_ref, o_ref):
        def set_state(h, val):
            st_ref[0, d, h] = val

        def put_out(c, val):
            o_ref[0, _chunk_rows(c), :] = val

        return _hgrn_stages(q_ref[0], z_ref[0], v_ref[0], lb[d:d + 1], d == 1, n_chunks,
                            lambda h: st_ref[0, d, h], set_state, put_out)

    _round_robin([direction(0, qf_ref, zf_ref, vf_ref, of_ref),
                  direction(1, qb_ref, zb_ref, vb_ref, ob_ref)])


def _hgrn(pa, lb_logits, s0, layer, tile):
    bsz, t_len, _ = pa.shape
    nb = t_len // tile
    depth = lb_logits.shape[0]

    def col_spec(colblk, rev):
        if rev:
            return pl.BlockSpec((1, tile, A_WIDTH), lambda b, i: (b, nb - 1 - i, colblk))
        return pl.BlockSpec((1, tile, A_WIDTH), lambda b, i: (b, i, colblk))

    st_spec = pl.BlockSpec((1, 2, A_HEADS, A_DK, A_DK), lambda b, i: (b, 0, 0, 0, 0))
    o_sds = jax.ShapeDtypeStruct((bsz, t_len, A_WIDTH), F32)
    return pl.pallas_call(
        functools.partial(_hgrn_kernel, layer=layer, n_chunks=tile // A_CHUNK),
        grid=(bsz, nb),
        in_specs=[col_spec(0, False), col_spec(1, False), col_spec(3, False),
                  col_spec(0, True), col_spec(2, True), col_spec(3, True),
                  _full((depth, 2, A_WIDTH)), st_spec],
        out_specs=[col_spec(0, False), col_spec(0, True), st_spec],
        out_shape=[o_sds, o_sds, jax.ShapeDtypeStruct(s0.shape, F32)],
        compiler_params=_cparams(("arbitrary", "arbitrary")),
        name="hgrn",
    )(pa, pa, pa, pa, pa, pa, lb_logits, s0)


def _out_ffn_kernel(x_ref, of_ref, ob_ref, gate_ref, ybc_ref, mod_ref, og_ref, wo_ref,
                    g_ref, w13_ref, w2_ref, fg_ref, o_ref, *, d_ff, chunk, final_norm):
    o = of_ref[0] + ob_ref[0]
    parts = []
    for h in range(A_HEADS):
        oh = o[:, h * A_DK:(h + 1) * A_DK]
        parts.append(oh * lax.rsqrt(jnp.mean(oh * oh, axis=-1, keepdims=True) + EPS))
    gate = gate_ref[0]
    ya = jnp.concatenate(parts, axis=1) * og_ref[...] * (gate * jax.nn.sigmoid(gate))
    y = (jnp.dot(ya.astype(BF16), wo_ref[0:A_WIDTH, :], preferred_element_type=F32)
         + jnp.dot(ybc_ref[0], wo_ref[A_WIDTH:, :], preferred_element_type=F32))
    x = x_ref[0] + mod_ref[0, 2:3, :] * y
    h = _norm_modulate(x, g_ref[...], mod_ref[0, 3:4, :], mod_ref[0, 4:5, :]).astype(BF16)
    acts = []
    for c in range(d_ff // chunk):
        a = jnp.dot(h, w13_ref[:, c * chunk:(c + 1) * chunk], preferred_element_type=F32)
        b = jnp.dot(h, w13_ref[:, d_ff + c * chunk:d_ff + (c + 1) * chunk], preferred_element_type=F32)
        acts.append((a * jax.nn.sigmoid(a) * b).astype(BF16))
    acc = jnp.dot(jnp.concatenate(acts, axis=1), w2_ref[...], preferred_element_type=F32)
    y = x + mod_ref[0, 5:6, :] * acc
    if final_norm:
        y = y * lax.rsqrt(jnp.mean(y * y, axis=-1, keepdims=True) + EPS) * fg_ref[...]
    o_ref[0] = y


def _out_ffn(x, o_f, o_b, gate, ybc, mod, lw, final_g, tile, final_norm):
    bsz, t_len, d = x.shape
    d_ff = lw['ffn_w2'].shape[0]

    def tok(cols):
        return pl.BlockSpec((1, tile, cols), lambda b, i: (b, i, 0))

    params = [lw['onorm_g'], lw['w_out'], lw['norm2_g'], lw['ffn_w13'], lw['ffn_w2'], final_g.reshape(1, d)]
    return pl.pallas_call(
        functools.partial(_out_ffn_kernel, d_ff=d_ff, chunk=FFN_CHUNK, final_norm=final_norm),
        grid=(bsz, t_len // tile),
        in_specs=[tok(d), tok(A_WIDTH), tok(A_WIDTH), tok(A_WIDTH), tok(B_WIDTH + C_WIDTH),
                  pl.BlockSpec((1, 6, d), lambda b, i: (b, 0, 0))] + [_param_spec(p) for p in params],
        out_specs=tok(d),
        out_shape=jax.ShapeDtypeStruct(x.shape, F32),
        compiler_params=_cparams(("arbitrary", "arbitrary")),
        name="out_ffn",
    )(x, o_f, o_b, gate, ybc, mod, *[_param_arg(p) for p in params])


def _layer_params(l, gavg, mxu_weights, norm1_g, hgrn_onorm_g, gmlp_ln_g, gmlp_ln_b, gmlp_b_s,
                  conv_dw_w, conv_dw_b, conv_ln_g, conv_ln_b, conv_pw_b, norm2_g):
    row = lambda a: a.reshape(1, -1).astype(F32)
    lw = {name: _LayerSlice(w, l) for name, w in mxu_weights.items()}
    lw.update({
        'norm1_g': row(norm1_g[l]), 'onorm_g': row(hgrn_onorm_g[l]), 'gavg': gavg,
        'gmlp_ln_g': row(gmlp_ln_g[l]), 'gmlp_ln_b': row(gmlp_ln_b[l]),
        'bs_tile': jnp.repeat(gmlp_b_s[l].T, B_WIDTH // B_GROUPS, axis=1).astype(F32),
        'dw_w': conv_dw_w[l].astype(F32), 'dw_b': row(conv_dw_b[l]),
        'conv_ln_g': row(conv_ln_g[l]), 'conv_ln_b': row(conv_ln_b[l]),
        'pw_b': row(conv_pw_b[l]), 'norm2_g': row(norm2_g[l]),
    })
    return lw


def kernel(x, c, ctx, c_ctx, ada_w, ada_b, norm1_g, w_in, hgrn_lb_logits, hgrn_onorm_g, gmlp_ln_g, gmlp_ln_b, gmlp_w_s, gmlp_b_s, conv_dw_w, conv_dw_b, conv_ln_g, conv_ln_b, conv_pw_w, conv_pw_b, w_out, norm2_g, ffn_w13, ffn_w2, final_norm_g):
    bsz, t_len, d = x.shape
    ctx_len = ctx.shape[1]
    depth = ada_w.shape[0]
    lb_logits = hgrn_lb_logits.astype(F32)

    cs = jnp.concatenate([c, c_ctx[None, :], jnp.zeros((MOD_ROWS - bsz - 1, d), F32)], axis=0)
    mod_all = _modulation(cs, ada_w, ada_b)
    grp = jnp.arange(B_WIDTH) // (B_WIDTH // B_GROUPS)
    gavg = ((grp[:, None] == grp[None, :]).astype(F32) / (B_WIDTH // B_GROUPS)).astype(BF16)
    zero_state = jnp.zeros((bsz, 2, A_HEADS, A_DK, A_DK), F32)
    mxu_weights = {
        'w_in': w_in.astype(BF16), 'w_out': w_out.astype(BF16),
        'ffn_w13': ffn_w13.astype(BF16), 'ffn_w2': ffn_w2.astype(BF16),
        'pw_w': conv_pw_w.astype(BF16),
        'ws_cat': gmlp_w_s.reshape(depth, B_GROUPS * B_CHUNK, B_CHUNK).astype(BF16),
    }
    row_len = t_len // (t_len // GRID_W)
    ctx_flat = ctx.reshape(1, bsz * ctx_len, d)
    ctx_tile = min(LATENT_TILE, bsz * ctx_len)

    for l in range(depth):
        last = l == depth - 1
        lw = _layer_params(l, gavg, mxu_weights, norm1_g, hgrn_onorm_g, gmlp_ln_g, gmlp_ln_b, gmlp_b_s,
                           conv_dw_w, conv_dw_b, conv_ln_g, conv_ln_b, conv_pw_b, norm2_g)
        mod = mod_all[l, :bsz].reshape(bsz, 6, d)
        mod_ctx = mod_all[l, bsz].reshape(1, 6, d)
        if last:
            pa_c = _in_proj(ctx_flat, mod_ctx, lw, ctx_tile)
            _, _, state = _hgrn(pa_c.reshape(bsz, ctx_len, PA_COLS), lb_logits, zero_state, l, ctx_len)
        else:
            pa_c, gate_c, ybc_c = _in_proj_mix(ctx_flat, mod_ctx, lw, ctx_tile, ctx_len)
            of_c, ob_c, state = _hgrn(pa_c.reshape(bsz, ctx_len, PA_COLS), lb_logits, zero_state, l, ctx_len)
            ctx_flat = _out_ffn(ctx_flat, of_c.reshape(1, -1, A_WIDTH), ob_c.reshape(1, -1, A_WIDTH),
                                gate_c, ybc_c, mod_ctx, lw, final_norm_g, ctx_tile, False)
        pa, gate, ybc = _in_proj_mix(x, mod, lw, LATENT_TILE, row_len)
        o_f, o_b, _ = _hgrn(pa, lb_logits, state, l, LATENT_TILE)
        x = _out_ffn(x, o_f, o_b, gate, ybc, mod, lw, final_norm_g, LATENT_TILE, last)
    return x
```

```python
import functools
from typing import NamedTuple

import jax
import jax.numpy as jnp
from jax import lax
from jax.experimental import pallas as pl
from jax.experimental.pallas import tpu as pltpu

F32 = jnp.float32
BF16 = jnp.bfloat16

D_MODEL = 1024
GRID_W = 64
A_HEADS = 4
A_DK = 128
A_WIDTH = A_HEADS * A_DK
A_CHUNK = 64
A_SUB = 32
HGRN_GROUP = 1
HGRN_LAG = 1
B_GROUPS = 4
B_WIDTH = 256
B_CHUNK = 128
C_GROUPS = 4
C_WIDTH = 256
C_KERNEL = 31
C_PAD = 16
SUBLANES = 8
PROJ_CHUNK = 256
FFN_CHUNK = 256
PA_COLS = 4 * A_WIDTH
EPS = 1e-6
EXP2_CLAMP = 80.0
LOG2E = 1.4426950408889634
MOD_ROWS = 8
LATENT_TILE = 512
SCAN_TILE = 1024

V7X_VMEM_LIMIT = 56 * 1024 * 1024

_NT = (((1,), (1,)), ((), ()))
_TN = (((0,), (0,)), ((), ()))
_FINISHED = object()


def _cparams(sem):
    return pltpu.CompilerParams(dimension_semantics=sem, vmem_limit_bytes=V7X_VMEM_LIMIT)


def _full(shape):
    n = len(shape)
    return pl.BlockSpec(shape, lambda *_: (0,) * n)


class _LayerSlice(NamedTuple):
    stacked: jax.Array
    layer: int

    @property
    def shape(self):
        return self.stacked.shape[1:]


def _param_spec(p, block=None):
    if not isinstance(p, _LayerSlice):
        return _full(p.shape)
    block = p.shape if block is None else block
    index = (p.layer,) + (0,) * len(block)
    return pl.BlockSpec((None,) + tuple(block), lambda *_: index, pipeline_mode=pl.Buffered(1))


def _param_arg(p):
    return p.stacked if isinstance(p, _LayerSlice) else p


def _round_robin(branches, between=None):
    while branches:
        branches = [g for g in branches if next(g, _FINISHED) is not _FINISHED]
        if between is not None:
            between()


def _mod_kernel(cs_ref, w_ref, b_ref, o_ref):
    s = cs_ref[...]
    s = s * jax.nn.sigmoid(s)
    w = w_ref[0]
    s_hi = s.astype(BF16)
    s_lo = (s - s_hi.astype(F32)).astype(BF16)
    w_hi = w.astype(BF16)
    w_lo = (w - w_hi.astype(F32)).astype(BF16)
    r = jnp.dot(jnp.concatenate([s_hi, s_lo], axis=0), w_hi, preferred_element_type=F32)
    o_ref[0] = (r[:MOD_ROWS] + r[MOD_ROWS:] + jnp.dot(s_hi, w_lo, preferred_element_type=F32)
                + b_ref[0])


def _modulation(cs, ada_w, ada_b):
    depth, d, n = ada_w.shape
    tn = 1024
    return pl.pallas_call(
        _mod_kernel,
        grid=(depth, n // tn),
        in_specs=[
            pl.BlockSpec((MOD_ROWS, d), lambda l, j: (0, 0)),
            pl.BlockSpec((1, d, tn), lambda l, j: (l, 0, j)),
            pl.BlockSpec((1, 1, tn), lambda l, j: (l, 0, j)),
        ],
        out_specs=pl.BlockSpec((1, MOD_ROWS, tn), lambda l, j: (l, 0, j)),
        out_shape=jax.ShapeDtypeStruct((depth, MOD_ROWS, n), F32),
        compiler_params=_cparams(("arbitrary", "arbitrary")),
        name="modulation",
    )(cs, ada_w, ada_b.reshape(depth, 1, n))


def _norm_modulate(x, g, shift, scale):
    ms = jnp.mean(x * x, axis=-1, keepdims=True)
    y = x * lax.rsqrt(ms + EPS) * g
    return y * (1.0 + scale) + shift


def _group_mean(x, gavg):
    return jnp.dot(x.astype(BF16), gavg, preferred_element_type=F32)


def _group_layer_norm_stages(x, gavg, g, b, out):
    mu = _group_mean(x, gavg)
    yield
    xc = x - mu
    var = _group_mean(xc * xc, gavg)
    yield
    out['ln'] = xc * lax.rsqrt(var + EPS) * g + b


def _gmlp_stages(project, gavg, ln_g, ln_b, ws_ref, bs_ref, out):
    u = project(0)
    v = project(B_WIDTH)
    tile = u.shape[0]
    yield
    ln = {}
    yield from _group_layer_norm_stages(jax.nn.gelu(v), gavg, ln_g, ln_b, ln)
    vn = ln['ln'].astype(BF16)
    gu = jax.nn.gelu(u)
    lane_group = lax.broadcasted_iota(jnp.int32, (B_CHUNK, B_WIDTH), 1) // (B_WIDTH // B_GROUPS)
    mixed = []
    for n in range(tile // B_CHUNK):
        yield
        r = jnp.dot(ws_ref[...], vn[n * B_CHUNK:(n + 1) * B_CHUNK], preferred_element_type=F32)
        m = bs_ref[...]
        for g in range(B_GROUPS):
            m = m + jnp.where(lane_group == g, r[g * B_CHUNK:(g + 1) * B_CHUNK], 0.0)
        mixed.append(m)
    out['yb'] = gu * jnp.concatenate(mixed, axis=0)


def _conv_stages(ca, cgate, gavg, dww_ref, dw_b, ln_g, ln_b, pww_ref, pw_b, zpad_ref, zsh_ref, seg, out):
    tile = ca.shape[0]
    z = ca * jax.nn.sigmoid(cgate)
    nseg = tile // seg
    pad_zeros = jnp.zeros((C_PAD, C_WIDTH), F32)
    for s in range(nseg):
        zpad_ref[s, 0:C_PAD, :] = pad_zeros
        zpad_ref[s, C_PAD + seg:2 * C_PAD + seg, :] = pad_zeros
        zpad_ref[s, C_PAD:C_PAD + seg, :] = z[s * seg:(s + 1) * seg]
    span = seg + 2 * C_PAD - SUBLANES
    convs = []
    for s in range(nseg):
        for r in range(1, SUBLANES):
            zsh_ref[s, r] = zpad_ref[s, r:r + span, :]
        acc = jnp.zeros((seg, C_WIDTH), F32)
        for j in range(C_KERNEL):
            start = C_PAD - C_KERNEL // 2 + j
            r, a = start % SUBLANES, start - start % SUBLANES
            tap = zpad_ref[s, a:a + seg, :] if r == 0 else zsh_ref[s, r, a:a + seg, :]
            acc = acc + dww_ref[j:j + 1, :] * tap
        convs.append(acc)
        yield
    ln = {}
    yield from _group_layer_norm_stages(jnp.concatenate(convs, axis=0) + dw_b, gavg, ln_g, ln_b, ln)
    zc = ln['ln']
    zc = (zc * jax.nn.sigmoid(zc)).astype(BF16)
    out['yc'] = jnp.dot(zc, pww_ref[...], preferred_element_type=F32) + pw_b


_MIX_PARAMS = ('norm1_g', 'w_in', 'gavg', 'gmlp_ln_g', 'gmlp_ln_b', 'ws_cat', 'bs_tile',
               'dw_w', 'dw_b', 'conv_ln_g', 'conv_ln_b', 'pw_w', 'pw_b')


def _in_proj_kernel(x_ref, mod_ref, g_ref, w_ref, pa_ref):
    h = _norm_modulate(x_ref[0], g_ref[...], mod_ref[0, 0:1, :], mod_ref[0, 1:2, :]).astype(BF16)
    pa_ref[0] = jnp.dot(h, w_ref[...], preferred_element_type=F32)


def _in_proj(x, mod, lw, tile):
    bsz, t_len, d = x.shape
    return pl.pallas_call(
        _in_proj_kernel,
        grid=(bsz, t_len // tile),
        in_specs=[
            pl.BlockSpec((1, tile, d), lambda b, i: (b, i, 0)),
            pl.BlockSpec((1, 6, d), lambda b, i: (b, 0, 0)),
            _full((1, d)),
            _param_spec(lw['w_in'], (d, PA_COLS)),
        ],
        out_specs=pl.BlockSpec((1, tile, PA_COLS), lambda b, i: (b, i, 0)),
        out_shape=jax.ShapeDtypeStruct((bsz, t_len, PA_COLS), F32),
        compiler_params=_cparams(("arbitrary", "arbitrary")),
        name="in_proj",
    )(x, mod, lw['norm1_g'], _param_arg(lw['w_in']))


def _in_proj_mix_kernel(x_ref, mod_ref, *refs, seg):
    p = dict(zip(_MIX_PARAMS, refs))
    pa_ref, gate_ref, ybc_ref, zpad_ref, zsh_ref = refs[len(_MIX_PARAMS):]
    h = _norm_modulate(x_ref[0], p['norm1_g'][...], mod_ref[0, 0:1, :], mod_ref[0, 1:2, :]).astype(BF16)
    gavg = p['gavg'][...]

    def proj(lo, n):
        return jnp.dot(h, p['w_in'][:, lo:lo + n], preferred_element_type=F32)

    chunks = [(pa_ref, lo) for lo in range(0, PA_COLS, PROJ_CHUNK)]
    chunks += [(gate_ref, lo) for lo in range(0, A_WIDTH, PROJ_CHUNK)]
    pending = iter(chunks)

    def project_next():
        nxt = next(pending, None)
        if nxt is not None:
            ref, lo = nxt
            w_lo = lo if ref is pa_ref else PA_COLS + lo
            ref[0, :, lo:lo + PROJ_CHUNK] = proj(w_lo, PROJ_CHUNK)

    b0 = PA_COLS + A_WIDTH
    c0 = b0 + 2 * B_WIDTH
    out = {}
    branches = [
        _conv_stages(proj(c0, C_WIDTH), proj(c0 + C_WIDTH, C_WIDTH), gavg, p['dw_w'], p['dw_b'][...],
                     p['conv_ln_g'][...], p['conv_ln_b'][...], p['pw_w'], p['pw_b'][...],
                     zpad_ref, zsh_ref, seg, out),
        _gmlp_stages(lambda lo: proj(b0 + lo, B_WIDTH), gavg, p['gmlp_ln_g'][...], p['gmlp_ln_b'][...],
                     p['ws_cat'], p['bs_tile'], out),
    ]
    _round_robin(branches, project_next)
    ybc_ref[0] = jnp.concatenate([out['yb'], out['yc']], axis=1).astype(BF16)
    for _ in chunks:
        project_next()


def _in_proj_mix(x, mod, lw, tile, seg):
    bsz, t_len, d = x.shape
    nseg = tile // seg

    def tok(cols):
        return pl.BlockSpec((1, tile, cols), lambda b, i: (b, i, 0))

    params = [lw[name] for name in _MIX_PARAMS]
    return pl.pallas_call(
        functools.partial(_in_proj_mix_kernel, seg=seg),
        grid=(bsz, t_len // tile),
        in_specs=[tok(d), pl.BlockSpec((1, 6, d), lambda b, i: (b, 0, 0))] + [_param_spec(p) for p in params],
        out_specs=[tok(PA_COLS), tok(A_WIDTH), tok(B_WIDTH + C_WIDTH)],
        out_shape=[jax.ShapeDtypeStruct((bsz, t_len, PA_COLS), F32),
                   jax.ShapeDtypeStruct((bsz, t_len, A_WIDTH), F32),
                   jax.ShapeDtypeStruct((bsz, t_len, B_WIDTH + C_WIDTH), BF16)],
        scratch_shapes=[pltpu.VMEM((nseg, seg + 2 * C_PAD, C_WIDTH), F32),
                        pltpu.VMEM((nseg, SUBLANES, seg + 2 * C_PAD - SUBLANES, C_WIDTH), F32)],
        compiler_params=_cparams(("arbitrary", "arbitrary")),
        name="in_proj_mix",
    )(x, mod, *[_param_arg(p) for p in params])


def _lower_bounds(logits, layer):
    depth = logits.shape[0]
    rows = [logits[j] for j in range(depth)]
    m = rows[0]
    for r in rows[1:]:
        m = jnp.maximum(m, r)
    es = [jnp.exp(r - m) for r in rows]
    tot = es[0]
    for e in es[1:]:
        tot = tot + e
    lb = jnp.zeros_like(m)
    for j in range(1, layer + 1):
        lb = lb + es[j] / tot
    return lb


def _chunk_rows(c, lo=0, n=A_CHUNK):
    return slice(c * A_CHUNK + lo, c * A_CHUNK + lo + n)


def _hgrn_gates(z, lb, rev, n_chunks):
    row = lax.broadcasted_iota(jnp.int32, (A_CHUNK, A_CHUNK), 0)
    col = lax.broadcasted_iota(jnp.int32, (A_CHUNK, A_CHUNK), 1)
    tri = ((col >= row) if rev else (col <= row)).astype(F32).astype(BF16)
    oml = 1.0 - lb
    t = oml * jax.nn.sigmoid(z)
    lf = jnp.log(lb + t)
    kk = oml - t
    hi = lf.astype(BF16)
    lo = (lf - hi.astype(F32)).astype(BF16)
    bs = [(jnp.dot(tri, hi[_chunk_rows(c)], preferred_element_type=F32)
           + jnp.dot(tri, lo[_chunk_rows(c)], preferred_element_type=F32)) * LOG2E
          for c in range(n_chunks)]
    return kk, bs


def _hgrn_factors(q, v, kk, bs, rev):
    sub = A_SUB
    wide, narrow = (0, 1) if rev else (1, 0)
    mid_row = sub // 2 if rev else sub // 2 - 1
    last_row = 0 if rev else A_CHUNK - 1
    halves = [slice(i * sub, (i + 1) * sub) for i in range(2)]
    per_chunk = {name: [] for name in ('qe', 'kd', 'qd', 'k_wide', 'k_narrow')}
    decs = []
    for c, bc in enumerate(bs):
        qc, kc = q[_chunk_rows(c)], kk[_chunk_rows(c)]
        mids = [bc[i * sub + mid_row:i * sub + mid_row + 1] for i in range(2)]
        last = bc[last_row:last_row + 1]
        decs.append(jnp.exp2(last))
        per_chunk['qe'].append((qc * jnp.exp2(bc)).astype(BF16))
        per_chunk['kd'].append((kc * jnp.exp2(last - bc)).astype(BF16))
        per_chunk['qd'].append(jnp.concatenate(
            [(qc[h] * jnp.exp2(jnp.minimum(bc[h] - mids[i], EXP2_CLAMP))).astype(BF16)
             for i, h in enumerate(halves)], axis=0))
        per_chunk['k_wide'].append((kc * jnp.exp2(jnp.minimum(mids[wide] - bc, EXP2_CLAMP))).astype(BF16))
        hn = halves[narrow]
        per_chunk['k_narrow'].append(
            (kc[hn] * jnp.exp2(jnp.minimum(mids[narrow] - bc[hn], EXP2_CLAMP))).astype(BF16))
    f = {name: jnp.concatenate(vals, axis=0) for name, vals in per_chunk.items()}
    f['dec'] = decs
    f['v'] = v.astype(BF16)
    return f


def _hgrn_scores(f, rev, c):
    sub = A_SUB
    wide, narrow = (0, 1) if rev else (1, 0)
    row_n = lax.broadcasted_iota(jnp.int32, (sub, sub), 0)
    col_n = lax.broadcasted_iota(jnp.int32, (sub, sub), 1)
    row_w = lax.broadcasted_iota(jnp.int32, (sub, A_CHUNK), 0)
    col_w = lax.broadcasted_iota(jnp.int32, (sub, A_CHUNK), 1)
    mask_n = (col_n >= row_n) if rev else (col_n <= row_n)
    mask_w = (col_w >= row_w) if rev else (col_w <= row_w + sub)
    per_head = []
    for h in range(A_HEADS):
        hc = slice(h * A_DK, (h + 1) * A_DK)
        s_n = lax.dot_general(f['qd'][_chunk_rows(c, narrow * sub, sub), hc],
                              f['k_narrow'][c * sub:(c + 1) * sub, hc], _NT, preferred_element_type=F32)
        s_w = lax.dot_general(f['qd'][_chunk_rows(c, wide * sub, sub), hc],
                              f['k_wide'][_chunk_rows(c), hc], _NT, preferred_element_type=F32)
        per_head.append((jnp.where(mask_n, s_n, 0.0).astype(BF16),
                         jnp.where(mask_w, s_w, 0.0).astype(BF16)))
    return per_head


def _hgrn_intra(f, scores, rev, c):
    sub = A_SUB
    narrow = 1 if rev else 0
    per_head = []
    for h in range(A_HEADS):
        hc = slice(h * A_DK, (h + 1) * A_DK)
        s_n, s_w = scores[h]
        o_n = jnp.dot(s_n, f['v'][_chunk_rows(c, narrow * sub, sub), hc], preferred_element_type=F32)
        o_w = jnp.dot(s_w, f['v'][_chunk_rows(c), hc], preferred_element_type=F32)
        intra = jnp.concatenate([o_w, o_n] if rev else [o_n, o_w], axis=0)
        upd = lax.dot_general(f['v'][_chunk_rows(c), hc], f['kd'][_chunk_rows(c), hc], _TN,
                              preferred_element_type=F32)
        per_head.append((intra, upd))
    return per_head


def _hgrn_stages(q, z, v, lb, rev, n_chunks, get_state, set_state, put_out, group=HGRN_GROUP, lag=HGRN_LAG):
    n_groups = n_chunks // group
    order = list(range(n_groups - 1, -1, -1) if rev else range(n_groups))
    facs, scores, intra = {}, {}, {}
    states = [get_state(h) for h in range(A_HEADS)]
    for step in range(n_groups + 3 * lag):
        if step < n_groups:
            g = order[step]
            rows = slice(g * group * A_CHUNK, (g + 1) * group * A_CHUNK)
            facs[g] = _hgrn_factors(q[rows], v[rows], *_hgrn_gates(z[rows], lb, rev, group), rev)
        if 0 <= step - lag < n_groups:
            g = order[step - lag]
            scores[g] = [_hgrn_scores(facs[g], rev, c) for c in range(group)]
        if 0 <= step - 2 * lag < n_groups:
            g = order[step - 2 * lag]
            intra[g] = [_hgrn_intra(facs[g], scores[g][c], rev, c) for c in range(group)]
        if 0 <= step - 3 * lag < n_groups:
            g = order[step - 3 * lag]
            f = facs[g]
            for c in (range(group - 1, -1, -1) if rev else range(group)):
                outs = []
                for h in range(A_HEADS):
                    hc = slice(h * A_DK, (h + 1) * A_DK)
                    inter = lax.dot_general(f['qe'][_chunk_rows(c), hc], states[h].astype(BF16), _NT,
                                            preferred_element_type=F32)
                    outs.append(inter + intra[g][c][h][0])
                    states[h] = states[h] * f['dec'][c][:, hc] + intra[g][c][h][1]
                put_out(g * group + c, jnp.concatenate(outs, axis=1))
        yield
    for h in range(A_HEADS):
        set_state(h, states[h])


def _hgrn_kernel(qf_ref, zf_ref, vf_ref, qb_ref, zb_ref, vb_ref, lbl_ref, s0_ref,
                 of_ref, ob_ref, st_ref, *, layer, n_chunks):
    @pl.when(pl.program_id(1) == 0)
    def _():
        st_ref[...] = s0_ref[...]

    lb = _lower_bounds(lbl_ref[...], layer)

    def direction(d, q_ref, z_ref, v_ref, o_ref):
        def set_state(h, val):
            st_ref[0, d, h] = val

        def put_out(c, val):
            o_ref[0, _chunk_rows(c), :] = val

        return _hgrn_stages(q_ref[0], z_ref[0], v_ref[0], lb[d:d + 1], d == 1, n_chunks,
                            lambda h: st_ref[0, d, h], set_state, put_out)

    _round_robin([direction(0, qf_ref, zf_ref, vf_ref, of_ref),
                  direction(1, qb_ref, zb_ref, vb_ref, ob_ref)])


def _hgrn(pa, lb_logits, s0, layer, tile):
    bsz, t_len, _ = pa.shape
    nb = t_len // tile
    depth = lb_logits.shape[0]

    def col_spec(colblk, rev):
        if rev:
            return pl.BlockSpec((1, tile, A_WIDTH), lambda b, i: (b, nb - 1 - i, colblk))
        return pl.BlockSpec((1, tile, A_WIDTH), lambda b, i: (b, i, colblk))

    st_spec = pl.BlockSpec((1, 2, A_HEADS, A_DK, A_DK), lambda b, i: (b, 0, 0, 0, 0))
    o_sds = jax.ShapeDtypeStruct((bsz, t_len, A_WIDTH), F32)
    return pl.pallas_call(
        functools.partial(_hgrn_kernel, layer=layer, n_chunks=tile // A_CHUNK),
        grid=(bsz, nb),
        in_specs=[col_spec(0, False), col_spec(1, False), col_spec(3, False),
                  col_spec(0, True), col_spec(2, True), col_spec(3, True),
                  _full((depth, 2, A_WIDTH)), st_spec],
        out_specs=[col_spec(0, False), col_spec(0, True), st_spec],
        out_shape=[o_sds, o_sds, jax.ShapeDtypeStruct(s0.shape, F32)],
        compiler_params=_cparams(("arbitrary", "arbitrary")),
        name="hgrn",
    )(pa, pa, pa, pa, pa, pa, lb_logits, s0)


def _out_ffn_kernel(x_ref, of_ref, ob_ref, gate_ref, ybc_ref, mod_ref, og_ref, wo_ref,
                    g_ref, w13_ref, w2_ref, fg_ref, o_ref, *, d_ff, chunk, final_norm):
    o = of_ref[0] + ob_ref[0]
    parts = []
    for h in range(A_HEADS):
        oh = o[:, h * A_DK:(h + 1) * A_DK]
        parts.append(oh * lax.rsqrt(jnp.mean(oh * oh, axis=-1, keepdims=True) + EPS))
    gate = gate_ref[0]
    ya = jnp.concatenate(parts, axis=1) * og_ref[...] * (gate * jax.nn.sigmoid(gate))
    y = (jnp.dot(ya.astype(BF16), wo_ref[0:A_WIDTH, :], preferred_element_type=F32)
         + jnp.dot(ybc_ref[0], wo_ref[A_WIDTH:, :], preferred_element_type=F32))
    x = x_ref[0] + mod_ref[0, 2:3, :] * y
    h = _norm_modulate(x, g_ref[...], mod_ref[0, 3:4, :], mod_ref[0, 4:5, :]).astype(BF16)
    acts = []
    for c in range(d_ff // chunk):
        a = jnp.dot(h, w13_ref[:, c * chunk:(c + 1) * chunk], preferred_element_type=F32)
        b = jnp.dot(h, w13_ref[:, d_ff + c * chunk:d_ff + (c + 1) * chunk], preferred_element_type=F32)
        acts.append((a * jax.nn.sigmoid(a) * b).astype(BF16))
    acc = jnp.dot(jnp.concatenate(acts, axis=1), w2_ref[...], preferred_element_type=F32)
    y = x + mod_ref[0, 5:6, :] * acc
    if final_norm:
        y = y * lax.rsqrt(jnp.mean(y * y, axis=-1, keepdims=True) + EPS) * fg_ref[...]
    o_ref[0] = y


def _out_ffn(x, o_f, o_b, gate, ybc, mod, lw, final_g, tile, final_norm):
    bsz, t_len, d = x.shape
    d_ff = lw['ffn_w2'].shape[0]

    def tok(cols):
        return pl.BlockSpec((1, tile, cols), lambda b, i: (b, i, 0))

    params = [lw['onorm_g'], lw['w_out'], lw['norm2_g'], lw['ffn_w13'], lw['ffn_w2'], final_g.reshape(1, d)]
    return pl.pallas_call(
        functools.partial(_out_ffn_kernel, d_ff=d_ff, chunk=FFN_CHUNK, final_norm=final_norm),
        grid=(bsz, t_len // tile),
        in_specs=[tok(d), tok(A_WIDTH), tok(A_WIDTH), tok(A_WIDTH), tok(B_WIDTH + C_WIDTH),
                  pl.BlockSpec((1, 6, d), lambda b, i: (b, 0, 0))] + [_param_spec(p) for p in params],
        out_specs=tok(d),
        out_shape=jax.ShapeDtypeStruct(x.shape, F32),
        compiler_params=_cparams(("arbitrary", "arbitrary")),
        name="out_ffn",
    )(x, o_f, o_b, gate, ybc, mod, *[_param_arg(p) for p in params])


def _layer_params(l, gavg, mxu_weights, norm1_g, hgrn_onorm_g, gmlp_ln_g, gmlp_ln_b, gmlp_b_s,
                  conv_dw_w, conv_dw_b, conv_ln_g, conv_ln_b, conv_pw_b, norm2_g):
    row = lambda a: a.reshape(1, -1).astype(F32)
    lw = {name: _LayerSlice(w, l) for name, w in mxu_weights.items()}
    lw.update({
        'norm1_g': row(norm1_g[l]), 'onorm_g': row(hgrn_onorm_g[l]), 'gavg': gavg,
        'gmlp_ln_g': row(gmlp_ln_g[l]), 'gmlp_ln_b': row(gmlp_ln_b[l]),
        'bs_tile': jnp.repeat(gmlp_b_s[l].T, B_WIDTH // B_GROUPS, axis=1).astype(F32),
        'dw_w': conv_dw_w[l].astype(F32), 'dw_b': row(conv_dw_b[l]),
        'conv_ln_g': row(conv_ln_g[l]), 'conv_ln_b': row(conv_ln_b[l]),
        'pw_b': row(conv_pw_b[l]), 'norm2_g': row(norm2_g[l]),
    })
    return lw


def kernel(x, c, ctx, c_ctx, ada_w, ada_b, norm1_g, w_in, hgrn_lb_logits, hgrn_onorm_g, gmlp_ln_g, gmlp_ln_b, gmlp_w_s, gmlp_b_s, conv_dw_w, conv_dw_b, conv_ln_g, conv_ln_b, conv_pw_w, conv_pw_b, w_out, norm2_g, ffn_w13, ffn_w2, final_norm_g):
    bsz, t_len, d = x.shape
    ctx_len = ctx.shape[1]
    depth = ada_w.shape[0]
    lb_logits = hgrn_lb_logits.astype(F32)

    cs = jnp.concatenate([c, c_ctx[None, :], jnp.zeros((MOD_ROWS - bsz - 1, d), F32)], axis=0)
    mod_all = _modulation(cs, ada_w, ada_b)
    grp = jnp.arange(B_WIDTH) // (B_WIDTH // B_GROUPS)
    gavg = ((grp[:, None] == grp[None, :]).astype(F32) / (B_WIDTH // B_GROUPS)).astype(BF16)
    zero_state = jnp.zeros((bsz, 2, A_HEADS, A_DK, A_DK), F32)
    mxu_weights = {
        'w_in': w_in.astype(BF16), 'w_out': w_out.astype(BF16),
        'ffn_w13': ffn_w13.astype(BF16), 'ffn_w2': ffn_w2.astype(BF16),
        'pw_w': conv_pw_w.astype(BF16),
        'ws_cat': gmlp_w_s.reshape(depth, B_GROUPS * B_CHUNK, B_CHUNK).astype(BF16),
    }
    row_len = t_len // (t_len // GRID_W)
    ctx_flat = ctx.reshape(1, bsz * ctx_len, d)
    ctx_tile = min(LATENT_TILE, bsz * ctx_len)

    for l in range(depth):
        last = l == depth - 1
        lw = _layer_params(l, gavg, mxu_weights, norm1_g, hgrn_onorm_g, gmlp_ln_g, gmlp_ln_b, gmlp_b_s,
                           conv_dw_w, conv_dw_b, conv_ln_g, conv_ln_b, conv_pw_b, norm2_g)
        mod = mod_all[l, :bsz].reshape(bsz, 6, d)
        mod_ctx = mod_all[l, bsz].reshape(1, 6, d)
        if last:
            pa_c = _in_proj(ctx_flat, mod_ctx, lw, ctx_tile)
            _, _, state = _hgrn(pa_c.reshape(bsz, ctx_len, PA_COLS), lb_logits, zero_state, l, ctx_len)
        else:
            pa_c, gate_c, ybc_c = _in_proj_mix(ctx_flat, mod_ctx, lw, ctx_tile, ctx_len)
            of_c, ob_c, state = _hgrn(pa_c.reshape(bsz, ctx_len, PA_COLS), lb_logits, zero_state, l, ctx_len)
            ctx_flat = _out_ffn(ctx_flat, of_c.reshape(1, -1, A_WIDTH), ob_c.reshape(1, -1, A_WIDTH),
                                gate_c, ybc_c, mod_ctx, lw, final_norm_g, ctx_tile, False)
        pa, gate, ybc = _in_proj_mix(x, mod, lw, LATENT_TILE, row_len)
        o_f, o_b, _ = _hgrn(pa, lb_logits, state, l, SCAN_TILE)
        x = _out_ffn(x, o_f, o_b, gate, ybc, mod, lw, final_norm_g, LATENT_TILE, last)
    return x
```

```python
import functools
from typing import NamedTuple

import jax
import jax.numpy as jnp
from jax import lax
from jax.experimental import pallas as pl
from jax.experimental.pallas import tpu as pltpu

F32 = jnp.float32
BF16 = jnp.bfloat16

D_MODEL = 1024
GRID_W = 64
A_HEADS = 4
A_DK = 128
A_WIDTH = A_HEADS * A_DK
A_CHUNK = 64
A_SUB = 32
HGRN_GROUP = 1
HGRN_LAG = 1
B_GROUPS = 4
B_WIDTH = 256
B_CHUNK = 128
C_GROUPS = 4
C_WIDTH = 256
C_KERNEL = 31
C_PAD = 16
SUBLANES = 8
PROJ_CHUNK = 256
FFN_CHUNK = 256
PA_COLS = 4 * A_WIDTH
EPS = 1e-6
EXP2_CLAMP = 80.0
LOG2E = 1.4426950408889634
MOD_ROWS = 8
LATENT_TILE = 512
SCAN_TILE = 1024

V7X_VMEM_LIMIT = 56 * 1024 * 1024

_NT = (((1,), (1,)), ((), ()))
_TN = (((0,), (0,)), ((), ()))
_FINISHED = object()


def _cparams(sem):
    return pltpu.CompilerParams(dimension_semantics=sem, vmem_limit_bytes=V7X_VMEM_LIMIT)


def _full(shape):
    n = len(shape)
    return pl.BlockSpec(shape, lambda *_: (0,) * n)


class _LayerSlice(NamedTuple):
    stacked: jax.Array
    layer: int

    @property
    def shape(self):
        return self.stacked.shape[1:]


def _param_spec(p, block=None):
    if not isinstance(p, _LayerSlice):
        return _full(p.shape)
    block = p.shape if block is None else block
    index = (p.layer,) + (0,) * len(block)
    return pl.BlockSpec((None,) + tuple(block), lambda *_: index, pipeline_mode=pl.Buffered(1))


def _param_arg(p):
    return p.stacked if isinstance(p, _LayerSlice) else p


def _round_robin(branches, between=None):
    while branches:
        branches = [g for g in branches if next(g, _FINISHED) is not _FINISHED]
        if between is not None:
            between()


def _mod_kernel(cs_ref, w_ref, b_ref, o_ref):
    s = cs_ref[...]
    s = s * jax.nn.sigmoid(s)
    w = w_ref[0]
    s_hi = s.astype(BF16)
    s_lo = (s - s_hi.astype(F32)).astype(BF16)
    w_hi = w.astype(BF16)
    w_lo = (w - w_hi.astype(F32)).astype(BF16)
    r = jnp.dot(jnp.concatenate([s_hi, s_lo], axis=0), w_hi, preferred_element_type=F32)
    o_ref[0] = (r[:MOD_ROWS] + r[MOD_ROWS:] + jnp.dot(s_hi, w_lo, preferred_element_type=F32)
                + b_ref[0])


def _modulation(cs, ada_w, ada_b):
    depth, d, n = ada_w.shape
    tn = 1024
    return pl.pallas_call(
        _mod_kernel,
        grid=(depth, n // tn),
        in_specs=[
            pl.BlockSpec((MOD_ROWS, d), lambda l, j: (0, 0)),
            pl.BlockSpec((1, d, tn), lambda l, j: (l, 0, j)),
            pl.BlockSpec((1, 1, tn), lambda l, j: (l, 0, j)),
        ],
        out_specs=pl.BlockSpec((1, MOD_ROWS, tn), lambda l, j: (l, 0, j)),
        out_shape=jax.ShapeDtypeStruct((depth, MOD_ROWS, n), F32),
        compiler_params=_cparams(("arbitrary", "arbitrary")),
        name="modulation",
    )(cs, ada_w, ada_b.reshape(depth, 1, n))


def _norm_modulate(x, g, shift, scale):
    ms = jnp.mean(x * x, axis=-1, keepdims=True)
    y = x * lax.rsqrt(ms + EPS) * g
    return y * (1.0 + scale) + shift


def _group_mean(x, gavg):
    return jnp.dot(x.astype(BF16), gavg, preferred_element_type=F32)


def _group_layer_norm_stages(x, gavg, g, b, out):
    mu = _group_mean(x, gavg)
    yield
    xc = x - mu
    var = _group_mean(xc * xc, gavg)
    yield
    out['ln'] = xc * lax.rsqrt(var + EPS) * g + b


def _gmlp_stages(project, gavg, ln_g, ln_b, ws_ref, bs_ref, out):
    u = project(0)
    v = project(B_WIDTH)
    tile = u.shape[0]
    yield
    ln = {}
    yield from _group_layer_norm_stages(jax.nn.gelu(v), gavg, ln_g, ln_b, ln)
    vn = ln['ln'].astype(BF16)
    gu = jax.nn.gelu(u)
    lane_group = lax.broadcasted_iota(jnp.int32, (B_CHUNK, B_WIDTH), 1) // (B_WIDTH // B_GROUPS)
    mixed = []
    for n in range(tile // B_CHUNK):
        yield
        r = jnp.dot(ws_ref[...], vn[n * B_CHUNK:(n + 1) * B_CHUNK], preferred_element_type=F32)
        m = bs_ref[...]
        for g in range(B_GROUPS):
            m = m + jnp.where(lane_group == g, r[g * B_CHUNK:(g + 1) * B_CHUNK], 0.0)
        mixed.append(m)
    out['yb'] = gu * jnp.concatenate(mixed, axis=0)


def _conv_stages(ca, cgate, gavg, dww_ref, dw_b, ln_g, ln_b, pww_ref, pw_b, zpad_ref, zsh_ref, seg, out):
    tile = ca.shape[0]
    z = ca * jax.nn.sigmoid(cgate)
    nseg = tile // seg
    pad_zeros = jnp.zeros((C_PAD, C_WIDTH), F32)
    for s in range(nseg):
        zpad_ref[s, 0:C_PAD, :] = pad_zeros
        zpad_ref[s, C_PAD + seg:2 * C_PAD + seg, :] = pad_zeros
        zpad_ref[s, C_PAD:C_PAD + seg, :] = z[s * seg:(s + 1) * seg]
    span = seg + 2 * C_PAD - SUBLANES
    convs = []
    for s in range(nseg):
        for r in range(1, SUBLANES):
            zsh_ref[s, r] = zpad_ref[s, r:r + span, :]
        acc = jnp.zeros((seg, C_WIDTH), F32)
        for j in range(C_KERNEL):
            start = C_PAD - C_KERNEL // 2 + j
            r, a = start % SUBLANES, start - start % SUBLANES
            tap = zpad_ref[s, a:a + seg, :] if r == 0 else zsh_ref[s, r, a:a + seg, :]
            acc = acc + dww_ref[j:j + 1, :] * tap
        convs.append(acc)
        yield
    ln = {}
    yield from _group_layer_norm_stages(jnp.concatenate(convs, axis=0) + dw_b, gavg, ln_g, ln_b, ln)
    zc = ln['ln']
    zc = (zc * jax.nn.sigmoid(zc)).astype(BF16)
    out['yc'] = jnp.dot(zc, pww_ref[...], preferred_element_type=F32) + pw_b


_MIX_PARAMS = ('norm1_g', 'w_in', 'gavg', 'gmlp_ln_g', 'gmlp_ln_b', 'ws_cat', 'bs_tile',
               'dw_w', 'dw_b', 'conv_ln_g', 'conv_ln_b', 'pw_w', 'pw_b')


def _pa_shape(bsz, t_len):
    return jax.ShapeDtypeStruct((bsz, PA_COLS // A_WIDTH, t_len, A_WIDTH), F32)


def _pa_spec(tile):
    return pl.BlockSpec((1, PA_COLS // A_WIDTH, tile, A_WIDTH), lambda b, i: (b, 0, i, 0))


def _in_proj_kernel(x_ref, mod_ref, g_ref, w_ref, pa_ref):
    h = _norm_modulate(x_ref[0], g_ref[...], mod_ref[0, 0:1, :], mod_ref[0, 1:2, :]).astype(BF16)
    pa = jnp.dot(h, w_ref[...], preferred_element_type=F32)
    for k in range(PA_COLS // A_WIDTH):
        pa_ref[0, k] = pa[:, k * A_WIDTH:(k + 1) * A_WIDTH]


def _in_proj(x, mod, lw, tile):
    bsz, t_len, d = x.shape
    return pl.pallas_call(
        _in_proj_kernel,
        grid=(bsz, t_len // tile),
        in_specs=[
            pl.BlockSpec((1, tile, d), lambda b, i: (b, i, 0)),
            pl.BlockSpec((1, 6, d), lambda b, i: (b, 0, 0)),
            _full((1, d)),
            _param_spec(lw['w_in'], (d, PA_COLS)),
        ],
        out_specs=_pa_spec(tile),
        out_shape=_pa_shape(bsz, t_len),
        compiler_params=_cparams(("arbitrary", "arbitrary")),
        name="in_proj",
    )(x, mod, lw['norm1_g'], _param_arg(lw['w_in']))


def _in_proj_mix_kernel(x_ref, mod_ref, *refs, seg):
    p = dict(zip(_MIX_PARAMS, refs))
    pa_ref, gate_ref, ybc_ref, zpad_ref, zsh_ref = refs[len(_MIX_PARAMS):]
    h = _norm_modulate(x_ref[0], p['norm1_g'][...], mod_ref[0, 0:1, :], mod_ref[0, 1:2, :]).astype(BF16)
    gavg = p['gavg'][...]

    def proj(lo, n):
        return jnp.dot(h, p['w_in'][:, lo:lo + n], preferred_element_type=F32)

    chunks = [(pa_ref, lo) for lo in range(0, PA_COLS, PROJ_CHUNK)]
    chunks += [(gate_ref, lo) for lo in range(0, A_WIDTH, PROJ_CHUNK)]
    pending = iter(chunks)

    def project_next():
        nxt = next(pending, None)
        if nxt is not None:
            ref, lo = nxt
            if ref is pa_ref:
                k, off = divmod(lo, A_WIDTH)
                pa_ref[0, k, :, off:off + PROJ_CHUNK] = proj(lo, PROJ_CHUNK)
            else:
                ref[0, :, lo:lo + PROJ_CHUNK] = proj(PA_COLS + lo, PROJ_CHUNK)

    b0 = PA_COLS + A_WIDTH
    c0 = b0 + 2 * B_WIDTH
    out = {}
    branches = [
        _conv_stages(proj(c0, C_WIDTH), proj(c0 + C_WIDTH, C_WIDTH), gavg, p['dw_w'], p['dw_b'][...],
                     p['conv_ln_g'][...], p['conv_ln_b'][...], p['pw_w'], p['pw_b'][...],
                     zpad_ref, zsh_ref, seg, out),
        _gmlp_stages(lambda lo: proj(b0 + lo, B_WIDTH), gavg, p['gmlp_ln_g'][...], p['gmlp_ln_b'][...],
                     p['ws_cat'], p['bs_tile'], out),
    ]
    _round_robin(branches, project_next)
    ybc_ref[0] = jnp.concatenate([out['yb'], out['yc']], axis=1).astype(BF16)
    for _ in chunks:
        project_next()


def _in_proj_mix(x, mod, lw, tile, seg):
    bsz, t_len, d = x.shape
    nseg = tile // seg

    def tok(cols):
        return pl.BlockSpec((1, tile, cols), lambda b, i: (b, i, 0))

    params = [lw[name] for name in _MIX_PARAMS]
    return pl.pallas_call(
        functools.partial(_in_proj_mix_kernel, seg=seg),
        grid=(bsz, t_len // tile),
        in_specs=[tok(d), pl.BlockSpec((1, 6, d), lambda b, i: (b, 0, 0))] + [_param_spec(p) for p in params],
        out_specs=[_pa_spec(tile), tok(A_WIDTH), tok(B_WIDTH + C_WIDTH)],
        out_shape=[_pa_shape(bsz, t_len),
                   jax.ShapeDtypeStruct((bsz, t_len, A_WIDTH), F32),
                   jax.ShapeDtypeStruct((bsz, t_len, B_WIDTH + C_WIDTH), BF16)],
        scratch_shapes=[pltpu.VMEM((nseg, seg + 2 * C_PAD, C_WIDTH), F32),
                        pltpu.VMEM((nseg, SUBLANES, seg + 2 * C_PAD - SUBLANES, C_WIDTH), F32)],
        compiler_params=_cparams(("arbitrary", "arbitrary")),
        name="in_proj_mix",
    )(x, mod, *[_param_arg(p) for p in params])


def _lower_bounds(logits, layer):
    depth = logits.shape[0]
    rows = [logits[j] for j in range(depth)]
    m = rows[0]
    for r in rows[1:]:
        m = jnp.maximum(m, r)
    es = [jnp.exp(r - m) for r in rows]
    tot = es[0]
    for e in es[1:]:
        tot = tot + e
    lb = jnp.zeros_like(m)
    for j in range(1, layer + 1):
        lb = lb + es[j] / tot
    return lb


def _chunk_rows(c, lo=0, n=A_CHUNK):
    return slice(c * A_CHUNK + lo, c * A_CHUNK + lo + n)


def _hgrn_gates(z, lb, rev, n_chunks):
    row = lax.broadcasted_iota(jnp.int32, (A_CHUNK, A_CHUNK), 0)
    col = lax.broadcasted_iota(jnp.int32, (A_CHUNK, A_CHUNK), 1)
    tri = ((col >= row) if rev else (col <= row)).astype(F32).astype(BF16)
    oml = 1.0 - lb
    t = oml * jax.nn.sigmoid(z)
    lf = jnp.log(lb + t)
    kk = oml - t
    hi = lf.astype(BF16)
    lo = (lf - hi.astype(F32)).astype(BF16)
    bs = [(jnp.dot(tri, hi[_chunk_rows(c)], preferred_element_type=F32)
           + jnp.dot(tri, lo[_chunk_rows(c)], preferred_element_type=F32)) * LOG2E
          for c in range(n_chunks)]
    return kk, bs


def _hgrn_factors(q, v, kk, bs, rev):
    sub = A_SUB
    wide, narrow = (0, 1) if rev else (1, 0)
    mid_row = sub // 2 if rev else sub // 2 - 1
    last_row = 0 if rev else A_CHUNK - 1
    halves = [slice(i * sub, (i + 1) * sub) for i in range(2)]
    per_chunk = {name: [] for name in ('qe', 'kd', 'qd', 'k_wide', 'k_narrow')}
    decs = []
    for c, bc in enumerate(bs):
        qc, kc = q[_chunk_rows(c)], kk[_chunk_rows(c)]
        mids = [bc[i * sub + mid_row:i * sub + mid_row + 1] for i in range(2)]
        last = bc[last_row:last_row + 1]
        decs.append(jnp.exp2(last))
        per_chunk['qe'].append((qc * jnp.exp2(bc)).astype(BF16))
        per_chunk['kd'].append((kc * jnp.exp2(last - bc)).astype(BF16))
        per_chunk['qd'].append(jnp.concatenate(
            [(qc[h] * jnp.exp2(jnp.minimum(bc[h] - mids[i], EXP2_CLAMP))).astype(BF16)
             for i, h in enumerate(halves)], axis=0))
        per_chunk['k_wide'].append((kc * jnp.exp2(jnp.minimum(mids[wide] - bc, EXP2_CLAMP))).astype(BF16))
        hn = halves[narrow]
        per_chunk['k_narrow'].append(
            (kc[hn] * jnp.exp2(jnp.minimum(mids[narrow] - bc[hn], EXP2_CLAMP))).astype(BF16))
    f = {name: jnp.concatenate(vals, axis=0) for name, vals in per_chunk.items()}
    f['dec'] = decs
    f['v'] = v.astype(BF16)
    return f


def _hgrn_scores(f, rev, c):
    sub = A_SUB
    wide, narrow = (0, 1) if rev else (1, 0)
    row_n = lax.broadcasted_iota(jnp.int32, (sub, sub), 0)
    col_n = lax.broadcasted_iota(jnp.int32, (sub, sub), 1)
    row_w = lax.broadcasted_iota(jnp.int32, (sub, A_CHUNK), 0)
    col_w = lax.broadcasted_iota(jnp.int32, (sub, A_CHUNK), 1)
    mask_n = (col_n >= row_n) if rev else (col_n <= row_n)
    mask_w = (col_w >= row_w) if rev else (col_w <= row_w + sub)
    per_head = []
    for h in range(A_HEADS):
        hc = slice(h * A_DK, (h + 1) * A_DK)
        s_n = lax.dot_general(f['qd'][_chunk_rows(c, narrow * sub, sub), hc],
                              f['k_narrow'][c * sub:(c + 1) * sub, hc], _NT, preferred_element_type=F32)
        s_w = lax.dot_general(f['qd'][_chunk_rows(c, wide * sub, sub), hc],
                              f['k_wide'][_chunk_rows(c), hc], _NT, preferred_element_type=F32)
        per_head.append((jnp.where(mask_n, s_n, 0.0).astype(BF16),
                         jnp.where(mask_w, s_w, 0.0).astype(BF16)))
    return per_head


def _hgrn_intra(f, scores, rev, c):
    sub = A_SUB
    narrow = 1 if rev else 0
    per_head = []
    for h in range(A_HEADS):
        hc = slice(h * A_DK, (h + 1) * A_DK)
        s_n, s_w = scores[h]
        o_n = jnp.dot(s_n, f['v'][_chunk_rows(c, narrow * sub, sub), hc], preferred_element_type=F32)
        o_w = jnp.dot(s_w, f['v'][_chunk_rows(c), hc], preferred_element_type=F32)
        intra = jnp.concatenate([o_w, o_n] if rev else [o_n, o_w], axis=0)
        upd = lax.dot_general(f['v'][_chunk_rows(c), hc], f['kd'][_chunk_rows(c), hc], _TN,
                              preferred_element_type=F32)
        per_head.append((intra, upd))
    return per_head


def _hgrn_stages(q, z, v, lb, rev, n_chunks, get_state, set_state, put_out, group=HGRN_GROUP, lag=HGRN_LAG):
    n_groups = n_chunks // group
    order = list(range(n_groups - 1, -1, -1) if rev else range(n_groups))
    facs, scores, intra = {}, {}, {}
    states = [get_state(h) for h in range(A_HEADS)]
    for step in range(n_groups + 3 * lag):
        if step < n_groups:
            g = order[step]
            rows = slice(g * group * A_CHUNK, (g + 1) * group * A_CHUNK)
            facs[g] = _hgrn_factors(q[rows], v[rows], *_hgrn_gates(z[rows], lb, rev, group), rev)
        if 0 <= step - lag < n_groups:
            g = order[step - lag]
            scores[g] = [_hgrn_scores(facs[g], rev, c) for c in range(group)]
        if 0 <= step - 2 * lag < n_groups:
            g = order[step - 2 * lag]
            intra[g] = [_hgrn_intra(facs[g], scores[g][c], rev, c) for c in range(group)]
        if 0 <= step - 3 * lag < n_groups:
            g = order[step - 3 * lag]
            f = facs[g]
            for c in (range(group - 1, -1, -1) if rev else range(group)):
                outs = []
                for h in range(A_HEADS):
                    hc = slice(h * A_DK, (h + 1) * A_DK)
                    inter = lax.dot_general(f['qe'][_chunk_rows(c), hc], states[h].astype(BF16), _NT,
                                            preferred_element_type=F32)
                    outs.append(inter + intra[g][c][h][0])
                    states[h] = states[h] * f['dec'][c][:, hc] + intra[g][c][h][1]
                put_out(g * group + c, jnp.concatenate(outs, axis=1))
        yield
    for h in range(A_HEADS):
        set_state(h, states[h])


def _hgrn_kernel(qf_ref, zf_ref, vf_ref, qb_ref, zb_ref, vb_ref, lbl_ref, s0_ref,
                 of_ref, ob_ref, st_ref, *, layer, n_chunks):
    @pl.when(pl.program_id(1) == 0)
    def _():
        st_ref[...] = s0_ref[...]

    lb = _lower_bounds(lbl_ref[...], layer)

    def direction(d, q_ref, z_ref, v_ref, o_ref):
        def set_state(h, val):
            st_ref[0, d, h] = val

        def put_out(c, val):
            o_ref[0, _chunk_rows(c), :] = val

        return _hgrn_stages(q_ref[...], z_ref[...], v_ref[...], lb[d:d + 1], d == 1, n_chunks,
                            lambda h: st_ref[0, d, h], set_state, put_out)

    _round_robin([direction(0, qf_ref, zf_ref, vf_ref, of_ref),
                  direction(1, qb_ref, zb_ref, vb_ref, ob_ref)])


def _hgrn(pa, lb_logits, s0, layer, tile):
    bsz, _, t_len, _ = pa.shape
    nb = t_len // tile
    depth = lb_logits.shape[0]

    def in_spec(group, rev):
        if rev:
            return pl.BlockSpec((None, None, tile, A_WIDTH), lambda b, i: (b, group, nb - 1 - i, 0))
        return pl.BlockSpec((None, None, tile, A_WIDTH), lambda b, i: (b, group, i, 0))

    def out_spec(rev):
        if rev:
            return pl.BlockSpec((1, tile, A_WIDTH), lambda b, i: (b, nb - 1 - i, 0))
        return pl.BlockSpec((1, tile, A_WIDTH), lambda b, i: (b, i, 0))

    st_spec = pl.BlockSpec((1, 2, A_HEADS, A_DK, A_DK), lambda b, i: (b, 0, 0, 0, 0))
    o_sds = jax.ShapeDtypeStruct((bsz, t_len, A_WIDTH), F32)
    return pl.pallas_call(
        functools.partial(_hgrn_kernel, layer=layer, n_chunks=tile // A_CHUNK),
        grid=(bsz, nb),
        in_specs=[in_spec(0, False), in_spec(1, False), in_spec(3, False),
                  in_spec(0, True), in_spec(2, True), in_spec(3, True),
                  _full((depth, 2, A_WIDTH)), st_spec],
        out_specs=[out_spec(False), out_spec(True), st_spec],
        out_shape=[o_sds, o_sds, jax.ShapeDtypeStruct(s0.shape, F32)],
        compiler_params=_cparams(("arbitrary", "arbitrary")),
        name="hgrn",
    )(pa, pa, pa, pa, pa, pa, lb_logits, s0)


def _out_ffn_kernel(x_ref, of_ref, ob_ref, gate_ref, ybc_ref, mod_ref, og_ref, wo_ref,
                    g_ref, w13_ref, w2_ref, fg_ref, o_ref, *, d_ff, chunk, final_norm):
    o = of_ref[0] + ob_ref[0]
    parts = []
    for h in range(A_HEADS):
        oh = o[:, h * A_DK:(h + 1) * A_DK]
        parts.append(oh * lax.rsqrt(jnp.mean(oh * oh, axis=-1, keepdims=True) + EPS))
    gate = gate_ref[0]
    ya = jnp.concatenate(parts, axis=1) * og_ref[...] * (gate * jax.nn.sigmoid(gate))
    y = (jnp.dot(ya.astype(BF16), wo_ref[0:A_WIDTH, :], preferred_element_type=F32)
         + jnp.dot(ybc_ref[0], wo_ref[A_WIDTH:, :], preferred_element_type=F32))
    x = x_ref[0] + mod_ref[0, 2:3, :] * y
    h = _norm_modulate(x, g_ref[...], mod_ref[0, 3:4, :], mod_ref[0, 4:5, :]).astype(BF16)
    acts = []
    for c in range(d_ff // chunk):
        a = jnp.dot(h, w13_ref[:, c * chunk:(c + 1) * chunk], preferred_element_type=F32)
        b = jnp.dot(h, w13_ref[:, d_ff + c * chunk:d_ff + (c + 1) * chunk], preferred_element_type=F32)
        acts.append((a * jax.nn.sigmoid(a) * b).astype(BF16))
    acc = jnp.dot(jnp.concatenate(acts, axis=1), w2_ref[...], preferred_element_type=F32)
    y = x + mod_ref[0, 5:6, :] * acc
    if final_norm:
        y = y * lax.rsqrt(jnp.mean(y * y, axis=-1, keepdims=True) + EPS) * fg_ref[...]
    o_ref[0] = y


def _out_ffn(x, o_f, o_b, gate, ybc, mod, lw, final_g, tile, final_norm):
    bsz, t_len, d = x.shape
    d_ff = lw['ffn_w2'].shape[0]

    def tok(cols):
        return pl.BlockSpec((1, tile, cols), lambda b, i: (b, i, 0))

    params = [lw['onorm_g'], lw['w_out'], lw['norm2_g'], lw['ffn_w13'], lw['ffn_w2'], final_g.reshape(1, d)]
    return pl.pallas_call(
        functools.partial(_out_ffn_kernel, d_ff=d_ff, chunk=FFN_CHUNK, final_norm=final_norm),
        grid=(bsz, t_len // tile),
        in_specs=[tok(d), tok(A_WIDTH), tok(A_WIDTH), tok(A_WIDTH), tok(B_WIDTH + C_WIDTH),
                  pl.BlockSpec((1, 6, d), lambda b, i: (b, 0, 0))] + [_param_spec(p) for p in params],
        out_specs=tok(d),
        out_shape=jax.ShapeDtypeStruct(x.shape, F32),
        compiler_params=_cparams(("arbitrary", "arbitrary")),
        name="out_ffn",
    )(x, o_f, o_b, gate, ybc, mod, *[_param_arg(p) for p in params])


def _layer_params(l, gavg, mxu_weights, norm1_g, hgrn_onorm_g, gmlp_ln_g, gmlp_ln_b, gmlp_b_s,
                  conv_dw_w, conv_dw_b, conv_ln_g, conv_ln_b, conv_pw_b, norm2_g):
    row = lambda a: a.reshape(1, -1).astype(F32)
    lw = {name: _LayerSlice(w, l) for name, w in mxu_weights.items()}
    lw.update({
        'norm1_g': row(norm1_g[l]), 'onorm_g': row(hgrn_onorm_g[l]), 'gavg': gavg,
        'gmlp_ln_g': row(gmlp_ln_g[l]), 'gmlp_ln_b': row(gmlp_ln_b[l]),
        'bs_tile': jnp.repeat(gmlp_b_s[l].T, B_WIDTH // B_GROUPS, axis=1).astype(F32),
        'dw_w': conv_dw_w[l].astype(F32), 'dw_b': row(conv_dw_b[l]),
        'conv_ln_g': row(conv_ln_g[l]), 'conv_ln_b': row(conv_ln_b[l]),
        'pw_b': row(conv_pw_b[l]), 'norm2_g': row(norm2_g[l]),
    })
    return lw


def kernel(x, c, ctx, c_ctx, ada_w, ada_b, norm1_g, w_in, hgrn_lb_logits, hgrn_onorm_g, gmlp_ln_g, gmlp_ln_b, gmlp_w_s, gmlp_b_s, conv_dw_w, conv_dw_b, conv_ln_g, conv_ln_b, conv_pw_w, conv_pw_b, w_out, norm2_g, ffn_w13, ffn_w2, final_norm_g):
    bsz, t_len, d = x.shape
    ctx_len = ctx.shape[1]
    depth = ada_w.shape[0]
    lb_logits = hgrn_lb_logits.astype(F32)

    cs = jnp.concatenate([c, c_ctx[None, :], jnp.zeros((MOD_ROWS - bsz - 1, d), F32)], axis=0)
    mod_all = _modulation(cs, ada_w, ada_b)
    grp = jnp.arange(B_WIDTH) // (B_WIDTH // B_GROUPS)
    gavg = ((grp[:, None] == grp[None, :]).astype(F32) / (B_WIDTH // B_GROUPS)).astype(BF16)
    zero_state = jnp.zeros((bsz, 2, A_HEADS, A_DK, A_DK), F32)
    mxu_weights = {
        'w_in': w_in.astype(BF16), 'w_out': w_out.astype(BF16),
        'ffn_w13': ffn_w13.astype(BF16), 'ffn_w2': ffn_w2.astype(BF16),
        'pw_w': conv_pw_w.astype(BF16),
        'ws_cat': gmlp_w_s.reshape(depth, B_GROUPS * B_CHUNK, B_CHUNK).astype(BF16),
    }
    row_len = t_len // (t_len // GRID_W)

    for l in range(depth):
        last = l == depth - 1
        lw = _layer_params(l, gavg, mxu_weights, norm1_g, hgrn_onorm_g, gmlp_ln_g, gmlp_ln_b, gmlp_b_s,
                           conv_dw_w, conv_dw_b, conv_ln_g, conv_ln_b, conv_pw_b, norm2_g)
        mod = mod_all[l, :bsz].reshape(bsz, 6, d)
        mod_ctx = jnp.broadcast_to(mod_all[l, bsz].reshape(1, 6, d), (bsz, 6, d))
        if last:
            pa_c = _in_proj(ctx, mod_ctx, lw, ctx_len)
            _, _, state = _hgrn(pa_c, lb_logits, zero_state, l, ctx_len)
        else:
            pa_c, gate_c, ybc_c = _in_proj_mix(ctx, mod_ctx, lw, ctx_len, ctx_len)
            of_c, ob_c, state = _hgrn(pa_c, lb_logits, zero_state, l, ctx_len)
            ctx = _out_ffn(ctx, of_c, ob_c, gate_c, ybc_c, mod_ctx, lw, final_norm_g, ctx_len, False)
        pa, gate, ybc = _in_proj_mix(x, mod, lw, LATENT_TILE, row_len)
        o_f, o_b, _ = _hgrn(pa, lb_logits, state, l, SCAN_TILE)
        x = _out_ffn(x, o_f, o_b, gate, ybc, mod, lw, final_norm_g, LATENT_TILE, last)
    return x
```

```python
import functools
from typing import NamedTuple

import jax
import jax.numpy as jnp
from jax import lax
from jax.experimental import pallas as pl
from jax.experimental.pallas import tpu as pltpu

F32 = jnp.float32
BF16 = jnp.bfloat16

D_MODEL = 1024
GRID_W = 64
A_HEADS = 4
A_DK = 128
A_WIDTH = A_HEADS * A_DK
A_CHUNK = 64
A_SUB = 32
HGRN_GROUP = 1
HGRN_LAG = 1
B_GROUPS = 4
B_WIDTH = 256
B_CHUNK = 128
C_GROUPS = 4
C_WIDTH = 256
C_KERNEL = 31
C_PAD = 16
SUBLANES = 8
PROJ_CHUNK = 256
FFN_CHUNK = 256
PA_COLS = 4 * A_WIDTH
EPS = 1e-6
EXP2_CLAMP = 80.0
LOG2E = 1.4426950408889634
MOD_ROWS = 8
LATENT_TILE = 512
SCAN_TILE = 1024

V7X_VMEM_LIMIT = 56 * 1024 * 1024

_NT = (((1,), (1,)), ((), ()))
_TN = (((0,), (0,)), ((), ()))
_FINISHED = object()


def _cparams(sem):
    return pltpu.CompilerParams(dimension_semantics=sem, vmem_limit_bytes=V7X_VMEM_LIMIT)


def _full(shape):
    n = len(shape)
    return pl.BlockSpec(shape, lambda *_: (0,) * n)


class _LayerSlice(NamedTuple):
    stacked: jax.Array
    layer: int

    @property
    def shape(self):
        return self.stacked.shape[1:]


def _param_spec(p, block=None):
    if not isinstance(p, _LayerSlice):
        return _full(p.shape)
    block = p.shape if block is None else block
    index = (p.layer,) + (0,) * len(block)
    return pl.BlockSpec((None,) + tuple(block), lambda *_: index, pipeline_mode=pl.Buffered(1))


def _param_arg(p):
    return p.stacked if isinstance(p, _LayerSlice) else p


def _round_robin(branches, between=None):
    while branches:
        branches = [g for g in branches if next(g, _FINISHED) is not _FINISHED]
        if between is not None:
            between()


def _mod_kernel(cs_ref, w_ref, b_ref, o_ref):
    s = cs_ref[...]
    s = s * jax.nn.sigmoid(s)
    w = w_ref[0]
    s_hi = s.astype(BF16)
    s_lo = (s - s_hi.astype(F32)).astype(BF16)
    w_hi = w.astype(BF16)
    w_lo = (w - w_hi.astype(F32)).astype(BF16)
    r = jnp.dot(jnp.concatenate([s_hi, s_lo], axis=0), w_hi, preferred_element_type=F32)
    o_ref[0] = (r[:MOD_ROWS] + r[MOD_ROWS:] + jnp.dot(s_hi, w_lo, preferred_element_type=F32)
                + b_ref[0])


def _modulation(cs, ada_w, ada_b):
    depth, d, n = ada_w.shape
    tn = 1024
    return pl.pallas_call(
        _mod_kernel,
        grid=(depth, n // tn),
        in_specs=[
            pl.BlockSpec((MOD_ROWS, d), lambda l, j: (0, 0)),
            pl.BlockSpec((1, d, tn), lambda l, j: (l, 0, j)),
            pl.BlockSpec((1, 1, tn), lambda l, j: (l, 0, j)),
        ],
        out_specs=pl.BlockSpec((1, MOD_ROWS, tn), lambda l, j: (l, 0, j)),
        out_shape=jax.ShapeDtypeStruct((depth, MOD_ROWS, n), F32),
        compiler_params=_cparams(("arbitrary", "arbitrary")),
        name="modulation",
    )(cs, ada_w, ada_b.reshape(depth, 1, n))


def _norm_modulate(x, g, shift, scale):
    ms = jnp.mean(x * x, axis=-1, keepdims=True)
    y = x * lax.rsqrt(ms + EPS) * g
    return y * (1.0 + scale) + shift


def _group_mean(x, gavg):
    return jnp.dot(x.astype(BF16), gavg, preferred_element_type=F32)


def _group_layer_norm_stages(x, gavg, g, b, out):
    mu = _group_mean(x, gavg)
    yield
    xc = x - mu
    var = _group_mean(xc * xc, gavg)
    yield
    out['ln'] = xc * lax.rsqrt(var + EPS) * g + b


def _gmlp_stages(project, gavg, ln_g, ln_b, ws_ref, bs_ref, out):
    u = project(0)
    v = project(B_WIDTH)
    tile = u.shape[0]
    yield
    ln = {}
    yield from _group_layer_norm_stages(jax.nn.gelu(v), gavg, ln_g, ln_b, ln)
    vn = ln['ln'].astype(BF16)
    gu = jax.nn.gelu(u)
    lane_group = lax.broadcasted_iota(jnp.int32, (B_CHUNK, B_WIDTH), 1) // (B_WIDTH // B_GROUPS)
    mixed = []
    for n in range(tile // B_CHUNK):
        yield
        r = jnp.dot(ws_ref[...], vn[n * B_CHUNK:(n + 1) * B_CHUNK], preferred_element_type=F32)
        m = bs_ref[...]
        for g in range(B_GROUPS):
            m = m + jnp.where(lane_group == g, r[g * B_CHUNK:(g + 1) * B_CHUNK], 0.0)
        mixed.append(m)
    out['yb'] = gu * jnp.concatenate(mixed, axis=0)


def _conv_stages(ca, cgate, gavg, dww_ref, dw_b, ln_g, ln_b, pww_ref, pw_b, zpad_ref, zsh_ref, seg, out):
    tile = ca.shape[0]
    z = ca * jax.nn.sigmoid(cgate)
    nseg = tile // seg
    pad_zeros = jnp.zeros((C_PAD, C_WIDTH), F32)
    for s in range(nseg):
        zpad_ref[s, 0:C_PAD, :] = pad_zeros
        zpad_ref[s, C_PAD + seg:2 * C_PAD + seg, :] = pad_zeros
        zpad_ref[s, C_PAD:C_PAD + seg, :] = z[s * seg:(s + 1) * seg]
    span = seg + 2 * C_PAD - SUBLANES
    convs = []
    for s in range(nseg):
        for r in range(1, SUBLANES):
            zsh_ref[s, r] = zpad_ref[s, r:r + span, :]
        acc = jnp.zeros((seg, C_WIDTH), F32)
        for j in range(C_KERNEL):
            start = C_PAD - C_KERNEL // 2 + j
            r, a = start % SUBLANES, start - start % SUBLANES
            tap = zpad_ref[s, a:a + seg, :] if r == 0 else zsh_ref[s, r, a:a + seg, :]
            acc = acc + dww_ref[j:j + 1, :] * tap
        convs.append(acc)
        yield
    ln = {}
    yield from _group_layer_norm_stages(jnp.concatenate(convs, axis=0) + dw_b, gavg, ln_g, ln_b, ln)
    zc = ln['ln']
    zc = (zc * jax.nn.sigmoid(zc)).astype(BF16)
    out['yc'] = jnp.dot(zc, pww_ref[...], preferred_element_type=F32) + pw_b


_MIX_PARAMS = ('norm1_g', 'w_in', 'gavg', 'gmlp_ln_g', 'gmlp_ln_b', 'ws_cat', 'bs_tile',
               'dw_w', 'dw_b', 'conv_ln_g', 'conv_ln_b', 'pw_w', 'pw_b')


def _in_proj_kernel(x_ref, mod_ref, g_ref, w_ref, pa_ref):
    h = _norm_modulate(x_ref[0], g_ref[...], mod_ref[0, 0:1, :], mod_ref[0, 1:2, :]).astype(BF16)
    pa_ref[0] = jnp.dot(h, w_ref[...].astype(BF16), preferred_element_type=F32)


def _in_proj(x, mod, lw, tile):
    bsz, t_len, d = x.shape
    return pl.pallas_call(
        _in_proj_kernel,
        grid=(bsz, t_len // tile),
        in_specs=[
            pl.BlockSpec((1, tile, d), lambda b, i: (b, i, 0)),
            pl.BlockSpec((1, 6, d), lambda b, i: (b, 0, 0)),
            _full((1, d)),
            _param_spec(lw['w_in'], (d, PA_COLS)),
        ],
        out_specs=pl.BlockSpec((1, tile, PA_COLS), lambda b, i: (b, i, 0)),
        out_shape=jax.ShapeDtypeStruct((bsz, t_len, PA_COLS), F32),
        compiler_params=_cparams(("arbitrary", "arbitrary")),
        name="in_proj",
    )(x, mod, lw['norm1_g'], _param_arg(lw['w_in']))


def _in_proj_mix_kernel(x_ref, mod_ref, *refs, seg):
    p = dict(zip(_MIX_PARAMS, refs))
    pa_ref, gate_ref, ybc_ref, zpad_ref, zsh_ref = refs[len(_MIX_PARAMS):]
    h = _norm_modulate(x_ref[0], p['norm1_g'][...], mod_ref[0, 0:1, :], mod_ref[0, 1:2, :]).astype(BF16)
    gavg = p['gavg'][...]

    def proj(lo, n):
        return jnp.dot(h, p['w_in'][:, lo:lo + n].astype(BF16), preferred_element_type=F32)

    chunks = [(pa_ref, lo) for lo in range(0, PA_COLS, PROJ_CHUNK)]
    chunks += [(gate_ref, lo) for lo in range(0, A_WIDTH, PROJ_CHUNK)]
    pending = iter(chunks)

    def project_next():
        nxt = next(pending, None)
        if nxt is not None:
            ref, lo = nxt
            w_lo = lo if ref is pa_ref else PA_COLS + lo
            ref[0, :, lo:lo + PROJ_CHUNK] = proj(w_lo, PROJ_CHUNK)

    b0 = PA_COLS + A_WIDTH
    c0 = b0 + 2 * B_WIDTH
    out = {}
    branches = [
        _conv_stages(proj(c0, C_WIDTH), proj(c0 + C_WIDTH, C_WIDTH), gavg, p['dw_w'], p['dw_b'][...],
                     p['conv_ln_g'][...], p['conv_ln_b'][...], p['pw_w'], p['pw_b'][...],
                     zpad_ref, zsh_ref, seg, out),
        _gmlp_stages(lambda lo: proj(b0 + lo, B_WIDTH), gavg, p['gmlp_ln_g'][...], p['gmlp_ln_b'][...],
                     p['ws_cat'], p['bs_tile'], out),
    ]
    _round_robin(branches, project_next)
    ybc_ref[0] = jnp.concatenate([out['yb'], out['yc']], axis=1).astype(BF16)
    for _ in chunks:
        project_next()


def _in_proj_mix(x, mod, lw, tile, seg):
    bsz, t_len, d = x.shape
    nseg = tile // seg

    def tok(cols):
        return pl.BlockSpec((1, tile, cols), lambda b, i: (b, i, 0))

    params = [lw[name] for name in _MIX_PARAMS]
    return pl.pallas_call(
        functools.partial(_in_proj_mix_kernel, seg=seg),
        grid=(bsz, t_len // tile),
        in_specs=[tok(d), pl.BlockSpec((1, 6, d), lambda b, i: (b, 0, 0))] + [_param_spec(p) for p in params],
        out_specs=[tok(PA_COLS), tok(A_WIDTH), tok(B_WIDTH + C_WIDTH)],
        out_shape=[jax.ShapeDtypeStruct((bsz, t_len, PA_COLS), F32),
                   jax.ShapeDtypeStruct((bsz, t_len, A_WIDTH), F32),
                   jax.ShapeDtypeStruct((bsz, t_len, B_WIDTH + C_WIDTH), BF16)],
        scratch_shapes=[pltpu.VMEM((nseg, seg + 2 * C_PAD, C_WIDTH), F32),
                        pltpu.VMEM((nseg, SUBLANES, seg + 2 * C_PAD - SUBLANES, C_WIDTH), F32)],
        compiler_params=_cparams(("arbitrary", "arbitrary")),
        name="in_proj_mix",
    )(x, mod, *[_param_arg(p) for p in params])


def _lower_bounds(logits, layer):
    depth = logits.shape[0]
    rows = [logits[j] for j in range(depth)]
    m = rows[0]
    for r in rows[1:]:
        m = jnp.maximum(m, r)
    es = [jnp.exp(r - m) for r in rows]
    tot = es[0]
    for e in es[1:]:
        tot = tot + e
    lb = jnp.zeros_like(m)
    for j in range(1, layer + 1):
        lb = lb + es[j] / tot
    return lb


def _chunk_rows(c, lo=0, n=A_CHUNK):
    return slice(c * A_CHUNK + lo, c * A_CHUNK + lo + n)


def _hgrn_gates(z, lb, rev, n_chunks):
    row = lax.broadcasted_iota(jnp.int32, (A_CHUNK, A_CHUNK), 0)
    col = lax.broadcasted_iota(jnp.int32, (A_CHUNK, A_CHUNK), 1)
    tri = ((col >= row) if rev else (col <= row)).astype(F32).astype(BF16)
    oml = 1.0 - lb
    t = oml * jax.nn.sigmoid(z)
    lf = jnp.log(lb + t)
    kk = oml - t
    hi = lf.astype(BF16)
    lo = (lf - hi.astype(F32)).astype(BF16)
    bs = [(jnp.dot(tri, hi[_chunk_rows(c)], preferred_element_type=F32)
           + jnp.dot(tri, lo[_chunk_rows(c)], preferred_element_type=F32)) * LOG2E
          for c in range(n_chunks)]
    return kk, bs


def _hgrn_factors(q, v, kk, bs, rev):
    sub = A_SUB
    wide, narrow = (0, 1) if rev else (1, 0)
    mid_row = sub // 2 if rev else sub // 2 - 1
    last_row = 0 if rev else A_CHUNK - 1
    halves = [slice(i * sub, (i + 1) * sub) for i in range(2)]
    per_chunk = {name: [] for name in ('qe', 'kd', 'qd', 'k_wide', 'k_narrow')}
    decs = []
    for c, bc in enumerate(bs):
        qc, kc = q[_chunk_rows(c)], kk[_chunk_rows(c)]
        mids = [bc[i * sub + mid_row:i * sub + mid_row + 1] for i in range(2)]
        last = bc[last_row:last_row + 1]
        decs.append(jnp.exp2(last))
        per_chunk['qe'].append((qc * jnp.exp2(bc)).astype(BF16))
        per_chunk['kd'].append((kc * jnp.exp2(last - bc)).astype(BF16))
        per_chunk['qd'].append(jnp.concatenate(
            [(qc[h] * jnp.exp2(jnp.minimum(bc[h] - mids[i], EXP2_CLAMP))).astype(BF16)
             for i, h in enumerate(halves)], axis=0))
        per_chunk['k_wide'].append((kc * jnp.exp2(jnp.minimum(mids[wide] - bc, EXP2_CLAMP))).astype(BF16))
        hn = halves[narrow]
        per_chunk['k_narrow'].append(
            (kc[hn] * jnp.exp2(jnp.minimum(mids[narrow] - bc[hn], EXP2_CLAMP))).astype(BF16))
    f = {name: jnp.concatenate(vals, axis=0) for name, vals in per_chunk.items()}
    f['dec'] = decs
    f['v'] = v.astype(BF16)
    return f


def _hgrn_scores(f, rev, c):
    sub = A_SUB
    wide, narrow = (0, 1) if rev else (1, 0)
    row_n = lax.broadcasted_iota(jnp.int32, (sub, sub), 0)
    col_n = lax.broadcasted_iota(jnp.int32, (sub, sub), 1)
    row_w = lax.broadcasted_iota(jnp.int32, (sub, A_CHUNK), 0)
    col_w = lax.broadcasted_iota(jnp.int32, (sub, A_CHUNK), 1)
    mask_n = (col_n >= row_n) if rev else (col_n <= row_n)
    mask_w = (col_w >= row_w) if rev else (col_w <= row_w + sub)
    per_head = []
    for h in range(A_HEADS):
        hc = slice(h * A_DK, (h + 1) * A_DK)
        s_n = lax.dot_general(f['qd'][_chunk_rows(c, narrow * sub, sub), hc],
                              f['k_narrow'][c * sub:(c + 1) * sub, hc], _NT, preferred_element_type=F32)
        s_w = lax.dot_general(f['qd'][_chunk_rows(c, wide * sub, sub), hc],
                              f['k_wide'][_chunk_rows(c), hc], _NT, preferred_element_type=F32)
        per_head.append((jnp.where(mask_n, s_n, 0.0).astype(BF16),
                         jnp.where(mask_w, s_w, 0.0).astype(BF16)))
    return per_head


def _hgrn_intra(f, scores, rev, c):
    sub = A_SUB
    narrow = 1 if rev else 0
    per_head = []
    for h in range(A_HEADS):
        hc = slice(h * A_DK, (h + 1) * A_DK)
        s_n, s_w = scores[h]
        o_n = jnp.dot(s_n, f['v'][_chunk_rows(c, narrow * sub, sub), hc], preferred_element_type=F32)
        o_w = jnp.dot(s_w, f['v'][_chunk_rows(c), hc], preferred_element_type=F32)
        intra = jnp.concatenate([o_w, o_n] if rev else [o_n, o_w], axis=0)
        upd = lax.dot_general(f['v'][_chunk_rows(c), hc], f['kd'][_chunk_rows(c), hc], _TN,
                              preferred_element_type=F32)
        per_head.append((intra, upd))
    return per_head


def _hgrn_stages(q, z, v, lb, rev, n_chunks, get_state, set_state, put_out, group=HGRN_GROUP, lag=HGRN_LAG):
    n_groups = n_chunks // group
    order = list(range(n_groups - 1, -1, -1) if rev else range(n_groups))
    facs, scores, intra = {}, {}, {}
    states = [get_state(h) for h in range(A_HEADS)]
    for step in range(n_groups + 3 * lag):
        if step < n_groups:
            g = order[step]
            rows = slice(g * group * A_CHUNK, (g + 1) * group * A_CHUNK)
            facs[g] = _hgrn_factors(q[rows], v[rows], *_hgrn_gates(z[rows], lb, rev, group), rev)
        if 0 <= step - lag < n_groups:
            g = order[step - lag]
            scores[g] = [_hgrn_scores(facs[g], rev, c) for c in range(group)]
        if 0 <= step - 2 * lag < n_groups:
            g = order[step - 2 * lag]
            intra[g] = [_hgrn_intra(facs[g], scores[g][c], rev, c) for c in range(group)]
        if 0 <= step - 3 * lag < n_groups:
            g = order[step - 3 * lag]
            f = facs[g]
            for c in (range(group - 1, -1, -1) if rev else range(group)):
                outs = []
                for h in range(A_HEADS):
                    hc = slice(h * A_DK, (h + 1) * A_DK)
                    inter = lax.dot_general(f['qe'][_chunk_rows(c), hc], states[h].astype(BF16), _NT,
                                            preferred_element_type=F32)
                    outs.append(inter + intra[g][c][h][0])
                    states[h] = states[h] * f['dec'][c][:, hc] + intra[g][c][h][1]
                put_out(g * group + c, jnp.concatenate(outs, axis=1))
        yield
    for h in range(A_HEADS):
        set_state(h, states[h])


def _hgrn_kernel(qf_ref, zf_ref, vf_ref, qb_ref, zb_ref, vb_ref, lbl_ref, s0_ref,
                 of_ref, ob_ref, st_ref, *, layer, n_chunks):
    @pl.when(pl.program_id(1) == 0)
    def _():
        st_ref[...] = s0_ref[...]

    lb = _lower_bounds(lbl_ref[...], layer)

    def direction(d, q_ref, z_ref, v_ref, o_ref):
        def set_state(h, val):
            st_ref[0, d, h] = val

        def put_out(c, val):
            o_ref[0, _chunk_rows(c), :] = val

        return _hgrn_stages(q_ref[0], z_ref[0], v_ref[0], lb[d:d + 1], d == 1, n_chunks,
                            lambda h: st_ref[0, d, h], set_state, put_out)

    _round_robin([direction(0, qf_ref, zf_ref, vf_ref, of_ref),
                  direction(1, qb_ref, zb_ref, vb_ref, ob_ref)])


def _hgrn(pa, lb_logits, s0, layer, tile):
    bsz, t_len, _ = pa.shape
    nb = t_len // tile
    depth = lb_logits.shape[0]

    def col_spec(colblk, rev):
        if rev:
            return pl.BlockSpec((1, tile, A_WIDTH), lambda b, i: (b, nb - 1 - i, colblk))
        return pl.BlockSpec((1, tile, A_WIDTH), lambda b, i: (b, i, colblk))

    st_spec = pl.BlockSpec((1, 2, A_HEADS, A_DK, A_DK), lambda b, i: (b, 0, 0, 0, 0))
    o_sds = jax.ShapeDtypeStruct((bsz, t_len, A_WIDTH), F32)
    return pl.pallas_call(
        functools.partial(_hgrn_kernel, layer=layer, n_chunks=tile // A_CHUNK),
        grid=(bsz, nb),
        in_specs=[col_spec(0, False), col_spec(1, False), col_spec(3, False),
                  col_spec(0, True), col_spec(2, True), col_spec(3, True),
                  _full((depth, 2, A_WIDTH)), st_spec],
        out_specs=[col_spec(0, False), col_spec(0, True), st_spec],
        out_shape=[o_sds, o_sds, jax.ShapeDtypeStruct(s0.shape, F32)],
        compiler_params=_cparams(("arbitrary", "arbitrary")),
        name="hgrn",
    )(pa, pa, pa, pa, pa, pa, lb_logits, s0)


def _out_ffn_kernel(x_ref, of_ref, ob_ref, gate_ref, ybc_ref, mod_ref, og_ref, wo_ref,
                    g_ref, w13_ref, w2_ref, fg_ref, o_ref, *, d_ff, chunk, final_norm):
    o = of_ref[0] + ob_ref[0]
    parts = []
    for h in range(A_HEADS):
        oh = o[:, h * A_DK:(h + 1) * A_DK]
        parts.append(oh * lax.rsqrt(jnp.mean(oh * oh, axis=-1, keepdims=True) + EPS))
    gate = gate_ref[0]
    ya = jnp.concatenate(parts, axis=1) * og_ref[...] * (gate * jax.nn.sigmoid(gate))
    y = (jnp.dot(ya.astype(BF16), wo_ref[0:A_WIDTH, :], preferred_element_type=F32)
         + jnp.dot(ybc_ref[0], wo_ref[A_WIDTH:, :], preferred_element_type=F32))
    x = x_ref[0] + mod_ref[0, 2:3, :] * y
    h = _norm_modulate(x, g_ref[...], mod_ref[0, 3:4, :], mod_ref[0, 4:5, :]).astype(BF16)
    acts = []
    for c in range(d_ff // chunk):
        a = jnp.dot(h, w13_ref[:, c * chunk:(c + 1) * chunk], preferred_element_type=F32)
        b = jnp.dot(h, w13_ref[:, d_ff + c * chunk:d_ff + (c + 1) * chunk], preferred_element_type=F32)
        acts.append((a * jax.nn.sigmoid(a) * b).astype(BF16))
    acc = jnp.dot(jnp.concatenate(acts, axis=1), w2_ref[...], preferred_element_type=F32)
    y = x + mod_ref[0, 5:6, :] * acc
    if final_norm:
        y = y * lax.rsqrt(jnp.mean(y * y, axis=-1, keepdims=True) + EPS) * fg_ref[...]
    o_ref[0] = y


def _out_ffn(x, o_f, o_b, gate, ybc, mod, lw, final_g, tile, final_norm):
    bsz, t_len, d = x.shape
    d_ff = lw['ffn_w2'].shape[0]

    def tok(cols):
        return pl.BlockSpec((1, tile, cols), lambda b, i: (b, i, 0))

    params = [lw['onorm_g'], lw['w_out'], lw['norm2_g'], lw['ffn_w13'], lw['ffn_w2'], final_g.reshape(1, d)]
    return pl.pallas_call(
        functools.partial(_out_ffn_kernel, d_ff=d_ff, chunk=FFN_CHUNK, final_norm=final_norm),
        grid=(bsz, t_len // tile),
        in_specs=[tok(d), tok(A_WIDTH), tok(A_WIDTH), tok(A_WIDTH), tok(B_WIDTH + C_WIDTH),
                  pl.BlockSpec((1, 6, d), lambda b, i: (b, 0, 0))] + [_param_spec(p) for p in params],
        out_specs=tok(d),
        out_shape=jax.ShapeDtypeStruct(x.shape, F32),
        compiler_params=_cparams(("arbitrary", "arbitrary")),
        name="out_ffn",
    )(x, o_f, o_b, gate, ybc, mod, *[_param_arg(p) for p in params])


def _layer_params(l, gavg, mxu_weights, norm1_g, hgrn_onorm_g, gmlp_ln_g, gmlp_ln_b, gmlp_b_s,
                  conv_dw_w, conv_dw_b, conv_ln_g, conv_ln_b, conv_pw_b, norm2_g):
    row = lambda a: a.reshape(1, -1).astype(F32)
    lw = {name: _LayerSlice(w, l) for name, w in mxu_weights.items()}
    lw.update({
        'norm1_g': row(norm1_g[l]), 'onorm_g': row(hgrn_onorm_g[l]), 'gavg': gavg,
        'gmlp_ln_g': row(gmlp_ln_g[l]), 'gmlp_ln_b': row(gmlp_ln_b[l]),
        'bs_tile': jnp.repeat(gmlp_b_s[l].T, B_WIDTH // B_GROUPS, axis=1).astype(F32),
        'dw_w': conv_dw_w[l].astype(F32), 'dw_b': row(conv_dw_b[l]),
        'conv_ln_g': row(conv_ln_g[l]), 'conv_ln_b': row(conv_ln_b[l]),
        'pw_b': row(conv_pw_b[l]), 'norm2_g': row(norm2_g[l]),
    })
    return lw


def kernel(x, c, ctx, c_ctx, ada_w, ada_b, norm1_g, w_in, hgrn_lb_logits, hgrn_onorm_g, gmlp_ln_g, gmlp_ln_b, gmlp_w_s, gmlp_b_s, conv_dw_w, conv_dw_b, conv_ln_g, conv_ln_b, conv_pw_w, conv_pw_b, w_out, norm2_g, ffn_w13, ffn_w2, final_norm_g):
    bsz, t_len, d = x.shape
    ctx_len = ctx.shape[1]
    depth = ada_w.shape[0]
    lb_logits = hgrn_lb_logits.astype(F32)

    cs = jnp.concatenate([c, c_ctx[None, :], jnp.zeros((MOD_ROWS - bsz - 1, d), F32)], axis=0)
    mod_all = _modulation(cs, ada_w, ada_b)
    grp = jnp.arange(B_WIDTH) // (B_WIDTH // B_GROUPS)
    gavg = ((grp[:, None] == grp[None, :]).astype(F32) / (B_WIDTH // B_GROUPS)).astype(BF16)
    zero_state = jnp.zeros((bsz, 2, A_HEADS, A_DK, A_DK), F32)
    mxu_weights = {
        'w_in': w_in, 'w_out': w_out.astype(BF16),
        'ffn_w13': ffn_w13.astype(BF16), 'ffn_w2': ffn_w2.astype(BF16),
        'pw_w': conv_pw_w.astype(BF16),
        'ws_cat': gmlp_w_s.reshape(depth, B_GROUPS * B_CHUNK, B_CHUNK).astype(BF16),
    }
    row_len = t_len // (t_len // GRID_W)
    ctx_flat = ctx.reshape(1, bsz * ctx_len, d)
    ctx_tile = min(LATENT_TILE, bsz * ctx_len)

    for l in range(depth):
        last = l == depth - 1
        lw = _layer_params(l, gavg, mxu_weights, norm1_g, hgrn_onorm_g, gmlp_ln_g, gmlp_ln_b, gmlp_b_s,
                           conv_dw_w, conv_dw_b, conv_ln_g, conv_ln_b, conv_pw_b, norm2_g)
        mod = mod_all[l, :bsz].reshape(bsz, 6, d)
        mod_ctx = mod_all[l, bsz].reshape(1, 6, d)
        if last:
            pa_c = _in_proj(ctx_flat, mod_ctx, lw, ctx_tile)
            _, _, state = _hgrn(pa_c.reshape(bsz, ctx_len, PA_COLS), lb_logits, zero_state, l, ctx_len)
        else:
            pa_c, gate_c, ybc_c = _in_proj_mix(ctx_flat, mod_ctx, lw, ctx_tile, ctx_len)
            of_c, ob_c, state = _hgrn(pa_c.reshape(bsz, ctx_len, PA_COLS), lb_logits, zero_state, l, ctx_len)
            ctx_flat = _out_ffn(ctx_flat, of_c.reshape(1, -1, A_WIDTH), ob_c.reshape(1, -1, A_WIDTH),
                                gate_c, ybc_c, mod_ctx, lw, final_norm_g, ctx_tile, False)
        pa, gate, ybc = _in_proj_mix(x, mod, lw, LATENT_TILE, row_len)
        o_f, o_b, _ = _hgrn(pa, lb_logits, state, l, SCAN_TILE)
        x = _out_ffn(x, o_f, o_b, gate, ybc, mod, lw, final_norm_g, LATENT_TILE, last)
    return x
```

```python
import functools
from typing import NamedTuple

import jax
import jax.numpy as jnp
from jax import lax
from jax.experimental import pallas as pl
from jax.experimental.pallas import tpu as pltpu

F32 = jnp.float32
BF16 = jnp.bfloat16

D_MODEL = 1024
GRID_W = 64
A_HEADS = 4
A_DK = 128
A_WIDTH = A_HEADS * A_DK
A_CHUNK = 64
A_SUB = 32
HGRN_GROUP = 1
HGRN_LAG = 1
B_GROUPS = 4
B_WIDTH = 256
B_CHUNK = 128
C_GROUPS = 4
C_WIDTH = 256
C_KERNEL = 31
C_PAD = 16
SUBLANES = 8
PROJ_CHUNK = 256
FFN_CHUNK = 256
PA_COLS = 4 * A_WIDTH
EPS = 1e-6
EXP2_CLAMP = 80.0
LOG2E = 1.4426950408889634
MOD_ROWS = 8
LATENT_TILE = 512
SCAN_TILE = 1024

V7X_VMEM_LIMIT = 56 * 1024 * 1024

_NT = (((1,), (1,)), ((), ()))
_TN = (((0,), (0,)), ((), ()))
_FINISHED = object()


def _cparams(sem):
    return pltpu.CompilerParams(dimension_semantics=sem, vmem_limit_bytes=V7X_VMEM_LIMIT)


def _full(shape):
    n = len(shape)
    return pl.BlockSpec(shape, lambda *_: (0,) * n)


class _LayerSlice(NamedTuple):
    stacked: jax.Array
    layer: int

    @property
    def shape(self):
        return self.stacked.shape[1:]


def _param_spec(p, block=None):
    if not isinstance(p, _LayerSlice):
        return _full(p.shape)
    block = p.shape if block is None else block
    index = (p.layer,) + (0,) * len(block)
    return pl.BlockSpec((None,) + tuple(block), lambda *_: index, pipeline_mode=pl.Buffered(1))


def _param_arg(p):
    return p.stacked if isinstance(p, _LayerSlice) else p


def _round_robin(branches, between=None):
    while branches:
        branches = [g for g in branches if next(g, _FINISHED) is not _FINISHED]
        if between is not None:
            between()


def _mod_kernel(cs_ref, w_ref, b_ref, o_ref):
    s = cs_ref[...]
    s = s * jax.nn.sigmoid(s)
    w = w_ref[0]
    s_hi = s.astype(BF16)
    s_lo = (s - s_hi.astype(F32)).astype(BF16)
    w_hi = w.astype(BF16)
    w_lo = (w - w_hi.astype(F32)).astype(BF16)
    r = jnp.dot(jnp.concatenate([s_hi, s_lo], axis=0), w_hi, preferred_element_type=F32)
    o_ref[0] = (r[:MOD_ROWS] + r[MOD_ROWS:] + jnp.dot(s_hi, w_lo, preferred_element_type=F32)
                + b_ref[0])


def _modulation(cs, ada_w, ada_b):
    depth, d, n = ada_w.shape
    tn = 2048
    return pl.pallas_call(
        _mod_kernel,
        grid=(depth, n // tn),
        in_specs=[
            pl.BlockSpec((MOD_ROWS, d), lambda l, j: (0, 0)),
            pl.BlockSpec((1, d, tn), lambda l, j: (l, 0, j)),
            pl.BlockSpec((1, 1, tn), lambda l, j: (l, 0, j)),
        ],
        out_specs=pl.BlockSpec((1, MOD_ROWS, tn), lambda l, j: (l, 0, j)),
        out_shape=jax.ShapeDtypeStruct((depth, MOD_ROWS, n), F32),
        compiler_params=_cparams(("arbitrary", "arbitrary")),
        name="modulation",
    )(cs, ada_w, ada_b.reshape(depth, 1, n))


def _norm_modulate(x, g, shift, scale):
    ms = jnp.mean(x * x, axis=-1, keepdims=True)
    return x * lax.rsqrt(ms + EPS) * (g * (1.0 + scale)) + shift


def _group_mean(x, gavg):
    return jnp.dot(x.astype(BF16), gavg, preferred_element_type=F32)


def _group_layer_norm_stages(x, gavg, g, b, out):
    mu = _group_mean(x, gavg)
    yield
    xc = x - mu
    var = _group_mean(xc * xc, gavg)
    yield
    out['ln'] = xc * lax.rsqrt(var + EPS) * g + b


def _gmlp_stages(project, gavg, ln_g, ln_b, ws_ref, bs_ref, out):
    u = project(0)
    v = project(B_WIDTH)
    tile = u.shape[0]
    yield
    ln = {}
    yield from _group_layer_norm_stages(jax.nn.gelu(v), gavg, ln_g, ln_b, ln)
    vn = ln['ln'].astype(BF16)
    gu = jax.nn.gelu(u)
    lane_group = lax.broadcasted_iota(jnp.int32, (B_CHUNK, B_WIDTH), 1) // (B_WIDTH // B_GROUPS)
    mixed = []
    for n in range(tile // B_CHUNK):
        yield
        r = jnp.dot(ws_ref[...], vn[n * B_CHUNK:(n + 1) * B_CHUNK], preferred_element_type=F32)
        m = bs_ref[...]
        for g in range(B_GROUPS):
            m = m + jnp.where(lane_group == g, r[g * B_CHUNK:(g + 1) * B_CHUNK], 0.0)
        mixed.append(m)
    out['yb'] = gu * jnp.concatenate(mixed, axis=0)


def _conv_stages(ca, cgate, gavg, dww_ref, dw_b, ln_g, ln_b, pww_ref, pw_b, zpad_ref, zsh_ref, seg, out):
    tile = ca.shape[0]
    z = ca * jax.nn.sigmoid(cgate)
    nseg = tile // seg
    pad_zeros = jnp.zeros((C_PAD, C_WIDTH), F32)
    for s in range(nseg):
        zpad_ref[s, 0:C_PAD, :] = pad_zeros
        zpad_ref[s, C_PAD + seg:2 * C_PAD + seg, :] = pad_zeros
        zpad_ref[s, C_PAD:C_PAD + seg, :] = z[s * seg:(s + 1) * seg]
    span = seg + 2 * C_PAD - SUBLANES
    convs = []
    for s in range(nseg):
        for r in range(1, SUBLANES):
            zsh_ref[s, r] = zpad_ref[s, r:r + span, :]
        acc = jnp.zeros((seg, C_WIDTH), F32)
        for j in range(C_KERNEL):
            start = C_PAD - C_KERNEL // 2 + j
            r, a = start % SUBLANES, start - start % SUBLANES
            tap = zpad_ref[s, a:a + seg, :] if r == 0 else zsh_ref[s, r, a:a + seg, :]
            acc = acc + dww_ref[j:j + 1, :] * tap
        convs.append(acc)
        yield
    ln = {}
    yield from _group_layer_norm_stages(jnp.concatenate(convs, axis=0) + dw_b, gavg, ln_g, ln_b, ln)
    zc = ln['ln']
    zc = (zc * jax.nn.sigmoid(zc)).astype(BF16)
    out['yc'] = jnp.dot(zc, pww_ref[...], preferred_element_type=F32) + pw_b


_MIX_PARAMS = ('norm1_g', 'w_in', 'gavg', 'gmlp_ln_g', 'gmlp_ln_b', 'ws_cat', 'bs_tile',
               'dw_w', 'dw_b', 'conv_ln_g', 'conv_ln_b', 'pw_w', 'pw_b')


def _in_proj_kernel(x_ref, mod_ref, g_ref, w_ref, pa_ref):
    h = _norm_modulate(x_ref[0], g_ref[...], mod_ref[0, 0:1, :], mod_ref[0, 1:2, :]).astype(BF16)
    pa_ref[0] = jnp.dot(h, w_ref[...].astype(BF16), preferred_element_type=F32)


def _in_proj(x, mod, lw, tile):
    bsz, t_len, d = x.shape
    return pl.pallas_call(
        _in_proj_kernel,
        grid=(bsz, t_len // tile),
        in_specs=[
            pl.BlockSpec((1, tile, d), lambda b, i: (b, i, 0)),
            pl.BlockSpec((1, 6, d), lambda b, i: (b, 0, 0)),
            _full((1, d)),
            _param_spec(lw['w_in'], (d, PA_COLS)),
        ],
        out_specs=pl.BlockSpec((1, tile, PA_COLS), lambda b, i: (b, i, 0)),
        out_shape=jax.ShapeDtypeStruct((bsz, t_len, PA_COLS), F32),
        compiler_params=_cparams(("arbitrary", "arbitrary")),
        name="in_proj",
    )(x, mod, lw['norm1_g'], _param_arg(lw['w_in']))


def _in_proj_mix_kernel(x_ref, mod_ref, *refs, seg):
    p = dict(zip(_MIX_PARAMS, refs))
    pa_ref, gate_ref, ybc_ref, zpad_ref, zsh_ref = refs[len(_MIX_PARAMS):]
    h = _norm_modulate(x_ref[0], p['norm1_g'][...], mod_ref[0, 0:1, :], mod_ref[0, 1:2, :]).astype(BF16)
    gavg = p['gavg'][...]

    def proj(lo, n):
        return jnp.dot(h, p['w_in'][:, lo:lo + n].astype(BF16), preferred_element_type=F32)

    chunks = [(pa_ref, lo) for lo in range(0, PA_COLS, PROJ_CHUNK)]
    chunks += [(gate_ref, lo) for lo in range(0, A_WIDTH, PROJ_CHUNK)]
    pending = iter(chunks)

    def project_next():
        nxt = next(pending, None)
        if nxt is not None:
            ref, lo = nxt
            w_lo = lo if ref is pa_ref else PA_COLS + lo
            ref[0, :, lo:lo + PROJ_CHUNK] = proj(w_lo, PROJ_CHUNK)

    b0 = PA_COLS + A_WIDTH
    c0 = b0 + 2 * B_WIDTH
    out = {}
    branches = [
        _conv_stages(proj(c0, C_WIDTH), proj(c0 + C_WIDTH, C_WIDTH), gavg, p['dw_w'], p['dw_b'][...],
                     p['conv_ln_g'][...], p['conv_ln_b'][...], p['pw_w'], p['pw_b'][...],
                     zpad_ref, zsh_ref, seg, out),
        _gmlp_stages(lambda lo: proj(b0 + lo, B_WIDTH), gavg, p['gmlp_ln_g'][...], p['gmlp_ln_b'][...],
                     p['ws_cat'], p['bs_tile'], out),
    ]
    _round_robin(branches, project_next)
    ybc_ref[0] = jnp.concatenate([out['yb'], out['yc']], axis=1).astype(BF16)
    for _ in chunks:
        project_next()


def _in_proj_mix(x, mod, lw, tile, seg):
    bsz, t_len, d = x.shape
    nseg = tile // seg

    def tok(cols):
        return pl.BlockSpec((1, tile, cols), lambda b, i: (b, i, 0))

    params = [lw[name] for name in _MIX_PARAMS]
    return pl.pallas_call(
        functools.partial(_in_proj_mix_kernel, seg=seg),
        grid=(bsz, t_len // tile),
        in_specs=[tok(d), pl.BlockSpec((1, 6, d), lambda b, i: (b, 0, 0))] + [_param_spec(p) for p in params],
        out_specs=[tok(PA_COLS), tok(A_WIDTH), tok(B_WIDTH + C_WIDTH)],
        out_shape=[jax.ShapeDtypeStruct((bsz, t_len, PA_COLS), F32),
                   jax.ShapeDtypeStruct((bsz, t_len, A_WIDTH), F32),
                   jax.ShapeDtypeStruct((bsz, t_len, B_WIDTH + C_WIDTH), BF16)],
        scratch_shapes=[pltpu.VMEM((nseg, seg + 2 * C_PAD, C_WIDTH), F32),
                        pltpu.VMEM((nseg, SUBLANES, seg + 2 * C_PAD - SUBLANES, C_WIDTH), F32)],
        compiler_params=_cparams(("arbitrary", "arbitrary")),
        name="in_proj_mix",
    )(x, mod, *[_param_arg(p) for p in params])


def _lower_bounds(logits, layer):
    depth = logits.shape[0]
    rows = [logits[j] for j in range(depth)]
    m = rows[0]
    for r in rows[1:]:
        m = jnp.maximum(m, r)
    es = [jnp.exp(r - m) for r in rows]
    tot = es[0]
    for e in es[1:]:
        tot = tot + e
    lb = jnp.zeros_like(m)
    for j in range(1, layer + 1):
        lb = lb + es[j] / tot
    return lb


def _chunk_rows(c, lo=0, n=A_CHUNK):
    return slice(c * A_CHUNK + lo, c * A_CHUNK + lo + n)


def _hgrn_gates(z, lb, rev, n_chunks):
    row = lax.broadcasted_iota(jnp.int32, (A_CHUNK, A_CHUNK), 0)
    col = lax.broadcasted_iota(jnp.int32, (A_CHUNK, A_CHUNK), 1)
    tri = ((col >= row) if rev else (col <= row)).astype(F32).astype(BF16)
    oml = 1.0 - lb
    t = oml * jax.nn.sigmoid(z)
    lf = jnp.log(lb + t)
    kk = oml - t
    hi = lf.astype(BF16)
    lo = (lf - hi.astype(F32)).astype(BF16)
    bs = [(jnp.dot(tri, hi[_chunk_rows(c)], preferred_element_type=F32)
           + jnp.dot(tri, lo[_chunk_rows(c)], preferred_element_type=F32)) * LOG2E
          for c in range(n_chunks)]
    return kk, bs


def _hgrn_factors(q, v, kk, bs, rev):
    sub = A_SUB
    wide, narrow = (0, 1) if rev else (1, 0)
    mid_row = sub // 2 if rev else sub // 2 - 1
    last_row = 0 if rev else A_CHUNK - 1
    halves = [slice(i * sub, (i + 1) * sub) for i in range(2)]
    per_chunk = {name: [] for name in ('qe', 'kd', 'qd', 'k_wide', 'k_narrow')}
    decs = []
    for c, bc in enumerate(bs):
        qc, kc = q[_chunk_rows(c)], kk[_chunk_rows(c)]
        mids = [bc[i * sub + mid_row:i * sub + mid_row + 1] for i in range(2)]
        last = bc[last_row:last_row + 1]
        decs.append(jnp.exp2(last))
        per_chunk['qe'].append((qc * jnp.exp2(bc)).astype(BF16))
        per_chunk['kd'].append((kc * jnp.exp2(last - bc)).astype(BF16))
        per_chunk['qd'].append(jnp.concatenate(
            [(qc[h] * jnp.exp2(jnp.minimum(bc[h] - mids[i], EXP2_CLAMP))).astype(BF16)
             for i, h in enumerate(halves)], axis=0))
        per_chunk['k_wide'].append((kc * jnp.exp2(jnp.minimum(mids[wide] - bc, EXP2_CLAMP))).astype(BF16))
        hn = halves[narrow]
        per_chunk['k_narrow'].append(
            (kc[hn] * jnp.exp2(jnp.minimum(mids[narrow] - bc[hn], EXP2_CLAMP))).astype(BF16))
    f = {name: jnp.concatenate(vals, axis=0) for name, vals in per_chunk.items()}
    f['dec'] = decs
    f['v'] = v.astype(BF16)
    return f


def _hgrn_scores(f, rev, c):
    sub = A_SUB
    wide, narrow = (0, 1) if rev else (1, 0)
    row_n = lax.broadcasted_iota(jnp.int32, (sub, sub), 0)
    col_n = lax.broadcasted_iota(jnp.int32, (sub, sub), 1)
    row_w = lax.broadcasted_iota(jnp.int32, (sub, A_CHUNK), 0)
    col_w = lax.broadcasted_iota(jnp.int32, (sub, A_CHUNK), 1)
    mask_n = (col_n >= row_n) if rev else (col_n <= row_n)
    mask_w = (col_w >= row_w) if rev else (col_w <= row_w + sub)
    per_head = []
    for h in range(A_HEADS):
        hc = slice(h * A_DK, (h + 1) * A_DK)
        s_n = lax.dot_general(f['qd'][_chunk_rows(c, narrow * sub, sub), hc],
                              f['k_narrow'][c * sub:(c + 1) * sub, hc], _NT, preferred_element_type=F32)
        s_w = lax.dot_general(f['qd'][_chunk_rows(c, wide * sub, sub), hc],
                              f['k_wide'][_chunk_rows(c), hc], _NT, preferred_element_type=F32)
        per_head.append((jnp.where(mask_n, s_n, 0.0).astype(BF16),
                         jnp.where(mask_w, s_w, 0.0).astype(BF16)))
    return per_head


def _hgrn_intra(f, scores, rev, c):
    sub = A_SUB
    narrow = 1 if rev else 0
    per_head = []
    for h in range(A_HEADS):
        hc = slice(h * A_DK, (h + 1) * A_DK)
        s_n, s_w = scores[h]
        o_n = jnp.dot(s_n, f['v'][_chunk_rows(c, narrow * sub, sub), hc], preferred_element_type=F32)
        o_w = jnp.dot(s_w, f['v'][_chunk_rows(c), hc], preferred_element_type=F32)
        intra = jnp.concatenate([o_w, o_n] if rev else [o_n, o_w], axis=0)
        upd = lax.dot_general(f['v'][_chunk_rows(c), hc], f['kd'][_chunk_rows(c), hc], _TN,
                              preferred_element_type=F32)
        per_head.append((intra, upd))
    return per_head


def _hgrn_stages(q, z, v, lb, rev, n_chunks, get_state, set_state, put_out, group=HGRN_GROUP, lag=HGRN_LAG):
    n_groups = n_chunks // group
    order = list(range(n_groups - 1, -1, -1) if rev else range(n_groups))
    facs, scores, intra = {}, {}, {}
    states = [get_state(h) for h in range(A_HEADS)]
    for step in range(n_groups + 3 * lag):
        if step < n_groups:
            g = order[step]
            rows = slice(g * group * A_CHUNK, (g + 1) * group * A_CHUNK)
            facs[g] = _hgrn_factors(q[rows], v[rows], *_hgrn_gates(z[rows], lb, rev, group), rev)
        if 0 <= step - lag < n_groups:
            g = order[step - lag]
            scores[g] = [_hgrn_scores(facs[g], rev, c) for c in range(group)]
        if 0 <= step - 2 * lag < n_groups:
            g = order[step - 2 * lag]
            intra[g] = [_hgrn_intra(facs[g], scores[g][c], rev, c) for c in range(group)]
        if 0 <= step - 3 * lag < n_groups:
            g = order[step - 3 * lag]
            f = facs[g]
            for c in (range(group - 1, -1, -1) if rev else range(group)):
                outs = []
                for h in range(A_HEADS):
                    hc = slice(h * A_DK, (h + 1) * A_DK)
                    inter = lax.dot_general(f['qe'][_chunk_rows(c), hc], states[h].astype(BF16), _NT,
                                            preferred_element_type=F32)
                    outs.append(inter + intra[g][c][h][0])
                    states[h] = states[h] * f['dec'][c][:, hc] + intra[g][c][h][1]
                put_out(g * group + c, jnp.concatenate(outs, axis=1))
        yield
    for h in range(A_HEADS):
        set_state(h, states[h])


def _hgrn_kernel(qf_ref, zf_ref, vf_ref, qb_ref, zb_ref, vb_ref, lbl_ref, s0_ref,
                 of_ref, ob_ref, st_ref, *, layer, n_chunks):
    @pl.when(pl.program_id(1) == 0)
    def _():
        st_ref[...] = s0_ref[...]

    lb = _lower_bounds(lbl_ref[...], layer)

    def direction(d, q_ref, z_ref, v_ref, o_ref):
        def set_state(h, val):
            st_ref[0, d, h] = val

        def put_out(c, val):
            o_ref[0, _chunk_rows(c), :] = val

        return _hgrn_stages(q_ref[0], z_ref[0], v_ref[0], lb[d:d + 1], d == 1, n_chunks,
                            lambda h: st_ref[0, d, h], set_state, put_out)

    _round_robin([direction(0, qf_ref, zf_ref, vf_ref, of_ref),
                  direction(1, qb_ref, zb_ref, vb_ref, ob_ref)])


def _hgrn(pa, lb_logits, s0, layer, tile):
    bsz, t_len, _ = pa.shape
    nb = t_len // tile
    depth = lb_logits.shape[0]

    def col_spec(colblk, rev):
        if rev:
            return pl.BlockSpec((1, tile, A_WIDTH), lambda b, i: (b, nb - 1 - i, colblk))
        return pl.BlockSpec((1, tile, A_WIDTH), lambda b, i: (b, i, colblk))

    st_spec = pl.BlockSpec((1, 2, A_HEADS, A_DK, A_DK), lambda b, i: (b, 0, 0, 0, 0))
    o_sds = jax.ShapeDtypeStruct((bsz, t_len, A_WIDTH), F32)
    return pl.pallas_call(
        functools.partial(_hgrn_kernel, layer=layer, n_chunks=tile // A_CHUNK),
        grid=(bsz, nb),
        in_specs=[col_spec(0, False), col_spec(1, False), col_spec(3, False),
                  col_spec(0, True), col_spec(2, True), col_spec(3, True),
                  _full((depth, 2, A_WIDTH)), st_spec],
        out_specs=[col_spec(0, False), col_spec(0, True), st_spec],
        out_shape=[o_sds, o_sds, jax.ShapeDtypeStruct(s0.shape, F32)],
        compiler_params=_cparams(("arbitrary", "arbitrary")),
        name="hgrn",
    )(pa, pa, pa, pa, pa, pa, lb_logits, s0)


def _out_ffn_kernel(x_ref, of_ref, ob_ref, gate_ref, ybc_ref, mod_ref, og_ref, wo_ref,
                    g_ref, w13_ref, w2_ref, fg_ref, o_ref, *, d_ff, chunk, final_norm):
    o = of_ref[0] + ob_ref[0]
    parts = []
    for h in range(A_HEADS):
        oh = o[:, h * A_DK:(h + 1) * A_DK]
        parts.append(oh * lax.rsqrt(jnp.mean(oh * oh, axis=-1, keepdims=True) + EPS))
    gate = gate_ref[0]
    ya = jnp.concatenate(parts, axis=1) * og_ref[...] * (gate * jax.nn.sigmoid(gate))
    y = (jnp.dot(ya.astype(BF16), wo_ref[0:A_WIDTH, :].astype(BF16), preferred_element_type=F32)
         + jnp.dot(ybc_ref[0], wo_ref[A_WIDTH:, :].astype(BF16), preferred_element_type=F32))
    x = x_ref[0] + mod_ref[0, 2:3, :] * y
    h = _norm_modulate(x, g_ref[...], mod_ref[0, 3:4, :], mod_ref[0, 4:5, :]).astype(BF16)
    acts = []
    for c in range(d_ff // chunk):
        a = jnp.dot(h, w13_ref[:, c * chunk:(c + 1) * chunk], preferred_element_type=F32)
        b = jnp.dot(h, w13_ref[:, d_ff + c * chunk:d_ff + (c + 1) * chunk], preferred_element_type=F32)
        acts.append((a * jax.nn.sigmoid(a) * b).astype(BF16))
    acc = jnp.dot(jnp.concatenate(acts, axis=1), w2_ref[...], preferred_element_type=F32)
    y = x + mod_ref[0, 5:6, :] * acc
    if final_norm:
        y = y * lax.rsqrt(jnp.mean(y * y, axis=-1, keepdims=True) + EPS) * fg_ref[...]
    o_ref[0] = y


def _out_ffn(x, o_f, o_b, gate, ybc, mod, lw, final_g, tile, final_norm):
    bsz, t_len, d = x.shape
    d_ff = lw['ffn_w2'].shape[0]

    def tok(cols):
        return pl.BlockSpec((1, tile, cols), lambda b, i: (b, i, 0))

    params = [lw['onorm_g'], lw['w_out'], lw['norm2_g'], lw['ffn_w13'], lw['ffn_w2'], final_g.reshape(1, d)]
    return pl.pallas_call(
        functools.partial(_out_ffn_kernel, d_ff=d_ff, chunk=FFN_CHUNK, final_norm=final_norm),
        grid=(bsz, t_len // tile),
        in_specs=[tok(d), tok(A_WIDTH), tok(A_WIDTH), tok(A_WIDTH), tok(B_WIDTH + C_WIDTH),
                  pl.BlockSpec((1, 6, d), lambda b, i: (b, 0, 0))] + [_param_spec(p) for p in params],
        out_specs=tok(d),
        out_shape=jax.ShapeDtypeStruct(x.shape, F32),
        compiler_params=_cparams(("arbitrary", "arbitrary")),
        name="out_ffn",
    )(x, o_f, o_b, gate, ybc, mod, *[_param_arg(p) for p in params])


def _layer_params(l, gavg, mxu_weights, norm1_g, hgrn_onorm_g, gmlp_ln_g, gmlp_ln_b, gmlp_b_s,
                  conv_dw_w, conv_dw_b, conv_ln_g, conv_ln_b, conv_pw_b, norm2_g):
    row = lambda a: a.reshape(1, -1).astype(F32)
    lw = {name: _LayerSlice(w, l) for name, w in mxu_weights.items()}
    lw.update({
        'norm1_g': row(norm1_g[l]), 'onorm_g': row(hgrn_onorm_g[l]), 'gavg': gavg,
        'gmlp_ln_g': row(gmlp_ln_g[l]), 'gmlp_ln_b': row(gmlp_ln_b[l]),
        'bs_tile': jnp.repeat(gmlp_b_s[l].T, B_WIDTH // B_GROUPS, axis=1).astype(F32),
        'dw_w': conv_dw_w[l].astype(F32), 'dw_b': row(conv_dw_b[l]),
        'conv_ln_g': row(conv_ln_g[l]), 'conv_ln_b': row(conv_ln_b[l]),
        'pw_b': row(conv_pw_b[l]), 'norm2_g': row(norm2_g[l]),
    })
    return lw


def kernel(x, c, ctx, c_ctx, ada_w, ada_b, norm1_g, w_in, hgrn_lb_logits, hgrn_onorm_g, gmlp_ln_g, gmlp_ln_b, gmlp_w_s, gmlp_b_s, conv_dw_w, conv_dw_b, conv_ln_g, conv_ln_b, conv_pw_w, conv_pw_b, w_out, norm2_g, ffn_w13, ffn_w2, final_norm_g):
    bsz, t_len, d = x.shape
    ctx_len = ctx.shape[1]
    depth = ada_w.shape[0]
    lb_logits = hgrn_lb_logits.astype(F32)

    cs = jnp.concatenate([c, c_ctx[None, :], jnp.zeros((MOD_ROWS - bsz - 1, d), F32)], axis=0)
    mod_all = _modulation(cs, ada_w, ada_b)
    grp = jnp.arange(B_WIDTH) // (B_WIDTH // B_GROUPS)
    gavg = ((grp[:, None] == grp[None, :]).astype(F32) / (B_WIDTH // B_GROUPS)).astype(BF16)
    zero_state = jnp.zeros((bsz, 2, A_HEADS, A_DK, A_DK), F32)
    mxu_weights = {
        'w_in': w_in, 'w_out': w_out,
        'ffn_w13': ffn_w13.astype(BF16), 'ffn_w2': ffn_w2.astype(BF16),
        'pw_w': conv_pw_w.astype(BF16),
        'ws_cat': gmlp_w_s.reshape(depth, B_GROUPS * B_CHUNK, B_CHUNK).astype(BF16),
    }
    row_len = t_len // (t_len // GRID_W)
    ctx_flat = ctx.reshape(1, bsz * ctx_len, d)
    ctx_tile = min(LATENT_TILE, bsz * ctx_len)

    for l in range(depth):
        last = l == depth - 1
        lw = _layer_params(l, gavg, mxu_weights, norm1_g, hgrn_onorm_g, gmlp_ln_g, gmlp_ln_b, gmlp_b_s,
                           conv_dw_w, conv_dw_b, conv_ln_g, conv_ln_b, conv_pw_b, norm2_g)
        mod = mod_all[l, :bsz].reshape(bsz, 6, d)
        mod_ctx = mod_all[l, bsz].reshape(1, 6, d)
        if last:
            pa_c = _in_proj(ctx_flat, mod_ctx, lw, ctx_tile)
            _, _, state = _hgrn(pa_c.reshape(bsz, ctx_len, PA_COLS), lb_logits, zero_state, l, ctx_len)
        else:
            pa_c, gate_c, ybc_c = _in_proj_mix(ctx_flat, mod_ctx, lw, ctx_tile, ctx_len)
            of_c, ob_c, state = _hgrn(pa_c.reshape(bsz, ctx_len, PA_COLS), lb_logits, zero_state, l, ctx_len)
            ctx_flat = _out_ffn(ctx_flat, of_c.reshape(1, -1, A_WIDTH), ob_c.reshape(1, -1, A_WIDTH),
                                gate_c, ybc_c, mod_ctx, lw, final_norm_g, ctx_tile, False)
        pa, gate, ybc = _in_proj_mix(x, mod, lw, LATENT_TILE, row_len)
        o_f, o_b, _ = _hgrn(pa, lb_logits, state, l, SCAN_TILE)
        x = _out_ffn(x, o_f, o_b, gate, ybc, mod, lw, final_norm_g, LATENT_TILE, last)
    return x
```

```python
import functools
from typing import NamedTuple

import jax
import jax.numpy as jnp
from jax import lax
from jax.experimental import pallas as pl
from jax.experimental.pallas import tpu as pltpu

F32 = jnp.float32
BF16 = jnp.bfloat16

D_MODEL = 1024
GRID_W = 64
A_HEADS = 4
A_DK = 128
A_WIDTH = A_HEADS * A_DK
A_CHUNK = 64
A_SUB = 32
HGRN_GROUP = 1
HGRN_LAG = 1
B_GROUPS = 4
B_WIDTH = 256
B_CHUNK = 128
C_GROUPS = 4
C_WIDTH = 256
C_KERNEL = 31
C_PAD = 16
SUBLANES = 8
PROJ_CHUNK = 256
FFN_CHUNK = 256
PA_COLS = 4 * A_WIDTH
EPS = 1e-6
EXP2_CLAMP = 80.0
LOG2E = 1.4426950408889634
MOD_ROWS = 8
LATENT_TILE = 512
SCAN_TILE = 1024

V7X_VMEM_LIMIT = 56 * 1024 * 1024

_NT = (((1,), (1,)), ((), ()))
_TN = (((0,), (0,)), ((), ()))
_FINISHED = object()


def _cparams(sem):
    return pltpu.CompilerParams(dimension_semantics=sem, vmem_limit_bytes=V7X_VMEM_LIMIT)


def _full(shape):
    n = len(shape)
    return pl.BlockSpec(shape, lambda *_: (0,) * n)


class _LayerSlice(NamedTuple):
    stacked: jax.Array
    layer: int

    @property
    def shape(self):
        return self.stacked.shape[1:]


def _param_spec(p, block=None):
    if not isinstance(p, _LayerSlice):
        return _full(p.shape)
    block = p.shape if block is None else block
    index = (p.layer,) + (0,) * len(block)
    return pl.BlockSpec((None,) + tuple(block), lambda *_: index, pipeline_mode=pl.Buffered(1))


def _param_arg(p):
    return p.stacked if isinstance(p, _LayerSlice) else p


def _round_robin(branches, between=None):
    while branches:
        branches = [g for g in branches if next(g, _FINISHED) is not _FINISHED]
        if between is not None:
            between()


def _mod_kernel(cs_ref, w_ref, b_ref, o_ref):
    s = cs_ref[...]
    s = s * jax.nn.sigmoid(s)
    w = w_ref[0]
    s_hi = s.astype(BF16)
    s_lo = (s - s_hi.astype(F32)).astype(BF16)
    w_hi = w.astype(BF16)
    w_lo = (w - w_hi.astype(F32)).astype(BF16)
    r = jnp.dot(jnp.concatenate([s_hi, s_lo], axis=0), w_hi, preferred_element_type=F32)
    o_ref[0] = (r[:MOD_ROWS] + r[MOD_ROWS:] + jnp.dot(s_hi, w_lo, preferred_element_type=F32)
                + b_ref[0])


def _modulation(cs, ada_w, ada_b):
    depth, d, n = ada_w.shape
    tn = 2048
    return pl.pallas_call(
        _mod_kernel,
        grid=(depth, n // tn),
        in_specs=[
            pl.BlockSpec((MOD_ROWS, d), lambda l, j: (0, 0)),
            pl.BlockSpec((1, d, tn), lambda l, j: (l, 0, j)),
            pl.BlockSpec((1, 1, tn), lambda l, j: (l, 0, j)),
        ],
        out_specs=pl.BlockSpec((1, MOD_ROWS, tn), lambda l, j: (l, 0, j)),
        out_shape=jax.ShapeDtypeStruct((depth, MOD_ROWS, n), F32),
        compiler_params=_cparams(("arbitrary", "arbitrary")),
        name="modulation",
    )(cs, ada_w, ada_b.reshape(depth, 1, n))


def _norm_modulate(x, g, shift, scale):
    ms = jnp.mean(x * x, axis=-1, keepdims=True)
    return x * lax.rsqrt(ms + EPS) * (g * (1.0 + scale)) + shift


def _group_mean(x, gavg):
    return jnp.dot(x.astype(BF16), gavg, preferred_element_type=F32)


def _group_layer_norm_stages(x, gavg, g, b, out):
    mu = _group_mean(x, gavg)
    yield
    xc = x - mu
    var = _group_mean(xc * xc, gavg)
    yield
    out['ln'] = xc * lax.rsqrt(var + EPS) * g + b


def _gmlp_stages(project, gavg, ln_g, ln_b, ws_ref, bs_ref, out):
    u = project(0)
    v = project(B_WIDTH)
    tile = u.shape[0]
    yield
    ln = {}
    yield from _group_layer_norm_stages(jax.nn.gelu(v), gavg, ln_g, ln_b, ln)
    vn = ln['ln'].astype(BF16)
    gu = jax.nn.gelu(u)
    lane_group = lax.broadcasted_iota(jnp.int32, (B_CHUNK, B_WIDTH), 1) // (B_WIDTH // B_GROUPS)
    mixed = []
    for n in range(tile // B_CHUNK):
        yield
        r = jnp.dot(ws_ref[...], vn[n * B_CHUNK:(n + 1) * B_CHUNK], preferred_element_type=F32)
        m = bs_ref[...]
        for g in range(B_GROUPS):
            m = m + jnp.where(lane_group == g, r[g * B_CHUNK:(g + 1) * B_CHUNK], 0.0)
        mixed.append(m)
    out['yb'] = gu * jnp.concatenate(mixed, axis=0)


def _conv_stages(ca, cgate, gavg, dww_ref, dw_b, ln_g, ln_b, pww_ref, pw_b, zpad_ref, zsh_ref, seg, out):
    tile = ca.shape[0]
    z = ca * jax.nn.sigmoid(cgate)
    nseg = tile // seg
    pad_zeros = jnp.zeros((C_PAD, C_WIDTH), F32)
    for s in range(nseg):
        zpad_ref[s, 0:C_PAD, :] = pad_zeros
        zpad_ref[s, C_PAD + seg:2 * C_PAD + seg, :] = pad_zeros
        zpad_ref[s, C_PAD:C_PAD + seg, :] = z[s * seg:(s + 1) * seg]
    span = seg + 2 * C_PAD - SUBLANES
    convs = []
    for s in range(nseg):
        for r in range(1, SUBLANES):
            zsh_ref[s, r] = zpad_ref[s, r:r + span, :]
        acc = jnp.zeros((seg, C_WIDTH), F32)
        for j in range(C_KERNEL):
            start = C_PAD - C_KERNEL // 2 + j
            r, a = start % SUBLANES, start - start % SUBLANES
            tap = zpad_ref[s, a:a + seg, :] if r == 0 else zsh_ref[s, r, a:a + seg, :]
            acc = acc + dww_ref[j:j + 1, :] * tap
        convs.append(acc)
        yield
    ln = {}
    yield from _group_layer_norm_stages(jnp.concatenate(convs, axis=0) + dw_b, gavg, ln_g, ln_b, ln)
    zc = ln['ln']
    zc = (zc * jax.nn.sigmoid(zc)).astype(BF16)
    out['yc'] = jnp.dot(zc, pww_ref[...], preferred_element_type=F32) + pw_b


_MIX_PARAMS = ('norm1_g', 'w_in', 'gavg', 'gmlp_ln_g', 'gmlp_ln_b', 'ws_cat', 'bs_tile',
               'dw_w', 'dw_b', 'conv_ln_g', 'conv_ln_b', 'pw_w', 'pw_b')


def _in_proj_kernel(x_ref, mod_ref, g_ref, w_ref, pa_ref):
    h = _norm_modulate(x_ref[0], g_ref[...], mod_ref[0, 0:1, :], mod_ref[0, 1:2, :]).astype(BF16)
    pa_ref[0] = jnp.dot(h, w_ref[...].astype(BF16), preferred_element_type=F32)


def _in_proj(x, mod, lw, tile):
    bsz, t_len, d = x.shape
    return pl.pallas_call(
        _in_proj_kernel,
        grid=(bsz, t_len // tile),
        in_specs=[
            pl.BlockSpec((1, tile, d), lambda b, i: (b, i, 0)),
            pl.BlockSpec((1, 6, d), lambda b, i: (b, 0, 0)),
            _full((1, d)),
            _param_spec(lw['w_in'], (d, PA_COLS)),
        ],
        out_specs=pl.BlockSpec((1, tile, PA_COLS), lambda b, i: (b, i, 0)),
        out_shape=jax.ShapeDtypeStruct((bsz, t_len, PA_COLS), F32),
        compiler_params=_cparams(("arbitrary", "arbitrary")),
        name="in_proj",
    )(x, mod, lw['norm1_g'], _param_arg(lw['w_in']))


def _in_proj_mix_kernel(x_ref, mod_ref, *refs, seg, n_cast):
    p = dict(zip(_MIX_PARAMS, refs))
    refs = refs[len(_MIX_PARAMS):]
    cast_in, refs = refs[:n_cast], refs[n_cast:]
    pa_ref, gate_ref, ybc_ref = refs[:3]
    cast_out, (zpad_ref, zsh_ref) = refs[3:3 + n_cast], refs[3 + n_cast:]
    for src, dst in zip(cast_in, cast_out):
        dst[...] = src[...].astype(BF16)
    h = _norm_modulate(x_ref[0], p['norm1_g'][...], mod_ref[0, 0:1, :], mod_ref[0, 1:2, :]).astype(BF16)
    gavg = p['gavg'][...]

    def proj(lo, n):
        return jnp.dot(h, p['w_in'][:, lo:lo + n].astype(BF16), preferred_element_type=F32)

    chunks = [(pa_ref, lo) for lo in range(0, PA_COLS, PROJ_CHUNK)]
    chunks += [(gate_ref, lo) for lo in range(0, A_WIDTH, PROJ_CHUNK)]
    pending = iter(chunks)

    def project_next():
        nxt = next(pending, None)
        if nxt is not None:
            ref, lo = nxt
            w_lo = lo if ref is pa_ref else PA_COLS + lo
            ref[0, :, lo:lo + PROJ_CHUNK] = proj(w_lo, PROJ_CHUNK)

    b0 = PA_COLS + A_WIDTH
    c0 = b0 + 2 * B_WIDTH
    out = {}
    branches = [
        _conv_stages(proj(c0, C_WIDTH), proj(c0 + C_WIDTH, C_WIDTH), gavg, p['dw_w'], p['dw_b'][...],
                     p['conv_ln_g'][...], p['conv_ln_b'][...], p['pw_w'], p['pw_b'][...],
                     zpad_ref, zsh_ref, seg, out),
        _gmlp_stages(lambda lo: proj(b0 + lo, B_WIDTH), gavg, p['gmlp_ln_g'][...], p['gmlp_ln_b'][...],
                     p['ws_cat'], p['bs_tile'], out),
    ]
    _round_robin(branches, project_next)
    ybc_ref[0] = jnp.concatenate([out['yb'], out['yc']], axis=1).astype(BF16)
    for _ in chunks:
        project_next()


def _in_proj_mix(x, mod, lw, tile, seg, cast=()):
    bsz, t_len, d = x.shape
    nseg = tile // seg
    nb = t_len // tile

    def tok(cols):
        return pl.BlockSpec((1, tile, cols), lambda b, i: (b, i, 0))

    def cast_spec(w):
        return pl.BlockSpec((w.shape[0] // (bsz * nb), w.shape[1]), lambda b, i: (b * nb + i, 0))

    params = [lw[name] for name in _MIX_PARAMS]
    return pl.pallas_call(
        functools.partial(_in_proj_mix_kernel, seg=seg, n_cast=len(cast)),
        grid=(bsz, nb),
        in_specs=[tok(d), pl.BlockSpec((1, 6, d), lambda b, i: (b, 0, 0))] + [_param_spec(p) for p in params]
                 + [cast_spec(w) for w in cast],
        out_specs=[tok(PA_COLS), tok(A_WIDTH), tok(B_WIDTH + C_WIDTH)] + [cast_spec(w) for w in cast],
        out_shape=[jax.ShapeDtypeStruct((bsz, t_len, PA_COLS), F32),
                   jax.ShapeDtypeStruct((bsz, t_len, A_WIDTH), F32),
                   jax.ShapeDtypeStruct((bsz, t_len, B_WIDTH + C_WIDTH), BF16)]
                  + [jax.ShapeDtypeStruct(w.shape, BF16) for w in cast],
        scratch_shapes=[pltpu.VMEM((nseg, seg + 2 * C_PAD, C_WIDTH), F32),
                        pltpu.VMEM((nseg, SUBLANES, seg + 2 * C_PAD - SUBLANES, C_WIDTH), F32)],
        compiler_params=_cparams(("arbitrary", "arbitrary")),
        name="in_proj_mix",
    )(x, mod, *[_param_arg(p) for p in params], *cast)


def _lower_bounds(logits, layer):
    depth = logits.shape[0]
    rows = [logits[j] for j in range(depth)]
    m = rows[0]
    for r in rows[1:]:
        m = jnp.maximum(m, r)
    es = [jnp.exp(r - m) for r in rows]
    tot = es[0]
    for e in es[1:]:
        tot = tot + e
    lb = jnp.zeros_like(m)
    for j in range(1, layer + 1):
        lb = lb + es[j] / tot
    return lb


def _chunk_rows(c, lo=0, n=A_CHUNK):
    return slice(c * A_CHUNK + lo, c * A_CHUNK + lo + n)


def _hgrn_gates(z, lb, rev, n_chunks):
    row = lax.broadcasted_iota(jnp.int32, (A_CHUNK, A_CHUNK), 0)
    col = lax.broadcasted_iota(jnp.int32, (A_CHUNK, A_CHUNK), 1)
    tri = ((col >= row) if rev else (col <= row)).astype(F32).astype(BF16)
    oml = 1.0 - lb
    t = oml * jax.nn.sigmoid(z)
    lf = jnp.log(lb + t)
    kk = oml - t
    hi = lf.astype(BF16)
    lo = (lf - hi.astype(F32)).astype(BF16)
    bs = [(jnp.dot(tri, hi[_chunk_rows(c)], preferred_element_type=F32)
           + jnp.dot(tri, lo[_chunk_rows(c)], preferred_element_type=F32)) * LOG2E
          for c in range(n_chunks)]
    return kk, bs


def _hgrn_factors(q, v, kk, bs, rev):
    sub = A_SUB
    wide, narrow = (0, 1) if rev else (1, 0)
    mid_row = sub // 2 if rev else sub // 2 - 1
    last_row = 0 if rev else A_CHUNK - 1
    halves = [slice(i * sub, (i + 1) * sub) for i in range(2)]
    per_chunk = {name: [] for name in ('qe', 'kd', 'qd', 'k_wide', 'k_narrow')}
    decs = []
    for c, bc in enumerate(bs):
        qc, kc = q[_chunk_rows(c)], kk[_chunk_rows(c)]
        mids = [bc[i * sub + mid_row:i * sub + mid_row + 1] for i in range(2)]
        last = bc[last_row:last_row + 1]
        decs.append(jnp.exp2(last))
        per_chunk['qe'].append((qc * jnp.exp2(bc)).astype(BF16))
        per_chunk['kd'].append((kc * jnp.exp2(last - bc)).astype(BF16))
        per_chunk['qd'].append(jnp.concatenate(
            [(qc[h] * jnp.exp2(jnp.minimum(bc[h] - mids[i], EXP2_CLAMP))).astype(BF16)
             for i, h in enumerate(halves)], axis=0))
        per_chunk['k_wide'].append((kc * jnp.exp2(jnp.minimum(mids[wide] - bc, EXP2_CLAMP))).astype(BF16))
        hn = halves[narrow]
        per_chunk['k_narrow'].append(
            (kc[hn] * jnp.exp2(jnp.minimum(mids[narrow] - bc[hn], EXP2_CLAMP))).astype(BF16))
    f = {name: jnp.concatenate(vals, axis=0) for name, vals in per_chunk.items()}
    f['dec'] = decs
    f['v'] = v.astype(BF16)
    return f


def _hgrn_scores(f, rev, c):
    sub = A_SUB
    wide, narrow = (0, 1) if rev else (1, 0)
    row_n = lax.broadcasted_iota(jnp.int32, (sub, sub), 0)
    col_n = lax.broadcasted_iota(jnp.int32, (sub, sub), 1)
    row_w = lax.broadcasted_iota(jnp.int32, (sub, A_CHUNK), 0)
    col_w = lax.broadcasted_iota(jnp.int32, (sub, A_CHUNK), 1)
    mask_n = (col_n >= row_n) if rev else (col_n <= row_n)
    mask_w = (col_w >= row_w) if rev else (col_w <= row_w + sub)
    per_head = []
    for h in range(A_HEADS):
        hc = slice(h * A_DK, (h + 1) * A_DK)
        s_n = lax.dot_general(f['qd'][_chunk_rows(c, narrow * sub, sub), hc],
                              f['k_narrow'][c * sub:(c + 1) * sub, hc], _NT, preferred_element_type=F32)
        s_w = lax.dot_general(f['qd'][_chunk_rows(c, wide * sub, sub), hc],
                              f['k_wide'][_chunk_rows(c), hc], _NT, preferred_element_type=F32)
        per_head.append((jnp.where(mask_n, s_n, 0.0).astype(BF16),
                         jnp.where(mask_w, s_w, 0.0).astype(BF16)))
    return per_head


def _hgrn_intra(f, scores, rev, c):
    sub = A_SUB
    narrow = 1 if rev else 0
    per_head = []
    for h in range(A_HEADS):
        hc = slice(h * A_DK, (h + 1) * A_DK)
        s_n, s_w = scores[h]
        o_n = jnp.dot(s_n, f['v'][_chunk_rows(c, narrow * sub, sub), hc], preferred_element_type=F32)
        o_w = jnp.dot(s_w, f['v'][_chunk_rows(c), hc], preferred_element_type=F32)
        intra = jnp.concatenate([o_w, o_n] if rev else [o_n, o_w], axis=0)
        upd = lax.dot_general(f['v'][_chunk_rows(c), hc], f['kd'][_chunk_rows(c), hc], _TN,
                              preferred_element_type=F32)
        per_head.append((intra, upd))
    return per_head


def _hgrn_stages(q, z, v, lb, rev, n_chunks, get_state, set_state, put_out, group=HGRN_GROUP, lag=HGRN_LAG):
    n_groups = n_chunks // group
    order = list(range(n_groups - 1, -1, -1) if rev else range(n_groups))
    facs, scores, intra = {}, {}, {}
    states = [get_state(h) for h in range(A_HEADS)]
    for step in range(n_groups + 3 * lag):
        if step < n_groups:
            g = order[step]
            rows = slice(g * group * A_CHUNK, (g + 1) * group * A_CHUNK)
            facs[g] = _hgrn_factors(q[rows], v[rows], *_hgrn_gates(z[rows], lb, rev, group), rev)
        if 0 <= step - lag < n_groups:
            g = order[step - lag]
            scores[g] = [_hgrn_scores(facs[g], rev, c) for c in range(group)]
        if 0 <= step - 2 * lag < n_groups:
            g = order[step - 2 * lag]
            intra[g] = [_hgrn_intra(facs[g], scores[g][c], rev, c) for c in range(group)]
        if 0 <= step - 3 * lag < n_groups:
            g = order[step - 3 * lag]
            f = facs[g]
            for c in (range(group - 1, -1, -1) if rev else range(group)):
                outs = []
                for h in range(A_HEADS):
                    hc = slice(h * A_DK, (h + 1) * A_DK)
                    inter = lax.dot_general(f['qe'][_chunk_rows(c), hc], states[h].astype(BF16), _NT,
                                            preferred_element_type=F32)
                    outs.append(inter + intra[g][c][h][0])
                    states[h] = states[h] * f['dec'][c][:, hc] + intra[g][c][h][1]
                put_out(g * group + c, jnp.concatenate(outs, axis=1))
        yield
    for h in range(A_HEADS):
        set_state(h, states[h])


def _hgrn_kernel(qf_ref, zf_ref, vf_ref, qb_ref, zb_ref, vb_ref, lbl_ref, s0_ref,
                 of_ref, ob_ref, st_ref, *, layer, n_chunks):
    @pl.when(pl.program_id(1) == 0)
    def _():
        st_ref[...] = s0_ref[...]

    lb = _lower_bounds(lbl_ref[...], layer)

    def direction(d, q_ref, z_ref, v_ref, o_ref):
        def set_state(h, val):
            st_ref[0, d, h] = val

        def put_out(c, val):
            o_ref[0, _chunk_rows(c), :] = val

        return _hgrn_stages(q_ref[0], z_ref[0], v_ref[0], lb[d:d + 1], d == 1, n_chunks,
                            lambda h: st_ref[0, d, h], set_state, put_out)

    _round_robin([direction(0, qf_ref, zf_ref, vf_ref, of_ref),
                  direction(1, qb_ref, zb_ref, vb_ref, ob_ref)])


def _hgrn(pa, lb_logits, s0, layer, tile):
    bsz, t_len, _ = pa.shape
    nb = t_len // tile
    depth = lb_logits.shape[0]

    def col_spec(colblk, rev):
        if rev:
            return pl.BlockSpec((1, tile, A_WIDTH), lambda b, i: (b, nb - 1 - i, colblk))
        return pl.BlockSpec((1, tile, A_WIDTH), lambda b, i: (b, i, colblk))

    st_spec = pl.BlockSpec((1, 2, A_HEADS, A_DK, A_DK), lambda b, i: (b, 0, 0, 0, 0))
    o_sds = jax.ShapeDtypeStruct((bsz, t_len, A_WIDTH), F32)
    return pl.pallas_call(
        functools.partial(_hgrn_kernel, layer=layer, n_chunks=tile // A_CHUNK),
        grid=(bsz, nb),
        in_specs=[col_spec(0, False), col_spec(1, False), col_spec(3, False),
                  col_spec(0, True), col_spec(2, True), col_spec(3, True),
                  _full((depth, 2, A_WIDTH)), st_spec],
        out_specs=[col_spec(0, False), col_spec(0, True), st_spec],
        out_shape=[o_sds, o_sds, jax.ShapeDtypeStruct(s0.shape, F32)],
        compiler_params=_cparams(("arbitrary", "arbitrary")),
        name="hgrn",
    )(pa, pa, pa, pa, pa, pa, lb_logits, s0)


def _out_ffn_kernel(x_ref, of_ref, ob_ref, gate_ref, ybc_ref, mod_ref, og_ref, wo_ref,
                    g_ref, w13_ref, w2_ref, fg_ref, o_ref, *, d_ff, chunk, final_norm):
    o = of_ref[0] + ob_ref[0]
    parts = []
    for h in range(A_HEADS):
        oh = o[:, h * A_DK:(h + 1) * A_DK]
        parts.append(oh * lax.rsqrt(jnp.mean(oh * oh, axis=-1, keepdims=True) + EPS))
    gate = gate_ref[0]
    ya = jnp.concatenate(parts, axis=1) * og_ref[...] * (gate * jax.nn.sigmoid(gate))
    y = (jnp.dot(ya.astype(BF16), wo_ref[0:A_WIDTH, :].astype(BF16), preferred_element_type=F32)
         + jnp.dot(ybc_ref[0], wo_ref[A_WIDTH:, :].astype(BF16), preferred_element_type=F32))
    x = x_ref[0] + mod_ref[0, 2:3, :] * y
    h = _norm_modulate(x, g_ref[...], mod_ref[0, 3:4, :], mod_ref[0, 4:5, :]).astype(BF16)
    acts = []
    for c in range(d_ff // chunk):
        a = jnp.dot(h, w13_ref[:, c * chunk:(c + 1) * chunk], preferred_element_type=F32)
        b = jnp.dot(h, w13_ref[:, d_ff + c * chunk:d_ff + (c + 1) * chunk], preferred_element_type=F32)
        acts.append((a * jax.nn.sigmoid(a) * b).astype(BF16))
    acc = jnp.dot(jnp.concatenate(acts, axis=1), w2_ref[...], preferred_element_type=F32)
    y = x + mod_ref[0, 5:6, :] * acc
    if final_norm:
        y = y * lax.rsqrt(jnp.mean(y * y, axis=-1, keepdims=True) + EPS) * fg_ref[...]
    o_ref[0] = y


def _out_ffn(x, o_f, o_b, gate, ybc, mod, lw, final_g, tile, final_norm):
    bsz, t_len, d = x.shape
    d_ff = lw['ffn_w2'].shape[0]

    def tok(cols):
        return pl.BlockSpec((1, tile, cols), lambda b, i: (b, i, 0))

    params = [lw['onorm_g'], lw['w_out'], lw['norm2_g'], lw['ffn_w13'], lw['ffn_w2'], final_g.reshape(1, d)]
    return pl.pallas_call(
        functools.partial(_out_ffn_kernel, d_ff=d_ff, chunk=FFN_CHUNK, final_norm=final_norm),
        grid=(bsz, t_len // tile),
        in_specs=[tok(d), tok(A_WIDTH), tok(A_WIDTH), tok(A_WIDTH), tok(B_WIDTH + C_WIDTH),
                  pl.BlockSpec((1, 6, d), lambda b, i: (b, 0, 0))] + [_param_spec(p) for p in params],
        out_specs=tok(d),
        out_shape=jax.ShapeDtypeStruct(x.shape, F32),
        compiler_params=_cparams(("arbitrary", "arbitrary")),
        name="out_ffn",
    )(x, o_f, o_b, gate, ybc, mod, *[_param_arg(p) for p in params])


def _layer_params(l, gavg, mxu_weights, norm1_g, hgrn_onorm_g, gmlp_ln_g, gmlp_ln_b, gmlp_b_s,
                  conv_dw_w, conv_dw_b, conv_ln_g, conv_ln_b, conv_pw_b, norm2_g):
    row = lambda a: a.reshape(1, -1).astype(F32)
    lw = {name: _LayerSlice(w, l) for name, w in mxu_weights.items()}
    lw.update({
        'norm1_g': row(norm1_g[l]), 'onorm_g': row(hgrn_onorm_g[l]), 'gavg': gavg,
        'gmlp_ln_g': row(gmlp_ln_g[l]), 'gmlp_ln_b': row(gmlp_ln_b[l]),
        'bs_tile': jnp.repeat(gmlp_b_s[l].T, B_WIDTH // B_GROUPS, axis=1).astype(F32),
        'dw_w': conv_dw_w[l].astype(F32), 'dw_b': row(conv_dw_b[l]),
        'conv_ln_g': row(conv_ln_g[l]), 'conv_ln_b': row(conv_ln_b[l]),
        'pw_b': row(conv_pw_b[l]), 'norm2_g': row(norm2_g[l]),
    })
    return lw


def kernel(x, c, ctx, c_ctx, ada_w, ada_b, norm1_g, w_in, hgrn_lb_logits, hgrn_onorm_g, gmlp_ln_g, gmlp_ln_b, gmlp_w_s, gmlp_b_s, conv_dw_w, conv_dw_b, conv_ln_g, conv_ln_b, conv_pw_w, conv_pw_b, w_out, norm2_g, ffn_w13, ffn_w2, final_norm_g):
    bsz, t_len, d = x.shape
    ctx_len = ctx.shape[1]
    depth = ada_w.shape[0]
    lb_logits = hgrn_lb_logits.astype(F32)

    cs = jnp.concatenate([c, c_ctx[None, :], jnp.zeros((MOD_ROWS - bsz - 1, d), F32)], axis=0)
    mod_all = _modulation(cs, ada_w, ada_b)
    grp = jnp.arange(B_WIDTH) // (B_WIDTH // B_GROUPS)
    gavg = ((grp[:, None] == grp[None, :]).astype(F32) / (B_WIDTH // B_GROUPS)).astype(BF16)
    zero_state = jnp.zeros((bsz, 2, A_HEADS, A_DK, A_DK), F32)
    mxu_weights = {
        'w_in': w_in, 'w_out': w_out,
        'pw_w': conv_pw_w.astype(BF16),
        'ws_cat': gmlp_w_s.reshape(depth, B_GROUPS * B_CHUNK, B_CHUNK).astype(BF16),
    }
    row_len = t_len // (t_len // GRID_W)
    ctx_flat = ctx.reshape(1, bsz * ctx_len, d)
    ctx_tile = min(LATENT_TILE, bsz * ctx_len)

    for l in range(depth):
        last = l == depth - 1
        lw = _layer_params(l, gavg, mxu_weights, norm1_g, hgrn_onorm_g, gmlp_ln_g, gmlp_ln_b, gmlp_b_s,
                           conv_dw_w, conv_dw_b, conv_ln_g, conv_ln_b, conv_pw_b, norm2_g)
        mod = mod_all[l, :bsz].reshape(bsz, 6, d)
        mod_ctx = mod_all[l, bsz].reshape(1, 6, d)
        if l == 0:
            ffn_f32 = (ffn_w13.reshape(-1, ffn_w13.shape[-1]), ffn_w2.reshape(-1, ffn_w2.shape[-1]))
            pa, gate, ybc, w13_bf16, w2_bf16 = _in_proj_mix(x, mod, lw, LATENT_TILE, row_len, ffn_f32)
            mxu_weights['ffn_w13'] = w13_bf16.reshape(ffn_w13.shape)
            mxu_weights['ffn_w2'] = w2_bf16.reshape(ffn_w2.shape)
        else:
            pa, gate, ybc = _in_proj_mix(x, mod, lw, LATENT_TILE, row_len)
        lw.update({name: _LayerSlice(mxu_weights[name], l) for name in ('ffn_w13', 'ffn_w2')})
        if last:
            pa_c = _in_proj(ctx_flat, mod_ctx, lw, ctx_tile)
            _, _, state = _hgrn(pa_c.reshape(bsz, ctx_len, PA_COLS), lb_logits, zero_state, l, ctx_len)
        else:
            pa_c, gate_c, ybc_c = _in_proj_mix(ctx_flat, mod_ctx, lw, ctx_tile, ctx_len)
            of_c, ob_c, state = _hgrn(pa_c.reshape(bsz, ctx_len, PA_COLS), lb_logits, zero_state, l, ctx_len)
            ctx_flat = _out_ffn(ctx_flat, of_c.reshape(1, -1, A_WIDTH), ob_c.reshape(1, -1, A_WIDTH),
                                gate_c, ybc_c, mod_ctx, lw, final_norm_g, ctx_tile, False)
        o_f, o_b, _ = _hgrn(pa, lb_logits, state, l, SCAN_TILE)
        x = _out_ffn(x, o_f, o_b, gate, ybc, mod, lw, final_norm_g, LATENT_TILE, last)
    return x
```

```python
import functools
from typing import NamedTuple

import jax
import jax.numpy as jnp
from jax import lax
from jax.experimental import pallas as pl
from jax.experimental.pallas import tpu as pltpu

F32 = jnp.float32
BF16 = jnp.bfloat16

D_MODEL = 1024
GRID_W = 64
A_HEADS = 4
A_DK = 128
A_WIDTH = A_HEADS * A_DK
A_CHUNK = 64
A_SUB = 32
HGRN_GROUP = 1
HGRN_LAG = 1
B_GROUPS = 4
B_WIDTH = 256
B_CHUNK = 128
C_GROUPS = 4
C_WIDTH = 256
C_KERNEL = 31
C_PAD = 16
SUBLANES = 8
PROJ_CHUNK = 256
FFN_CHUNK = 256
PA_COLS = 4 * A_WIDTH
EPS = 1e-6
EXP2_CLAMP = 80.0
LOG2E = 1.4426950408889634
MOD_ROWS = 8
LATENT_TILE = 512
SCAN_TILE = 1024

V7X_VMEM_LIMIT = 56 * 1024 * 1024

_NT = (((1,), (1,)), ((), ()))
_TN = (((0,), (0,)), ((), ()))
_FINISHED = object()


def _cparams(sem):
    return pltpu.CompilerParams(dimension_semantics=sem, vmem_limit_bytes=V7X_VMEM_LIMIT)


def _full(shape):
    n = len(shape)
    return pl.BlockSpec(shape, lambda *_: (0,) * n)


class _LayerSlice(NamedTuple):
    stacked: jax.Array
    layer: int

    @property
    def shape(self):
        return self.stacked.shape[1:]


def _param_spec(p, block=None):
    if not isinstance(p, _LayerSlice):
        return _full(p.shape)
    block = p.shape if block is None else block
    index = (p.layer,) + (0,) * len(block)
    return pl.BlockSpec((None,) + tuple(block), lambda *_: index, pipeline_mode=pl.Buffered(1))


def _param_arg(p):
    return p.stacked if isinstance(p, _LayerSlice) else p


class _ModRows(NamedTuple):
    rows: jax.Array
    layer: int
    shared_row: int | None = None

    def spec(self):
        layer, row = self.layer, self.shared_row
        block = (None, 1) + self.rows.shape[2:]
        if row is None:
            return pl.BlockSpec(block, lambda b, i: (layer, b, 0, 0))
        return pl.BlockSpec(block, lambda b, i: (layer, row, 0, 0))


def _round_robin(branches, between=None):
    while branches:
        branches = [g for g in branches if next(g, _FINISHED) is not _FINISHED]
        if between is not None:
            between()


def _mod_kernel(cs_ref, w_ref, b_ref, o_ref):
    s = cs_ref[...]
    s = s * jax.nn.sigmoid(s)
    w = w_ref[0]
    s_hi = s.astype(BF16)
    s_lo = (s - s_hi.astype(F32)).astype(BF16)
    w_hi = w.astype(BF16)
    w_lo = (w - w_hi.astype(F32)).astype(BF16)
    r = jnp.dot(jnp.concatenate([s_hi, s_lo], axis=0), w_hi, preferred_element_type=F32)
    o_ref[0] = (r[:MOD_ROWS] + r[MOD_ROWS:] + jnp.dot(s_hi, w_lo, preferred_element_type=F32)
                + b_ref[0])


def _modulation(cs, ada_w, ada_b):
    depth, d, n = ada_w.shape
    tn = 2048
    return pl.pallas_call(
        _mod_kernel,
        grid=(depth, n // tn),
        in_specs=[
            pl.BlockSpec((MOD_ROWS, d), lambda l, j: (0, 0)),
            pl.BlockSpec((1, d, tn), lambda l, j: (l, 0, j)),
            pl.BlockSpec((1, 1, tn), lambda l, j: (l, 0, j)),
        ],
        out_specs=pl.BlockSpec((1, MOD_ROWS, tn), lambda l, j: (l, 0, j)),
        out_shape=jax.ShapeDtypeStruct((depth, MOD_ROWS, n), F32),
        compiler_params=_cparams(("arbitrary", "arbitrary")),
        name="modulation",
    )(cs, ada_w, ada_b.reshape(depth, 1, n))


def _norm_modulate(x, g, shift, scale):
    ms = jnp.mean(x * x, axis=-1, keepdims=True)
    return x * lax.rsqrt(ms + EPS) * (g * (1.0 + scale)) + shift


def _group_mean(x, gavg):
    return jnp.dot(x.astype(BF16), gavg, preferred_element_type=F32)


def _group_layer_norm_stages(x, gavg, g, b, out):
    mu = _group_mean(x, gavg)
    yield
    xc = x - mu
    var = _group_mean(xc * xc, gavg)
    yield
    out['ln'] = xc * lax.rsqrt(var + EPS) * g + b


def _gmlp_stages(project, gavg, ln_g, ln_b, ws_ref, bs_ref, out):
    u = project(0)
    v = project(B_WIDTH)
    tile = u.shape[0]
    yield
    ln = {}
    yield from _group_layer_norm_stages(jax.nn.gelu(v), gavg, ln_g, ln_b, ln)
    vn = ln['ln'].astype(BF16)
    gu = jax.nn.gelu(u)
    lane_group = lax.broadcasted_iota(jnp.int32, (B_CHUNK, B_WIDTH), 1) // (B_WIDTH // B_GROUPS)
    mixed = []
    for n in range(tile // B_CHUNK):
        yield
        r = jnp.dot(ws_ref[...], vn[n * B_CHUNK:(n + 1) * B_CHUNK], preferred_element_type=F32)
        m = bs_ref[...]
        for g in range(B_GROUPS):
            m = m + jnp.where(lane_group == g, r[g * B_CHUNK:(g + 1) * B_CHUNK], 0.0)
        mixed.append(m)
    out['yb'] = gu * jnp.concatenate(mixed, axis=0)


def _conv_stages(ca, cgate, gavg, dww_ref, dw_b, ln_g, ln_b, pww_ref, pw_b, zpad_ref, zsh_ref, seg, out):
    tile = ca.shape[0]
    z = ca * jax.nn.sigmoid(cgate)
    nseg = tile // seg
    pad_zeros = jnp.zeros((C_PAD, C_WIDTH), F32)
    for s in range(nseg):
        zpad_ref[s, 0:C_PAD, :] = pad_zeros
        zpad_ref[s, C_PAD + seg:2 * C_PAD + seg, :] = pad_zeros
        zpad_ref[s, C_PAD:C_PAD + seg, :] = z[s * seg:(s + 1) * seg]
    span = seg + 2 * C_PAD - SUBLANES
    convs = []
    for s in range(nseg):
        for r in range(1, SUBLANES):
            zsh_ref[s, r] = zpad_ref[s, r:r + span, :]
        acc = jnp.zeros((seg, C_WIDTH), F32)
        for j in range(C_KERNEL):
            start = C_PAD - C_KERNEL // 2 + j
            r, a = start % SUBLANES, start - start % SUBLANES
            tap = zpad_ref[s, a:a + seg, :] if r == 0 else zsh_ref[s, r, a:a + seg, :]
            acc = acc + dww_ref[j:j + 1, :] * tap
        convs.append(acc)
        yield
    ln = {}
    yield from _group_layer_norm_stages(jnp.concatenate(convs, axis=0) + dw_b, gavg, ln_g, ln_b, ln)
    zc = ln['ln']
    zc = (zc * jax.nn.sigmoid(zc)).astype(BF16)
    out['yc'] = jnp.dot(zc, pww_ref[...], preferred_element_type=F32) + pw_b


_MIX_PARAMS = ('norm1_g', 'w_in', 'gavg', 'gmlp_ln_g', 'gmlp_ln_b', 'ws_cat', 'bs_tile',
               'dw_w', 'dw_b', 'conv_ln_g', 'conv_ln_b', 'pw_w', 'pw_b')


def _in_proj_kernel(x_ref, mod_ref, g_ref, w_ref, pa_ref):
    h = _norm_modulate(x_ref[0], g_ref[...], mod_ref[0, 0:1, :], mod_ref[0, 1:2, :]).astype(BF16)
    pa_ref[0] = jnp.dot(h, w_ref[...].astype(BF16), preferred_element_type=F32)


def _in_proj(x, mod, lw, tile):
    bsz, t_len, d = x.shape
    return pl.pallas_call(
        _in_proj_kernel,
        grid=(bsz, t_len // tile),
        in_specs=[
            pl.BlockSpec((1, tile, d), lambda b, i: (b, i, 0)),
            mod.spec(),
            _param_spec(lw['norm1_g']),
            _param_spec(lw['w_in'], (d, PA_COLS)),
        ],
        out_specs=pl.BlockSpec((1, tile, PA_COLS), lambda b, i: (b, i, 0)),
        out_shape=jax.ShapeDtypeStruct((bsz, t_len, PA_COLS), F32),
        compiler_params=_cparams(("arbitrary", "arbitrary")),
        name="in_proj",
    )(x, mod.rows, _param_arg(lw['norm1_g']), _param_arg(lw['w_in']))


def _in_proj_mix_kernel(x_ref, mod_ref, *refs, seg, n_cast):
    p = dict(zip(_MIX_PARAMS, refs))
    refs = refs[len(_MIX_PARAMS):]
    cast_in, refs = refs[:n_cast], refs[n_cast:]
    pa_ref, gate_ref, ybc_ref = refs[:3]
    cast_out, (zpad_ref, zsh_ref) = refs[3:3 + n_cast], refs[3 + n_cast:]
    for src, dst in zip(cast_in, cast_out):
        dst[...] = src[...].astype(BF16)
    h = _norm_modulate(x_ref[0], p['norm1_g'][...], mod_ref[0, 0:1, :], mod_ref[0, 1:2, :]).astype(BF16)
    gavg = p['gavg'][...]

    def proj(lo, n):
        return jnp.dot(h, p['w_in'][:, lo:lo + n].astype(BF16), preferred_element_type=F32)

    chunks = [(pa_ref, lo) for lo in range(0, PA_COLS, PROJ_CHUNK)]
    chunks += [(gate_ref, lo) for lo in range(0, A_WIDTH, PROJ_CHUNK)]
    pending = iter(chunks)

    def project_next():
        nxt = next(pending, None)
        if nxt is not None:
            ref, lo = nxt
            w_lo = lo if ref is pa_ref else PA_COLS + lo
            ref[0, :, lo:lo + PROJ_CHUNK] = proj(w_lo, PROJ_CHUNK)

    b0 = PA_COLS + A_WIDTH
    c0 = b0 + 2 * B_WIDTH
    out = {}
    branches = [
        _conv_stages(proj(c0, C_WIDTH), proj(c0 + C_WIDTH, C_WIDTH), gavg, p['dw_w'], p['dw_b'][...],
                     p['conv_ln_g'][...], p['conv_ln_b'][...], p['pw_w'], p['pw_b'][...],
                     zpad_ref, zsh_ref, seg, out),
        _gmlp_stages(lambda lo: proj(b0 + lo, B_WIDTH), gavg, p['gmlp_ln_g'][...], p['gmlp_ln_b'][...],
                     p['ws_cat'], p['bs_tile'], out),
    ]
    _round_robin(branches, project_next)
    ybc_ref[0] = jnp.concatenate([out['yb'], out['yc']], axis=1).astype(BF16)
    for _ in chunks:
        project_next()


def _in_proj_mix(x, mod, lw, tile, seg, cast=()):
    bsz, t_len, d = x.shape
    nseg = tile // seg
    nb = t_len // tile

    def tok(cols):
        return pl.BlockSpec((1, tile, cols), lambda b, i: (b, i, 0))

    def cast_spec(w):
        return pl.BlockSpec((w.shape[0] // (bsz * nb), w.shape[1]), lambda b, i: (b * nb + i, 0))

    params = [lw[name] for name in _MIX_PARAMS]
    return pl.pallas_call(
        functools.partial(_in_proj_mix_kernel, seg=seg, n_cast=len(cast)),
        grid=(bsz, nb),
        in_specs=[tok(d), mod.spec()] + [_param_spec(p) for p in params] + [cast_spec(w) for w in cast],
        out_specs=[tok(PA_COLS), tok(A_WIDTH), tok(B_WIDTH + C_WIDTH)] + [cast_spec(w) for w in cast],
        out_shape=[jax.ShapeDtypeStruct((bsz, t_len, PA_COLS), F32),
                   jax.ShapeDtypeStruct((bsz, t_len, A_WIDTH), F32),
                   jax.ShapeDtypeStruct((bsz, t_len, B_WIDTH + C_WIDTH), BF16)]
                  + [jax.ShapeDtypeStruct(w.shape, BF16) for w in cast],
        scratch_shapes=[pltpu.VMEM((nseg, seg + 2 * C_PAD, C_WIDTH), F32),
                        pltpu.VMEM((nseg, SUBLANES, seg + 2 * C_PAD - SUBLANES, C_WIDTH), F32)],
        compiler_params=_cparams(("arbitrary", "arbitrary")),
        name="in_proj_mix",
    )(x, mod.rows, *[_param_arg(p) for p in params], *cast)


def _lower_bounds(logits, layer):
    depth = logits.shape[0]
    rows = [logits[j] for j in range(depth)]
    m = rows[0]
    for r in rows[1:]:
        m = jnp.maximum(m, r)
    es = [jnp.exp(r - m) for r in rows]
    tot = es[0]
    for e in es[1:]:
        tot = tot + e
    lb = jnp.zeros_like(m)
    for j in range(1, layer + 1):
        lb = lb + es[j] / tot
    return lb


def _chunk_rows(c, lo=0, n=A_CHUNK):
    return slice(c * A_CHUNK + lo, c * A_CHUNK + lo + n)


def _hgrn_gates(z, lb, rev, n_chunks):
    row = lax.broadcasted_iota(jnp.int32, (A_CHUNK, A_CHUNK), 0)
    col = lax.broadcasted_iota(jnp.int32, (A_CHUNK, A_CHUNK), 1)
    tri = ((col >= row) if rev else (col <= row)).astype(F32).astype(BF16)
    oml = 1.0 - lb
    t = oml * jax.nn.sigmoid(z)
    lf = jnp.log(lb + t)
    kk = oml - t
    hi = lf.astype(BF16)
    lo = (lf - hi.astype(F32)).astype(BF16)
    bs = [(jnp.dot(tri, hi[_chunk_rows(c)], preferred_element_type=F32)
           + jnp.dot(tri, lo[_chunk_rows(c)], preferred_element_type=F32)) * LOG2E
          for c in range(n_chunks)]
    return kk, bs


def _hgrn_factors(q, v, kk, bs, rev):
    sub = A_SUB
    wide, narrow = (0, 1) if rev else (1, 0)
    mid_row = sub // 2 if rev else sub // 2 - 1
    last_row = 0 if rev else A_CHUNK - 1
    halves = [slice(i * sub, (i + 1) * sub) for i in range(2)]
    per_chunk = {name: [] for name in ('qe', 'kd', 'qd', 'k_wide', 'k_narrow')}
    decs = []
    for c, bc in enumerate(bs):
        qc, kc = q[_chunk_rows(c)], kk[_chunk_rows(c)]
        mids = [bc[i * sub + mid_row:i * sub + mid_row + 1] for i in range(2)]
        last = bc[last_row:last_row + 1]
        decs.append(jnp.exp2(last))
        per_chunk['qe'].append((qc * jnp.exp2(bc)).astype(BF16))
        per_chunk['kd'].append((kc * jnp.exp2(last - bc)).astype(BF16))
        per_chunk['qd'].append(jnp.concatenate(
            [(qc[h] * jnp.exp2(jnp.minimum(bc[h] - mids[i], EXP2_CLAMP))).astype(BF16)
             for i, h in enumerate(halves)], axis=0))
        per_chunk['k_wide'].append((kc * jnp.exp2(jnp.minimum(mids[wide] - bc, EXP2_CLAMP))).astype(BF16))
        hn = halves[narrow]
        per_chunk['k_narrow'].append(
            (kc[hn] * jnp.exp2(jnp.minimum(mids[narrow] - bc[hn], EXP2_CLAMP))).astype(BF16))
    f = {name: jnp.concatenate(vals, axis=0) for name, vals in per_chunk.items()}
    f['dec'] = decs
    f['v'] = v.astype(BF16)
    return f


def _hgrn_scores(f, rev, c):
    sub = A_SUB
    wide, narrow = (0, 1) if rev else (1, 0)
    row_n = lax.broadcasted_iota(jnp.int32, (sub, sub), 0)
    col_n = lax.broadcasted_iota(jnp.int32, (sub, sub), 1)
    row_w = lax.broadcasted_iota(jnp.int32, (sub, A_CHUNK), 0)
    col_w = lax.broadcasted_iota(jnp.int32, (sub, A_CHUNK), 1)
    mask_n = (col_n >= row_n) if rev else (col_n <= row_n)
    mask_w = (col_w >= row_w) if rev else (col_w <= row_w + sub)
    per_head = []
    for h in range(A_HEADS):
        hc = slice(h * A_DK, (h + 1) * A_DK)
        s_n = lax.dot_general(f['qd'][_chunk_rows(c, narrow * sub, sub), hc],
                              f['k_narrow'][c * sub:(c + 1) * sub, hc], _NT, preferred_element_type=F32)
        s_w = lax.dot_general(f['qd'][_chunk_rows(c, wide * sub, sub), hc],
                              f['k_wide'][_chunk_rows(c), hc], _NT, preferred_element_type=F32)
        per_head.append((jnp.where(mask_n, s_n, 0.0).astype(BF16),
                         jnp.where(mask_w, s_w, 0.0).astype(BF16)))
    return per_head


def _hgrn_intra(f, scores, rev, c):
    sub = A_SUB
    narrow = 1 if rev else 0
    per_head = []
    for h in range(A_HEADS):
        hc = slice(h * A_DK, (h + 1) * A_DK)
        s_n, s_w = scores[h]
        o_n = jnp.dot(s_n, f['v'][_chunk_rows(c, narrow * sub, sub), hc], preferred_element_type=F32)
        o_w = jnp.dot(s_w, f['v'][_chunk_rows(c), hc], preferred_element_type=F32)
        intra = jnp.concatenate([o_w, o_n] if rev else [o_n, o_w], axis=0)
        upd = lax.dot_general(f['v'][_chunk_rows(c), hc], f['kd'][_chunk_rows(c), hc], _TN,
                              preferred_element_type=F32)
        per_head.append((intra, upd))
    return per_head


def _hgrn_stages(q, z, v, lb, rev, n_chunks, get_state, set_state, put_out, group=HGRN_GROUP, lag=HGRN_LAG):
    n_groups = n_chunks // group
    order = list(range(n_groups - 1, -1, -1) if rev else range(n_groups))
    facs, scores, intra = {}, {}, {}
    states = [get_state(h) for h in range(A_HEADS)]
    for step in range(n_groups + 3 * lag):
        if step < n_groups:
            g = order[step]
            rows = slice(g * group * A_CHUNK, (g + 1) * group * A_CHUNK)
            facs[g] = _hgrn_factors(q[rows], v[rows], *_hgrn_gates(z[rows], lb, rev, group), rev)
        if 0 <= step - lag < n_groups:
            g = order[step - lag]
            scores[g] = [_hgrn_scores(facs[g], rev, c) for c in range(group)]
        if 0 <= step - 2 * lag < n_groups:
            g = order[step - 2 * lag]
            intra[g] = [_hgrn_intra(facs[g], scores[g][c], rev, c) for c in range(group)]
        if 0 <= step - 3 * lag < n_groups:
            g = order[step - 3 * lag]
            f = facs[g]
            for c in (range(group - 1, -1, -1) if rev else range(group)):
                outs = []
                for h in range(A_HEADS):
                    hc = slice(h * A_DK, (h + 1) * A_DK)
                    inter = lax.dot_general(f['qe'][_chunk_rows(c), hc], states[h].astype(BF16), _NT,
                                            preferred_element_type=F32)
                    outs.append(inter + intra[g][c][h][0])
                    states[h] = states[h] * f['dec'][c][:, hc] + intra[g][c][h][1]
                put_out(g * group + c, jnp.concatenate(outs, axis=1))
        yield
    for h in range(A_HEADS):
        set_state(h, states[h])


def _hgrn_kernel(qf_ref, zf_ref, vf_ref, qb_ref, zb_ref, vb_ref, lbl_ref, s0_ref,
                 of_ref, ob_ref, st_ref, *, layer, n_chunks):
    @pl.when(pl.program_id(1) == 0)
    def _():
        st_ref[...] = s0_ref[...]

    lb = _lower_bounds(lbl_ref[...], layer)

    def direction(d, q_ref, z_ref, v_ref, o_ref):
        def set_state(h, val):
            st_ref[0, d, h] = val

        def put_out(c, val):
            o_ref[0, _chunk_rows(c), :] = val

        return _hgrn_stages(q_ref[0], z_ref[0], v_ref[0], lb[d:d + 1], d == 1, n_chunks,
                            lambda h: st_ref[0, d, h], set_state, put_out)

    _round_robin([direction(0, qf_ref, zf_ref, vf_ref, of_ref),
                  direction(1, qb_ref, zb_ref, vb_ref, ob_ref)])


def _hgrn(pa, lb_logits, s0, layer, tile):
    bsz, t_len, _ = pa.shape
    nb = t_len // tile
    depth = lb_logits.shape[0]

    def col_spec(colblk, rev):
        if rev:
            return pl.BlockSpec((1, tile, A_WIDTH), lambda b, i: (b, nb - 1 - i, colblk))
        return pl.BlockSpec((1, tile, A_WIDTH), lambda b, i: (b, i, colblk))

    st_spec = pl.BlockSpec((1, 2, A_HEADS, A_DK, A_DK), lambda b, i: (b, 0, 0, 0, 0))
    o_sds = jax.ShapeDtypeStruct((bsz, t_len, A_WIDTH), F32)
    return pl.pallas_call(
        functools.partial(_hgrn_kernel, layer=layer, n_chunks=tile // A_CHUNK),
        grid=(bsz, nb),
        in_specs=[col_spec(0, False), col_spec(1, False), col_spec(3, False),
                  col_spec(0, True), col_spec(2, True), col_spec(3, True),
                  _full((depth, 2, A_WIDTH)), st_spec],
        out_specs=[col_spec(0, False), col_spec(0, True), st_spec],
        out_shape=[o_sds, o_sds, jax.ShapeDtypeStruct(s0.shape, F32)],
        compiler_params=_cparams(("arbitrary", "arbitrary")),
        name="hgrn",
    )(pa, pa, pa, pa, pa, pa, lb_logits, s0)


def _out_ffn_kernel(x_ref, of_ref, ob_ref, gate_ref, ybc_ref, mod_ref, og_ref, wo_ref,
                    g_ref, w13_ref, w2_ref, fg_ref, o_ref, *, d_ff, chunk, final_norm):
    o = of_ref[0] + ob_ref[0]
    parts = []
    for h in range(A_HEADS):
        oh = o[:, h * A_DK:(h + 1) * A_DK]
        parts.append(oh * lax.rsqrt(jnp.mean(oh * oh, axis=-1, keepdims=True) + EPS))
    gate = gate_ref[0]
    ya = jnp.concatenate(parts, axis=1) * og_ref[...] * (gate * jax.nn.sigmoid(gate))
    y = (jnp.dot(ya.astype(BF16), wo_ref[0:A_WIDTH, :].astype(BF16), preferred_element_type=F32)
         + jnp.dot(ybc_ref[0], wo_ref[A_WIDTH:, :].astype(BF16), preferred_element_type=F32))
    x = x_ref[0] + mod_ref[0, 2:3, :] * y
    h = _norm_modulate(x, g_ref[...], mod_ref[0, 3:4, :], mod_ref[0, 4:5, :]).astype(BF16)
    acts = []
    for c in range(d_ff // chunk):
        a = jnp.dot(h, w13_ref[:, c * chunk:(c + 1) * chunk], preferred_element_type=F32)
        b = jnp.dot(h, w13_ref[:, d_ff + c * chunk:d_ff + (c + 1) * chunk], preferred_element_type=F32)
        acts.append((a * jax.nn.sigmoid(a) * b).astype(BF16))
    acc = jnp.dot(jnp.concatenate(acts, axis=1), w2_ref[...], preferred_element_type=F32)
    y = x + mod_ref[0, 5:6, :] * acc
    if final_norm:
        y = y * lax.rsqrt(jnp.mean(y * y, axis=-1, keepdims=True) + EPS) * fg_ref[...]
    o_ref[0] = y


def _out_ffn(x, o_f, o_b, gate, ybc, mod, lw, final_g, tile, final_norm):
    bsz, t_len, d = x.shape
    d_ff = lw['ffn_w2'].shape[0]

    def tok(cols):
        return pl.BlockSpec((1, tile, cols), lambda b, i: (b, i, 0))

    params = [lw['onorm_g'], lw['w_out'], lw['norm2_g'], lw['ffn_w13'], lw['ffn_w2'], final_g.reshape(1, d)]
    return pl.pallas_call(
        functools.partial(_out_ffn_kernel, d_ff=d_ff, chunk=FFN_CHUNK, final_norm=final_norm),
        grid=(bsz, t_len // tile),
        in_specs=[tok(d), tok(A_WIDTH), tok(A_WIDTH), tok(A_WIDTH), tok(B_WIDTH + C_WIDTH), mod.spec()]
                 + [_param_spec(p) for p in params],
        out_specs=tok(d),
        out_shape=jax.ShapeDtypeStruct(x.shape, F32),
        compiler_params=_cparams(("arbitrary", "arbitrary")),
        name="out_ffn",
    )(x, o_f, o_b, gate, ybc, mod.rows, *[_param_arg(p) for p in params])


def _stacked_params(depth, norm1_g, hgrn_onorm_g, gmlp_ln_g, gmlp_ln_b, gmlp_b_s, conv_dw_w, conv_dw_b,
                    conv_ln_g, conv_ln_b, conv_pw_b, norm2_g):
    rows = lambda a: a.reshape(depth, 1, -1)
    return {
        'norm1_g': rows(norm1_g), 'onorm_g': rows(hgrn_onorm_g),
        'gmlp_ln_g': rows(gmlp_ln_g), 'gmlp_ln_b': rows(gmlp_ln_b),
        'bs_tile': jnp.repeat(jnp.swapaxes(gmlp_b_s, 1, 2), B_WIDTH // B_GROUPS, axis=2),
        'dw_w': conv_dw_w, 'dw_b': rows(conv_dw_b),
        'conv_ln_g': rows(conv_ln_g), 'conv_ln_b': rows(conv_ln_b),
        'pw_b': rows(conv_pw_b), 'norm2_g': rows(norm2_g),
    }


def kernel(x, c, ctx, c_ctx, ada_w, ada_b, norm1_g, w_in, hgrn_lb_logits, hgrn_onorm_g, gmlp_ln_g, gmlp_ln_b, gmlp_w_s, gmlp_b_s, conv_dw_w, conv_dw_b, conv_ln_g, conv_ln_b, conv_pw_w, conv_pw_b, w_out, norm2_g, ffn_w13, ffn_w2, final_norm_g):
    bsz, t_len, d = x.shape
    ctx_len = ctx.shape[1]
    depth = ada_w.shape[0]
    lb_logits = hgrn_lb_logits.astype(F32)

    cs = jnp.concatenate([c, c_ctx[None, :], jnp.zeros((MOD_ROWS - bsz - 1, d), F32)], axis=0)
    mod_all = _modulation(cs, ada_w, ada_b)
    grp = jnp.arange(B_WIDTH) // (B_WIDTH // B_GROUPS)
    gavg = ((grp[:, None] == grp[None, :]).astype(F32) / (B_WIDTH // B_GROUPS)).astype(BF16)
    zero_state = jnp.zeros((bsz, 2, A_HEADS, A_DK, A_DK), F32)
    mxu_weights = {
        'w_in': w_in, 'w_out': w_out,
        'pw_w': conv_pw_w.astype(BF16),
        'ws_cat': gmlp_w_s.reshape(depth, B_GROUPS * B_CHUNK, B_CHUNK).astype(BF16),
    }
    row_len = t_len // (t_len // GRID_W)
    ctx_flat = ctx.reshape(1, bsz * ctx_len, d)
    ctx_tile = min(LATENT_TILE, bsz * ctx_len)

    stacked = dict(mxu_weights)
    stacked.update(_stacked_params(depth, norm1_g, hgrn_onorm_g, gmlp_ln_g, gmlp_ln_b, gmlp_b_s, conv_dw_w,
                                   conv_dw_b, conv_ln_g, conv_ln_b, conv_pw_b, norm2_g))
    mod_rows = mod_all.reshape(depth, MOD_ROWS, 6, d)

    for l in range(depth):
        last = l == depth - 1
        lw = {name: _LayerSlice(w, l) for name, w in stacked.items()}
        lw['gavg'] = gavg
        mod = _ModRows(mod_rows, l)
        mod_ctx = _ModRows(mod_rows, l, bsz)
        if l == 0:
            ffn_f32 = (ffn_w13.reshape(-1, ffn_w13.shape[-1]), ffn_w2.reshape(-1, ffn_w2.shape[-1]))
            pa, gate, ybc, w13_bf16, w2_bf16 = _in_proj_mix(x, mod, lw, LATENT_TILE, row_len, ffn_f32)
            stacked['ffn_w13'] = w13_bf16.reshape(ffn_w13.shape)
            stacked['ffn_w2'] = w2_bf16.reshape(ffn_w2.shape)
        else:
            pa, gate, ybc = _in_proj_mix(x, mod, lw, LATENT_TILE, row_len)
        lw.update({name: _LayerSlice(stacked[name], l) for name in ('ffn_w13', 'ffn_w2')})
        if last:
            pa_c = _in_proj(ctx_flat, mod_ctx, lw, ctx_tile)
            _, _, state = _hgrn(pa_c.reshape(bsz, ctx_len, PA_COLS), lb_logits, zero_state, l, ctx_len)
        else:
            pa_c, gate_c, ybc_c = _in_proj_mix(ctx_flat, mod_ctx, lw, ctx_tile, ctx_len)
            of_c, ob_c, state = _hgrn(pa_c.reshape(bsz, ctx_len, PA_COLS), lb_logits, zero_state, l, ctx_len)
            ctx_flat = _out_ffn(ctx_flat, of_c.reshape(1, -1, A_WIDTH), ob_c.reshape(1, -1, A_WIDTH),
                                gate_c, ybc_c, mod_ctx, lw, final_norm_g, ctx_tile, False)
        o_f, o_b, _ = _hgrn(pa, lb_logits, state, l, SCAN_TILE)
        x = _out_ffn(x, o_f, o_b, gate, ybc, mod, lw, final_norm_g, LATENT_TILE, last)
    return x
```

```python
import functools
from typing import NamedTuple

import jax
import jax.numpy as jnp
from jax import lax
from jax.experimental import pallas as pl
from jax.experimental.pallas import tpu as pltpu

F32 = jnp.float32
BF16 = jnp.bfloat16

D_MODEL = 1024
GRID_W = 64
A_HEADS = 4
A_DK = 128
A_WIDTH = A_HEADS * A_DK
A_CHUNK = 64
A_SUB = 32
HGRN_GROUP = 1
HGRN_LAG = 1
B_GROUPS = 4
B_WIDTH = 256
B_CHUNK = 128
C_GROUPS = 4
C_WIDTH = 256
C_KERNEL = 31
C_PAD = 16
SUBLANES = 8
PROJ_CHUNK = 256
FFN_CHUNK = 256
PA_COLS = 4 * A_WIDTH
EPS = 1e-6
EXP2_CLAMP = 80.0
LOG2E = 1.4426950408889634
MOD_ROWS = 8
LATENT_TILE = 512
SCAN_TILE = 1024

V7X_VMEM_LIMIT = 56 * 1024 * 1024

_NT = (((1,), (1,)), ((), ()))
_TN = (((0,), (0,)), ((), ()))
_FINISHED = object()


def _cparams(sem):
    return pltpu.CompilerParams(dimension_semantics=sem, vmem_limit_bytes=V7X_VMEM_LIMIT)


def _full(shape):
    n = len(shape)
    return pl.BlockSpec(shape, lambda *_: (0,) * n)


class _LayerSlice(NamedTuple):
    stacked: jax.Array
    layer: int

    @property
    def shape(self):
        return self.stacked.shape[1:]


def _param_spec(p, block=None):
    if isinstance(p, _LayerRow):
        return _full(p.stacked.shape)
    if not isinstance(p, _LayerSlice):
        return _full(p.shape)
    block = p.shape if block is None else block
    index = (p.layer,) + (0,) * len(block)
    return pl.BlockSpec((None,) + tuple(block), lambda *_: index, pipeline_mode=pl.Buffered(1))


class _LayerRow(NamedTuple):
    stacked: jax.Array
    layer: int


class _RowOf:
    def __init__(self, ref, layer):
        self._ref, self._layer = ref, layer

    def __getitem__(self, idx):
        assert idx is Ellipsis
        return self._ref[self._layer:self._layer + 1, :]


def _param_arg(p):
    return p.stacked if isinstance(p, (_LayerSlice, _LayerRow)) else p


class _ModRows(NamedTuple):
    rows: jax.Array
    layer: int
    shared_row: int | None = None

    def spec(self):
        layer, row = self.layer, self.shared_row
        block = (None, 1) + self.rows.shape[2:]
        if row is None:
            return pl.BlockSpec(block, lambda b, i: (layer, b, 0, 0))
        return pl.BlockSpec(block, lambda b, i: (layer, row, 0, 0))


def _round_robin(branches, between=None):
    while branches:
        branches = [g for g in branches if next(g, _FINISHED) is not _FINISHED]
        if between is not None:
            between()


def _mod_kernel(cs_ref, w_ref, b_ref, o_ref):
    s = cs_ref[...]
    s = s * jax.nn.sigmoid(s)
    w = w_ref[0]
    s_hi = s.astype(BF16)
    s_lo = (s - s_hi.astype(F32)).astype(BF16)
    w_hi = w.astype(BF16)
    w_lo = (w - w_hi.astype(F32)).astype(BF16)
    r = jnp.dot(jnp.concatenate([s_hi, s_lo], axis=0), w_hi, preferred_element_type=F32)
    o_ref[0] = (r[:MOD_ROWS] + r[MOD_ROWS:] + jnp.dot(s_hi, w_lo, preferred_element_type=F32)
                + b_ref[0])


def _modulation(cs, ada_w, ada_b):
    depth, d, n = ada_w.shape
    tn = 2048
    return pl.pallas_call(
        _mod_kernel,
        grid=(depth, n // tn),
        in_specs=[
            pl.BlockSpec((MOD_ROWS, d), lambda l, j: (0, 0)),
            pl.BlockSpec((1, d, tn), lambda l, j: (l, 0, j)),
            pl.BlockSpec((1, 1, tn), lambda l, j: (l, 0, j)),
        ],
        out_specs=pl.BlockSpec((1, MOD_ROWS, tn), lambda l, j: (l, 0, j)),
        out_shape=jax.ShapeDtypeStruct((depth, MOD_ROWS, n), F32),
        compiler_params=_cparams(("arbitrary", "arbitrary")),
        name="modulation",
    )(cs, ada_w, ada_b.reshape(depth, 1, n))


def _norm_modulate(x, g, shift, scale):
    ms = jnp.mean(x * x, axis=-1, keepdims=True)
    return x * lax.rsqrt(ms + EPS) * (g * (1.0 + scale)) + shift


def _group_mean(x, gavg):
    return jnp.dot(x.astype(BF16), gavg, preferred_element_type=F32)


def _group_layer_norm_stages(x, gavg, g, b, out):
    mu = _group_mean(x, gavg)
    yield
    xc = x - mu
    var = _group_mean(xc * xc, gavg)
    yield
    out['ln'] = xc * lax.rsqrt(var + EPS) * g + b


def _gmlp_stages(project, gavg, ln_g, ln_b, ws_ref, bs_ref, out):
    u = project(0)
    v = project(B_WIDTH)
    tile = u.shape[0]
    yield
    ln = {}
    yield from _group_layer_norm_stages(jax.nn.gelu(v), gavg, ln_g, ln_b, ln)
    vn = ln['ln'].astype(BF16)
    gu = jax.nn.gelu(u)
    lane_group = lax.broadcasted_iota(jnp.int32, (B_CHUNK, B_WIDTH), 1) // (B_WIDTH // B_GROUPS)
    mixed = []
    for n in range(tile // B_CHUNK):
        yield
        r = jnp.dot(ws_ref[...].astype(BF16), vn[n * B_CHUNK:(n + 1) * B_CHUNK], preferred_element_type=F32)
        m = bs_ref[...]
        for g in range(B_GROUPS):
            m = m + jnp.where(lane_group == g, r[g * B_CHUNK:(g + 1) * B_CHUNK], 0.0)
        mixed.append(m)
    out['yb'] = gu * jnp.concatenate(mixed, axis=0)


def _conv_stages(ca, cgate, gavg, dww_ref, dw_b, ln_g, ln_b, pww_ref, pw_b, zpad_ref, zsh_ref, seg, out):
    tile = ca.shape[0]
    z = ca * jax.nn.sigmoid(cgate)
    nseg = tile // seg
    pad_zeros = jnp.zeros((C_PAD, C_WIDTH), F32)
    for s in range(nseg):
        zpad_ref[s, 0:C_PAD, :] = pad_zeros
        zpad_ref[s, C_PAD + seg:2 * C_PAD + seg, :] = pad_zeros
        zpad_ref[s, C_PAD:C_PAD + seg, :] = z[s * seg:(s + 1) * seg]
    span = seg + 2 * C_PAD - SUBLANES
    convs = []
    for s in range(nseg):
        for r in range(1, SUBLANES):
            zsh_ref[s, r] = zpad_ref[s, r:r + span, :]
        acc = jnp.zeros((seg, C_WIDTH), F32)
        for j in range(C_KERNEL):
            start = C_PAD - C_KERNEL // 2 + j
            r, a = start % SUBLANES, start - start % SUBLANES
            tap = zpad_ref[s, a:a + seg, :] if r == 0 else zsh_ref[s, r, a:a + seg, :]
            acc = acc + dww_ref[j:j + 1, :] * tap
        convs.append(acc)
        yield
    ln = {}
    yield from _group_layer_norm_stages(jnp.concatenate(convs, axis=0) + dw_b, gavg, ln_g, ln_b, ln)
    zc = ln['ln']
    zc = (zc * jax.nn.sigmoid(zc)).astype(BF16)
    out['yc'] = jnp.dot(zc, pww_ref[...].astype(BF16), preferred_element_type=F32) + pw_b


_MIX_PARAMS = ('norm1_g', 'w_in', 'gavg', 'gmlp_ln_g', 'gmlp_ln_b', 'ws_cat', 'bs_tile',
               'dw_w', 'dw_b', 'conv_ln_g', 'conv_ln_b', 'pw_w', 'pw_b')
_ROW_PARAMS = ('norm1_g', 'gmlp_ln_g', 'gmlp_ln_b', 'dw_b', 'conv_ln_g', 'conv_ln_b', 'pw_b')


def _in_proj_kernel(x_ref, mod_ref, g_ref, w_ref, pa_ref, *, layer):
    h = _norm_modulate(x_ref[0], g_ref[layer:layer + 1, :], mod_ref[0, 0:1, :], mod_ref[0, 1:2, :]).astype(BF16)
    pa_ref[0] = jnp.dot(h, w_ref[...].astype(BF16), preferred_element_type=F32)


def _in_proj(x, mod, lw, tile):
    bsz, t_len, d = x.shape
    return pl.pallas_call(
        functools.partial(_in_proj_kernel, layer=lw['norm1_g'].layer),
        grid=(bsz, t_len // tile),
        in_specs=[
            pl.BlockSpec((1, tile, d), lambda b, i: (b, i, 0)),
            mod.spec(),
            _param_spec(lw['norm1_g']),
            _param_spec(lw['w_in'], (d, PA_COLS)),
        ],
        out_specs=pl.BlockSpec((1, tile, PA_COLS), lambda b, i: (b, i, 0)),
        out_shape=jax.ShapeDtypeStruct((bsz, t_len, PA_COLS), F32),
        compiler_params=_cparams(("arbitrary", "arbitrary")),
        name="in_proj",
    )(x, mod.rows, _param_arg(lw['norm1_g']), _param_arg(lw['w_in']))


def _in_proj_mix_kernel(x_ref, mod_ref, *refs, seg, n_cast, layer):
    p = {name: _RowOf(ref, layer) if name in _ROW_PARAMS else ref for name, ref in zip(_MIX_PARAMS, refs)}
    refs = refs[len(_MIX_PARAMS):]
    cast_in, refs = refs[:n_cast], refs[n_cast:]
    pa_ref, gate_ref, ybc_ref = refs[:3]
    cast_out, (zpad_ref, zsh_ref) = refs[3:3 + n_cast], refs[3 + n_cast:]
    for src, dst in zip(cast_in, cast_out):
        dst[...] = src[...].astype(BF16)
    h = _norm_modulate(x_ref[0], p['norm1_g'][...], mod_ref[0, 0:1, :], mod_ref[0, 1:2, :]).astype(BF16)
    gavg = p['gavg'][...]

    def proj(lo, n):
        return jnp.dot(h, p['w_in'][:, lo:lo + n].astype(BF16), preferred_element_type=F32)

    chunks = [(pa_ref, lo) for lo in range(0, PA_COLS, PROJ_CHUNK)]
    chunks += [(gate_ref, lo) for lo in range(0, A_WIDTH, PROJ_CHUNK)]
    pending = iter(chunks)

    def project_next():
        nxt = next(pending, None)
        if nxt is not None:
            ref, lo = nxt
            w_lo = lo if ref is pa_ref else PA_COLS + lo
            ref[0, :, lo:lo + PROJ_CHUNK] = proj(w_lo, PROJ_CHUNK)

    b0 = PA_COLS + A_WIDTH
    c0 = b0 + 2 * B_WIDTH
    out = {}
    branches = [
        _conv_stages(proj(c0, C_WIDTH), proj(c0 + C_WIDTH, C_WIDTH), gavg, p['dw_w'], p['dw_b'][...],
                     p['conv_ln_g'][...], p['conv_ln_b'][...], p['pw_w'], p['pw_b'][...],
                     zpad_ref, zsh_ref, seg, out),
        _gmlp_stages(lambda lo: proj(b0 + lo, B_WIDTH), gavg, p['gmlp_ln_g'][...], p['gmlp_ln_b'][...],
                     p['ws_cat'], p['bs_tile'], out),
    ]
    _round_robin(branches, project_next)
    ybc_ref[0] = jnp.concatenate([out['yb'], out['yc']], axis=1).astype(BF16)
    for _ in chunks:
        project_next()


def _in_proj_mix(x, mod, lw, tile, seg, cast=()):
    bsz, t_len, d = x.shape
    nseg = tile // seg
    nb = t_len // tile

    def tok(cols):
        return pl.BlockSpec((1, tile, cols), lambda b, i: (b, i, 0))

    def cast_spec(w):
        assert w.shape[0] % (bsz * nb * 2 * SUBLANES) == 0, w.shape
        return pl.BlockSpec((w.shape[0] // (bsz * nb), w.shape[1]), lambda b, i: (b * nb + i, 0))

    params = [lw[name] for name in _MIX_PARAMS]
    return pl.pallas_call(
        functools.partial(_in_proj_mix_kernel, seg=seg, n_cast=len(cast), layer=lw['norm1_g'].layer),
        grid=(bsz, nb),
        in_specs=[tok(d), mod.spec()] + [_param_spec(p) for p in params] + [cast_spec(w) for w in cast],
        out_specs=[tok(PA_COLS), tok(A_WIDTH), tok(B_WIDTH + C_WIDTH)] + [cast_spec(w) for w in cast],
        out_shape=[jax.ShapeDtypeStruct((bsz, t_len, PA_COLS), F32),
                   jax.ShapeDtypeStruct((bsz, t_len, A_WIDTH), F32),
                   jax.ShapeDtypeStruct((bsz, t_len, B_WIDTH + C_WIDTH), BF16)]
                  + [jax.ShapeDtypeStruct(w.shape, BF16) for w in cast],
        scratch_shapes=[pltpu.VMEM((nseg, seg + 2 * C_PAD, C_WIDTH), F32),
                        pltpu.VMEM((nseg, SUBLANES, seg + 2 * C_PAD - SUBLANES, C_WIDTH), F32)],
        compiler_params=_cparams(("arbitrary", "arbitrary")),
        name="in_proj_mix",
    )(x, mod.rows, *[_param_arg(p) for p in params], *cast)


def _lower_bounds(logits, layer):
    depth = logits.shape[0]
    rows = [logits[j] for j in range(depth)]
    m = rows[0]
    for r in rows[1:]:
        m = jnp.maximum(m, r)
    es = [jnp.exp(r - m) for r in rows]
    tot = es[0]
    for e in es[1:]:
        tot = tot + e
    lb = jnp.zeros_like(m)
    for j in range(1, layer + 1):
        lb = lb + es[j] / tot
    return lb


def _chunk_rows(c, lo=0, n=A_CHUNK):
    return slice(c * A_CHUNK + lo, c * A_CHUNK + lo + n)


def _hgrn_gates(z, lb, rev, n_chunks):
    row = lax.broadcasted_iota(jnp.int32, (A_CHUNK, A_CHUNK), 0)
    col = lax.broadcasted_iota(jnp.int32, (A_CHUNK, A_CHUNK), 1)
    tri = ((col >= row) if rev else (col <= row)).astype(F32).astype(BF16)
    oml = 1.0 - lb
    t = oml * jax.nn.sigmoid(z)
    lf = jnp.log(lb + t)
    kk = oml - t
    hi = lf.astype(BF16)
    lo = (lf - hi.astype(F32)).astype(BF16)
    bs = [(jnp.dot(tri, hi[_chunk_rows(c)], preferred_element_type=F32)
           + jnp.dot(tri, lo[_chunk_rows(c)], preferred_element_type=F32)) * LOG2E
          for c in range(n_chunks)]
    return kk, bs


def _hgrn_factors(q, v, kk, bs, rev):
    sub = A_SUB
    wide, narrow = (0, 1) if rev else (1, 0)
    mid_row = sub // 2 if rev else sub // 2 - 1
    last_row = 0 if rev else A_CHUNK - 1
    halves = [slice(i * sub, (i + 1) * sub) for i in range(2)]
    per_chunk = {name: [] for name in ('qe', 'kd', 'qd', 'k_wide', 'k_narrow')}
    decs = []
    for c, bc in enumerate(bs):
        qc, kc = q[_chunk_rows(c)], kk[_chunk_rows(c)]
        mids = [bc[i * sub + mid_row:i * sub + mid_row + 1] for i in range(2)]
        last = bc[last_row:last_row + 1]
        decs.append(jnp.exp2(last))
        per_chunk['qe'].append((qc * jnp.exp2(bc)).astype(BF16))
        per_chunk['kd'].append((kc * jnp.exp2(last - bc)).astype(BF16))
        per_chunk['qd'].append(jnp.concatenate(
            [(qc[h] * jnp.exp2(jnp.minimum(bc[h] - mids[i], EXP2_CLAMP))).astype(BF16)
             for i, h in enumerate(halves)], axis=0))
        per_chunk['k_wide'].append((kc * jnp.exp2(jnp.minimum(mids[wide] - bc, EXP2_CLAMP))).astype(BF16))
        hn = halves[narrow]
        per_chunk['k_narrow'].append(
            (kc[hn] * jnp.exp2(jnp.minimum(mids[narrow] - bc[hn], EXP2_CLAMP))).astype(BF16))
    f = {name: jnp.concatenate(vals, axis=0) for name, vals in per_chunk.items()}
    f['dec'] = decs
    f['v'] = v.astype(BF16)
    return f


def _hgrn_scores(f, rev, c):
    sub = A_SUB
    wide, narrow = (0, 1) if rev else (1, 0)
    row_n = lax.broadcasted_iota(jnp.int32, (sub, sub), 0)
    col_n = lax.broadcasted_iota(jnp.int32, (sub, sub), 1)
    row_w = lax.broadcasted_iota(jnp.int32, (sub, A_CHUNK), 0)
    col_w = lax.broadcasted_iota(jnp.int32, (sub, A_CHUNK), 1)
    mask_n = (col_n >= row_n) if rev else (col_n <= row_n)
    mask_w = (col_w >= row_w) if rev else (col_w <= row_w + sub)
    per_head = []
    for h in range(A_HEADS):
        hc = slice(h * A_DK, (h + 1) * A_DK)
        s_n = lax.dot_general(f['qd'][_chunk_rows(c, narrow * sub, sub), hc],
                              f['k_narrow'][c * sub:(c + 1) * sub, hc], _NT, preferred_element_type=F32)
        s_w = lax.dot_general(f['qd'][_chunk_rows(c, wide * sub, sub), hc],
                              f['k_wide'][_chunk_rows(c), hc], _NT, preferred_element_type=F32)
        per_head.append((jnp.where(mask_n, s_n, 0.0).astype(BF16),
                         jnp.where(mask_w, s_w, 0.0).astype(BF16)))
    return per_head


def _hgrn_intra(f, scores, rev, c):
    sub = A_SUB
    narrow = 1 if rev else 0
    per_head = []
    for h in range(A_HEADS):
        hc = slice(h * A_DK, (h + 1) * A_DK)
        s_n, s_w = scores[h]
        o_n = jnp.dot(s_n, f['v'][_chunk_rows(c, narrow * sub, sub), hc], preferred_element_type=F32)
        o_w = jnp.dot(s_w, f['v'][_chunk_rows(c), hc], preferred_element_type=F32)
        intra = jnp.concatenate([o_w, o_n] if rev else [o_n, o_w], axis=0)
        upd = lax.dot_general(f['v'][_chunk_rows(c), hc], f['kd'][_chunk_rows(c), hc], _TN,
                              preferred_element_type=F32)
        per_head.append((intra, upd))
    return per_head


def _hgrn_stages(q, z, v, lb, rev, n_chunks, get_state, set_state, put_out, group=HGRN_GROUP, lag=HGRN_LAG):
    n_groups = n_chunks // group
    order = list(range(n_groups - 1, -1, -1) if rev else range(n_groups))
    facs, scores, intra = {}, {}, {}
    states = [get_state(h) for h in range(A_HEADS)]
    for step in range(n_groups + 3 * lag):
        if step < n_groups:
            g = order[step]
            rows = slice(g * group * A_CHUNK, (g + 1) * group * A_CHUNK)
            facs[g] = _hgrn_factors(q[rows], v[rows], *_hgrn_gates(z[rows], lb, rev, group), rev)
        if 0 <= step - lag < n_groups:
            g = order[step - lag]
            scores[g] = [_hgrn_scores(facs[g], rev, c) for c in range(group)]
        if 0 <= step - 2 * lag < n_groups:
            g = order[step - 2 * lag]
            intra[g] = [_hgrn_intra(facs[g], scores[g][c], rev, c) for c in range(group)]
        if 0 <= step - 3 * lag < n_groups:
            g = order[step - 3 * lag]
            f = facs[g]
            for c in (range(group - 1, -1, -1) if rev else range(group)):
                outs = []
                for h in range(A_HEADS):
                    hc = slice(h * A_DK, (h + 1) * A_DK)
                    inter = lax.dot_general(f['qe'][_chunk_rows(c), hc], states[h].astype(BF16), _NT,
                                            preferred_element_type=F32)
                    outs.append(inter + intra[g][c][h][0])
                    states[h] = states[h] * f['dec'][c][:, hc] + intra[g][c][h][1]
                put_out(g * group + c, jnp.concatenate(outs, axis=1))
        yield
    for h in range(A_HEADS):
        set_state(h, states[h])


def _hgrn_kernel(qf_ref, zf_ref, vf_ref, qb_ref, zb_ref, vb_ref, lbl_ref, s0_ref,
                 of_ref, ob_ref, st_ref, *, layer, n_chunks):
    @pl.when(pl.program_id(1) == 0)
    def _():
        st_ref[...] = s0_ref[...]

    lb = _lower_bounds(lbl_ref[...], layer)

    def direction(d, q_ref, z_ref, v_ref, o_ref):
        def set_state(h, val):
            st_ref[0, d, h] = val

        def put_out(c, val):
            o_ref[0, _chunk_rows(c), :] = val

        return _hgrn_stages(q_ref[0], z_ref[0], v_ref[0], lb[d:d + 1], d == 1, n_chunks,
                            lambda h: st_ref[0, d, h], set_state, put_out)

    _round_robin([direction(0, qf_ref, zf_ref, vf_ref, of_ref),
                  direction(1, qb_ref, zb_ref, vb_ref, ob_ref)])


def _hgrn(pa, lb_logits, s0, layer, tile):
    bsz, t_len, _ = pa.shape
    nb = t_len // tile
    depth = lb_logits.shape[0]

    def col_spec(colblk, rev):
        if rev:
            return pl.BlockSpec((1, tile, A_WIDTH), lambda b, i: (b, nb - 1 - i, colblk))
        return pl.BlockSpec((1, tile, A_WIDTH), lambda b, i: (b, i, colblk))

    st_spec = pl.BlockSpec((1, 2, A_HEADS, A_DK, A_DK), lambda b, i: (b, 0, 0, 0, 0))
    o_sds = jax.ShapeDtypeStruct((bsz, t_len, A_WIDTH), F32)
    return pl.pallas_call(
        functools.partial(_hgrn_kernel, layer=layer, n_chunks=tile // A_CHUNK),
        grid=(bsz, nb),
        in_specs=[col_spec(0, False), col_spec(1, False), col_spec(3, False),
                  col_spec(0, True), col_spec(2, True), col_spec(3, True),
                  _full((depth, 2, A_WIDTH)), st_spec],
        out_specs=[col_spec(0, False), col_spec(0, True), st_spec],
        out_shape=[o_sds, o_sds, jax.ShapeDtypeStruct(s0.shape, F32)],
        compiler_params=_cparams(("arbitrary", "arbitrary")),
        name="hgrn",
    )(pa, pa, pa, pa, pa, pa, lb_logits, s0)


def _out_ffn_kernel(x_ref, of_ref, ob_ref, gate_ref, ybc_ref, mod_ref, og_ref, wo_ref,
                    g_ref, w13_hbm, w2_hbm, fg_ref, o_ref, w13_ref, w2_ref, sem, *,
                    d_ff, chunk, final_norm, layer):
    @pl.when(jnp.logical_and(pl.program_id(0) == 0, pl.program_id(1) == 0))
    def _():
        copies = [pltpu.make_async_copy(w13_hbm.at[layer], w13_ref, sem.at[0]),
                  pltpu.make_async_copy(w2_hbm.at[layer], w2_ref, sem.at[1])]
        for cp in copies:
            cp.start()
        for cp in copies:
            cp.wait()

    og_ref, g_ref = _RowOf(og_ref, layer), _RowOf(g_ref, layer)
    o = of_ref[0] + ob_ref[0]
    parts = []
    for h in range(A_HEADS):
        oh = o[:, h * A_DK:(h + 1) * A_DK]
        parts.append(oh * lax.rsqrt(jnp.mean(oh * oh, axis=-1, keepdims=True) + EPS))
    gate = gate_ref[0]
    ya = jnp.concatenate(parts, axis=1) * og_ref[...] * (gate * jax.nn.sigmoid(gate))
    y = (jnp.dot(ya.astype(BF16), wo_ref[0:A_WIDTH, :].astype(BF16), preferred_element_type=F32)
         + jnp.dot(ybc_ref[0], wo_ref[A_WIDTH:, :].astype(BF16), preferred_element_type=F32))
    x = x_ref[0] + mod_ref[0, 2:3, :] * y
    h = _norm_modulate(x, g_ref[...], mod_ref[0, 3:4, :], mod_ref[0, 4:5, :]).astype(BF16)
    acts = []
    for c in range(d_ff // chunk):
        a = jnp.dot(h, w13_ref[:, c * chunk:(c + 1) * chunk], preferred_element_type=F32)
        b = jnp.dot(h, w13_ref[:, d_ff + c * chunk:d_ff + (c + 1) * chunk], preferred_element_type=F32)
        acts.append((a * jax.nn.sigmoid(a) * b).astype(BF16))
    acc = jnp.dot(jnp.concatenate(acts, axis=1), w2_ref[...], preferred_element_type=F32)
    y = x + mod_ref[0, 5:6, :] * acc
    if final_norm:
        y = y * lax.rsqrt(jnp.mean(y * y, axis=-1, keepdims=True) + EPS) * fg_ref[...]
    o_ref[0] = y


def _out_ffn(x, o_f, o_b, gate, ybc, mod, lw, final_g, tile, final_norm):
    bsz, t_len, d = x.shape
    d_ff = lw['ffn_w2'].shape[0]

    def tok(cols):
        return pl.BlockSpec((1, tile, cols), lambda b, i: (b, i, 0))

    w13, w2 = lw['ffn_w13'], lw['ffn_w2']
    assert w13.layer == w2.layer == lw['norm2_g'].layer and d_ff % FFN_CHUNK == 0
    params = [lw['onorm_g'], lw['w_out'], lw['norm2_g']]
    hbm = pl.BlockSpec(memory_space=pl.ANY)
    return pl.pallas_call(
        functools.partial(_out_ffn_kernel, d_ff=d_ff, chunk=FFN_CHUNK, final_norm=final_norm,
                          layer=lw['norm2_g'].layer),
        grid=(bsz, t_len // tile),
        in_specs=[tok(d), tok(A_WIDTH), tok(A_WIDTH), tok(A_WIDTH), tok(B_WIDTH + C_WIDTH), mod.spec()]
                 + [_param_spec(p) for p in params] + [hbm, hbm, _full((1, d))],
        out_specs=tok(d),
        out_shape=jax.ShapeDtypeStruct(x.shape, F32),
        scratch_shapes=[pltpu.VMEM(w13.shape, BF16), pltpu.VMEM(w2.shape, BF16),
                        pltpu.SemaphoreType.DMA((2,))],
        compiler_params=_cparams(("arbitrary", "arbitrary")),
        name="out_ffn",
    )(x, o_f, o_b, gate, ybc, mod.rows, *[_param_arg(p) for p in params], w13.stacked, w2.stacked,
      final_g.reshape(1, d))


def _stacked_params(depth, norm1_g, hgrn_onorm_g, gmlp_ln_g, gmlp_ln_b, gmlp_b_s, conv_dw_w, conv_dw_b,
                    conv_ln_g, conv_ln_b, conv_pw_b, norm2_g):
    return {
        'norm1_g': norm1_g, 'onorm_g': hgrn_onorm_g, 'gmlp_ln_g': gmlp_ln_g, 'gmlp_ln_b': gmlp_ln_b,
        'bs_tile': jnp.repeat(jnp.swapaxes(gmlp_b_s, 1, 2), B_WIDTH // B_GROUPS, axis=2),
        'dw_w': conv_dw_w, 'dw_b': conv_dw_b, 'conv_ln_g': conv_ln_g, 'conv_ln_b': conv_ln_b,
        'pw_b': conv_pw_b, 'norm2_g': norm2_g,
    }


def kernel(x, c, ctx, c_ctx, ada_w, ada_b, norm1_g, w_in, hgrn_lb_logits, hgrn_onorm_g, gmlp_ln_g, gmlp_ln_b, gmlp_w_s, gmlp_b_s, conv_dw_w, conv_dw_b, conv_ln_g, conv_ln_b, conv_pw_w, conv_pw_b, w_out, norm2_g, ffn_w13, ffn_w2, final_norm_g):
    bsz, t_len, d = x.shape
    ctx_len = ctx.shape[1]
    depth = ada_w.shape[0]
    lb_logits = hgrn_lb_logits.astype(F32)

    cs = jnp.concatenate([c, c_ctx[None, :], jnp.zeros((MOD_ROWS - bsz - 1, d), F32)], axis=0)
    mod_all = _modulation(cs, ada_w, ada_b)
    grp = jnp.arange(B_WIDTH) // (B_WIDTH // B_GROUPS)
    gavg = ((grp[:, None] == grp[None, :]).astype(F32) / (B_WIDTH // B_GROUPS)).astype(BF16)
    zero_state = jnp.zeros((bsz, 2, A_HEADS, A_DK, A_DK), F32)
    mxu_weights = {
        'w_in': w_in, 'w_out': w_out,
        'pw_w': conv_pw_w,
        'ws_cat': gmlp_w_s.reshape(depth, B_GROUPS * B_CHUNK, B_CHUNK),
    }
    row_len = t_len // (t_len // GRID_W)
    ctx_flat = ctx.reshape(1, bsz * ctx_len, d)
    ctx_tile = min(LATENT_TILE, bsz * ctx_len)
    assert d == D_MODEL and bsz + 1 <= MOD_ROWS
    assert t_len % SCAN_TILE == 0 and t_len % LATENT_TILE == 0 and LATENT_TILE % row_len == 0
    assert row_len % SUBLANES == 0 and LATENT_TILE % B_CHUNK == 0 and ctx_len % B_CHUNK == 0
    assert (bsz * ctx_len) % ctx_tile == 0 and ctx_tile % ctx_len == 0 and ctx_len % A_CHUNK == 0

    stacked = dict(mxu_weights)
    stacked.update(_stacked_params(depth, norm1_g, hgrn_onorm_g, gmlp_ln_g, gmlp_ln_b, gmlp_b_s, conv_dw_w,
                                   conv_dw_b, conv_ln_g, conv_ln_b, conv_pw_b, norm2_g))
    mod_rows = mod_all.reshape(depth, MOD_ROWS, 6, d)

    for l in range(depth):
        last = l == depth - 1
        lw = {name: (_LayerRow if w.ndim == 2 else _LayerSlice)(w, l) for name, w in stacked.items()}
        lw['gavg'] = gavg
        mod = _ModRows(mod_rows, l)
        mod_ctx = _ModRows(mod_rows, l, bsz)
        if l == 0:
            ffn_f32 = (ffn_w13.reshape(-1, ffn_w13.shape[-1]), ffn_w2.reshape(-1, ffn_w2.shape[-1]))
            pa, gate, ybc, w13_bf16, w2_bf16 = _in_proj_mix(x, mod, lw, LATENT_TILE, row_len, ffn_f32)
            stacked['ffn_w13'] = w13_bf16.reshape(ffn_w13.shape)
            stacked['ffn_w2'] = w2_bf16.reshape(ffn_w2.shape)
        else:
            pa, gate, ybc = _in_proj_mix(x, mod, lw, LATENT_TILE, row_len)
        lw.update({name: _LayerSlice(stacked[name], l) for name in ('ffn_w13', 'ffn_w2')})
        if last:
            pa_c = _in_proj(ctx_flat, mod_ctx, lw, ctx_tile)
            _, _, state = _hgrn(pa_c.reshape(bsz, ctx_len, PA_COLS), lb_logits, zero_state, l, ctx_len)
        else:
            pa_c, gate_c, ybc_c = _in_proj_mix(ctx_flat, mod_ctx, lw, ctx_tile, ctx_len)
            of_c, ob_c, state = _hgrn(pa_c.reshape(bsz, ctx_len, PA_COLS), lb_logits, zero_state, l, ctx_len)
            ctx_flat = _out_ffn(ctx_flat, of_c.reshape(1, -1, A_WIDTH), ob_c.reshape(1, -1, A_WIDTH),
                                gate_c, ybc_c, mod_ctx, lw, final_norm_g, ctx_tile, False)
        o_f, o_b, _ = _hgrn(pa, lb_logits, state, l, SCAN_TILE)
        x = _out_ffn(x, o_f, o_b, gate, ybc, mod, lw, final_norm_g, LATENT_TILE, last)
    return x
```
